```python
import jax
import jax.numpy as jnp
from jax import lax
import numpy as np

D_MODEL = 2048
BATCH = 2
SEQ = 4096
DEPTH = 1
DEC_BATCH = 128
DEC_SEQ = 1
PAST_LEN = 16384
PAGE_SIZE = 128

HEAD_DIM = 64
N_HEADS_A = D_MODEL // (2 * HEAD_DIM)
N_HEADS_B = D_MODEL // (2 * HEAD_DIM)
N_KV_B = N_HEADS_B // 8
DILATIONS = ((128, 1), (512, 4), (2048, 16))
WINDOW_A = max(w for w, _ in DILATIONS)
WINDOW_B = 128
BLOCK = 128
N_EXPERTS = 32
TOP_K = 4
D_FF = D_MODEL
SWIGLU_ALPHA = 1.702
SWIGLU_LIMIT = 7.0
PLE_DIM = 256
ROPE_THETA = 10000.0
NORM_EPS = 1e-5
MASK_VALUE = -1e30
WIDTH_A = N_HEADS_A * HEAD_DIM
WIDTH_B = N_HEADS_B * HEAD_DIM
KV_WIDTH_B = N_KV_B * HEAD_DIM
MIX_WIDTH = WIDTH_A + WIDTH_B
IN_SPLITS = (WIDTH_A, WIDTH_A, WIDTH_A, WIDTH_B, KV_WIDTH_B, KV_WIDTH_B)
IN_COLS = sum(IN_SPLITS)
SCALE = HEAD_DIM ** -0.5

kernel_name = 'hybrid_dilated_swa_moe_step'


def rmsnorm(x, g):
    xf = x.astype(jnp.float32)
    y = xf * lax.rsqrt(jnp.mean(xf * xf, axis=-1, keepdims=True) + NORM_EPS)
    return (y * g.astype(jnp.float32)).astype(x.dtype)


def rope(x, pos):
    half = HEAD_DIM // 2
    inv = ROPE_THETA ** (-jnp.arange(half, dtype=jnp.float32) / half)
    ang = pos.astype(jnp.float32)[:, None] * inv[None, :]
    cos = jnp.cos(ang)[None, :, None, :]
    sin = jnp.sin(ang)[None, :, None, :]
    xf = x.astype(jnp.float32)
    x1, x2 = xf[..., :half], xf[..., half:]
    return jnp.concatenate([x1 * cos - x2 * sin, x2 * cos + x1 * sin], axis=-1).astype(x.dtype)


def qkv_heads(h, w_in, b_in, pos):
    B, L, _ = h.shape
    proj = jnp.einsum('bld,dc->blc', h, w_in) + b_in
    qa, ka, va, qb, kb, vb = jnp.split(proj, list(np.cumsum(IN_SPLITS)[:-1]), axis=-1)
    qa = rope(qa.reshape(B, L, N_HEADS_A, HEAD_DIM), pos)
    ka = rope(ka.reshape(B, L, N_HEADS_A, HEAD_DIM), pos)
    va = va.reshape(B, L, N_HEADS_A, HEAD_DIM)
    qb = rope(qb.reshape(B, L, N_HEADS_B, HEAD_DIM), pos)
    kb = rope(kb.reshape(B, L, N_KV_B, HEAD_DIM), pos)
    vb = vb.reshape(B, L, N_KV_B, HEAD_DIM)
    return qa, ka, va, qb, kb, vb


def banded_attention(q, k, v, band):
    N, L, H, hd = q.shape
    Hkv = k.shape[2]
    G = H // Hkv
    nb = -(-L // BLOCK)
    pad = nb * BLOCK - L
    padseq = lambda t, front, back: jnp.pad(t, ((0, 0), (front, back), (0, 0), (0, 0)))
    qb = padseq(q, 0, pad).reshape(N, nb, BLOCK, Hkv, G, hd)
    kp = padseq(k, BLOCK, pad).reshape(N, nb + 1, BLOCK, Hkv, hd)
    vp = padseq(v, BLOCK, pad).reshape(N, nb + 1, BLOCK, Hkv, hd)
    kb = jnp.concatenate([kp[:, :-1], kp[:, 1:]], axis=2)
    vb = jnp.concatenate([vp[:, :-1], vp[:, 1:]], axis=2)
    s = jnp.einsum('nbqkgd,nbskd->nbkgqs', qb, kb).astype(jnp.float32) * SCALE
    qi = jnp.arange(BLOCK)[:, None]
    si = jnp.arange(2 * BLOCK)[None, :]
    dist = qi - si + BLOCK
    blk = jnp.arange(nb)[:, None, None]
    valid = (dist >= 0) & (dist <= band) & ((blk > 0) | (si >= BLOCK))
    s = jnp.where(valid[None, :, None, None], s, MASK_VALUE)
    lse = jax.nn.logsumexp(s, axis=-1)
    p = jnp.exp(s - lse[..., None]).astype(v.dtype)
    o = jnp.einsum('nbkgqs,nbskd->nbqkgd', p, vb).reshape(N, nb * BLOCK, H, hd)[:, :L]
    lse = lse.transpose(0, 1, 4, 2, 3).reshape(N, nb * BLOCK, H)[:, :L]
    return o, lse


def mix_dilations(outs, lses):
    wts = jax.nn.softmax(jnp.stack(lses, axis=0), axis=0)
    return jnp.einsum('pblh,pblhd->blhd', wts.astype(outs[0].dtype), jnp.stack(outs, axis=0))


def dilated_attention_prompt(q, k, v):
    B, S, H, hd = q.shape
    outs, lses = [], []
    for window, d in DILATIONS:
        L = S // d
        to_streams = lambda t: t.reshape(B, L, d, H, hd).transpose(0, 2, 1, 3, 4).reshape(B * d, L, H, hd)
        o, lse = banded_attention(to_streams(q), to_streams(k), to_streams(v), window // d)
        outs.append(o.reshape(B, d, L, H, hd).transpose(0, 2, 1, 3, 4).reshape(B, S, H, hd))
        lses.append(lse.reshape(B, d, L, H).transpose(0, 2, 1, 3).reshape(B, S, H))
    return mix_dilations(outs, lses)


def dilated_attention_sample(q, k_new, v_new, k_buf, v_buf):
    n = q.shape[1]
    Wb = k_buf.shape[1]
    k_all = jnp.concatenate([k_buf, k_new], axis=1)
    v_all = jnp.concatenate([v_buf, v_new], axis=1)
    j = jnp.arange(n)
    outs, lses = [], []
    for window, d in DILATIONS:
        steps = jnp.arange(window // d + 1) * d
        row = Wb + j[:, None] - steps[None, :]
        valid = row >= 0
        row = jnp.maximum(row, 0)
        kg = k_all[:, row]
        vg = v_all[:, row]
        s = jnp.einsum('bqhd,bqkhd->bqhk', q, kg).astype(jnp.float32) * SCALE
        s = jnp.where(valid[None, :, None, :], s, MASK_VALUE)
        lse = jax.nn.logsumexp(s, axis=-1)
        p = jnp.exp(s - lse[..., None]).astype(vg.dtype)
        outs.append(jnp.einsum('bqhk,bqkhd->bqhd', p, vg))
        lses.append(lse)
    return mix_dilations(outs, lses)


def sink_window_attention_prompt(q, k, v, sinks):
    o, lse = banded_attention(q, k, v, WINDOW_B)
    return o * jax.nn.sigmoid(lse - sinks.astype(jnp.float32))[..., None].astype(o.dtype)


def sink_window_attention_sample(q, k_new, v_new, k_buf, v_buf, sinks):
    Bd, n, H, hd = q.shape
    Wb = k_buf.shape[1]
    G = H // N_KV_B
    k_all = jnp.concatenate([k_buf, k_new], axis=1)
    v_all = jnp.concatenate([v_buf, v_new], axis=1)
    dist = jnp.arange(n)[:, None] + Wb - jnp.arange(Wb + n)[None, :]
    valid = (dist >= 0) & (dist <= WINDOW_B)
    qg = q.reshape(Bd, n, N_KV_B, G, hd)
    s = jnp.einsum('bqkgd,bskd->bkgqs', qg, k_all).astype(jnp.float32) * SCALE
    s = jnp.where(valid[None, None, None], s, MASK_VALUE)
    lse = jax.nn.logsumexp(s, axis=-1)
    p = jnp.exp(s - lse[..., None]).astype(v_all.dtype)
    o = jnp.einsum('bkgqs,bskd->bqkgd', p, v_all).reshape(Bd, n, H, hd)
    lse = lse.transpose(0, 3, 1, 2).reshape(Bd, n, H)
    return o * jax.nn.sigmoid(lse - sinks.astype(jnp.float32))[..., None].astype(o.dtype)


def moe_ffn(h, w_router, b_router, w_gate_up, b_gate_up, w_down, b_down):
    lead = h.shape[:-1]
    hf = h.reshape(-1, D_MODEL)
    T = hf.shape[0]
    logits = (hf @ w_router + b_router).astype(jnp.float32)
    top_val, top_idx = lax.top_k(logits, TOP_K)
    gate = jax.nn.softmax(top_val, axis=-1)
    M = T * TOP_K
    flat_e = top_idx.reshape(-1)
    flat_tok = jnp.repeat(jnp.arange(T, dtype=jnp.int32), TOP_K)
    flat_g = gate.reshape(-1)
    order = jnp.argsort(flat_e)
    sorted_e = flat_e[order]
    counts = jnp.bincount(flat_e, length=N_EXPERTS)
    padded = ((counts + BLOCK - 1) // BLOCK) * BLOCK
    pad_end = jnp.cumsum(padded)
    pad_start = pad_end - padded
    start = jnp.cumsum(counts) - counts
    dest = pad_start[sorted_e] + jnp.arange(M) - start[sorted_e]
    n_blk = -(-M // BLOCK) + N_EXPERTS
    P = n_blk * BLOCK
    slot_tok = jnp.full((P,), T, jnp.int32).at[dest].set(flat_tok[order])
    slot_gate = jnp.zeros((P,), jnp.float32).at[dest].set(flat_g[order])
    blk_e = jnp.minimum(jnp.searchsorted(pad_end, jnp.arange(n_blk) * BLOCK, side='right'), N_EXPERTS - 1)
    h_pad = jnp.concatenate([hf, jnp.zeros((1, D_MODEL), hf.dtype)], axis=0)
    xb = h_pad[slot_tok].reshape(n_blk, BLOCK, D_MODEL)

    def expert_block(args):
        xblk, e = args
        gu = xblk @ w_gate_up[e] + b_gate_up[e]
        g, u = gu[:, :D_FF], gu[:, D_FF:]
        g = jnp.minimum(g, SWIGLU_LIMIT)
        u = jnp.clip(u, -SWIGLU_LIMIT, SWIGLU_LIMIT)
        act = g * jax.nn.sigmoid(SWIGLU_ALPHA * g) * (u + 1)
        return act @ w_down[e] + b_down[e]

    yb = lax.map(expert_block, (xb, blk_e)).reshape(P, D_MODEL)
    y = jnp.zeros((T + 1, D_MODEL), hf.dtype).at[slot_tok].add(yb * slot_gate[:, None].astype(hf.dtype))
    return y[:T].reshape(*lead, D_MODEL)


def decoder_layer(x, p_l, pos, cache, lw):
    B, L, _ = x.shape
    h = rmsnorm(x, lw['g_attn'])
    qa, ka, va, qb, kb, vb = qkv_heads(h, lw['w_in'], lw['b_in'], pos)
    if cache is None:
        oa = dilated_attention_prompt(qa, ka, va)
        ob = sink_window_attention_prompt(qb, kb, vb, lw['sinks'])
        na, nbw = min(WINDOW_A, L), min(WINDOW_B, L)
        rows = (ka[:, L - na:], va[:, L - na:], kb[:, L - nbw:], vb[:, L - nbw:])
    else:
        ck_a, cv_a, ck_b, cv_b = cache
        oa = dilated_attention_sample(qa, ka, va, ck_a, cv_a)
        ob = sink_window_attention_sample(qb, kb, vb, ck_b, cv_b, lw['sinks'])
        rows = (ka, va, kb, vb)
    oa = rmsnorm(oa.reshape(B, L, WIDTH_A), lw['g_out_a'])
    ob = rmsnorm(ob.reshape(B, L, WIDTH_B), lw['g_out_b'])
    x = x + jnp.einsum('blc,cd->bld', jnp.concatenate([oa, ob], axis=-1), lw['w_out']) + lw['b_out']
    x = x + moe_ffn(rmsnorm(x, lw['g_ffn']), lw['w_router'], lw['b_router'], lw['w_gate_up'],
                    lw['b_gate_up'], lw['w_down'], lw['b_down'])
    gate = jax.nn.sigmoid(jnp.einsum('bld,de->ble', rmsnorm(x, lw['g_ple']), lw['w_ple_gate']))
    x = x + gate * jnp.einsum('blp,pd->bld', p_l, lw['w_ple_proj'])
    return x, rows


def setup_inputs(seed: int = 0) -> dict:
    key = jax.random.key(seed)
    ks = jax.random.split(key, 32)
    nrm = lambda k, shape, scale: jax.random.normal(k, shape, jnp.float32) * scale
    gain = lambda k, shape: 1.0 + 0.05 * jax.random.normal(k, shape, jnp.float32)
    wa = min(WINDOW_A, PAST_LEN)
    wb = min(WINDOW_B, PAST_LEN)
    return {
        'x_prompt': nrm(ks[0], (BATCH, SEQ, D_MODEL), 1.0),
        'x_sample': nrm(ks[1], (DEC_BATCH, DEC_SEQ, D_MODEL), 1.0),
        'cache_a_k': nrm(ks[2], (DEPTH, DEC_BATCH, wa, N_HEADS_A, HEAD_DIM), 1.0),
        'cache_a_v': nrm(ks[3], (DEPTH, DEC_BATCH, wa, N_HEADS_A, HEAD_DIM), 1.0),
        'cache_b_k': nrm(ks[4], (DEPTH, DEC_BATCH, wb, N_KV_B, HEAD_DIM), 1.0),
        'cache_b_v': nrm(ks[5], (DEPTH, DEC_BATCH, wb, N_KV_B, HEAD_DIM), 1.0),
        'p_prompt': nrm(ks[6], (DEPTH, BATCH, SEQ, PLE_DIM), 1.0),
        'p_sample': nrm(ks[7], (DEPTH, DEC_BATCH, DEC_SEQ, PLE_DIM), 1.0),
        'g_attn': gain(ks[8], (DEPTH, D_MODEL)),
        'w_in': nrm(ks[9], (DEPTH, D_MODEL, IN_COLS), D_MODEL ** -0.5),
        'b_in': nrm(ks[10], (DEPTH, IN_COLS), 0.02),
        'sinks': 2.0 + nrm(ks[11], (DEPTH, N_HEADS_B), 1.0),
        'g_out_a': gain(ks[12], (DEPTH, WIDTH_A)),
        'g_out_b': gain(ks[13], (DEPTH, WIDTH_B)),
        'w_out': nrm(ks[14], (DEPTH, MIX_WIDTH, D_MODEL), MIX_WIDTH ** -0.5),
        'b_out': nrm(ks[15], (DEPTH, D_MODEL), 0.02),
        'g_ffn': gain(ks[16], (DEPTH, D_MODEL)),
        'w_router': nrm(ks[17], (DEPTH, D_MODEL, N_EXPERTS), D_MODEL ** -0.5),
        'b_router': nrm(ks[18], (DEPTH, N_EXPERTS), 0.01),
        'w_gate_up': nrm(ks[19], (DEPTH, N_EXPERTS, D_MODEL, 2 * D_FF), D_MODEL ** -0.5),
        'b_gate_up': nrm(ks[20], (DEPTH, N_EXPERTS, 2 * D_FF), 0.02),
        'w_down': nrm(ks[21], (DEPTH, N_EXPERTS, D_FF, D_MODEL), D_FF ** -0.5),
        'b_down': nrm(ks[22], (DEPTH, N_EXPERTS, D_MODEL), 0.02),
        'g_ple': gain(ks[23], (DEPTH, D_MODEL)),
        'w_ple_gate': nrm(ks[24], (DEPTH, D_MODEL, D_MODEL), D_MODEL ** -0.5),
        'w_ple_proj': nrm(ks[25], (DEPTH, PLE_DIM, D_MODEL), PLE_DIM ** -0.5),
        'g_final': gain(ks[26], (D_MODEL,)),
    }


def reference(x_prompt, x_sample, cache_a_k, cache_a_v, cache_b_k, cache_b_v, p_prompt, p_sample,
              g_attn, w_in, b_in, sinks, g_out_a, g_out_b, w_out, b_out, g_ffn, w_router, b_router,
              w_gate_up, b_gate_up, w_down, b_down, g_ple, w_ple_gate, w_ple_proj, g_final):
    pos_prompt = jnp.arange(x_prompt.shape[1], dtype=jnp.int32)
    pos_sample = PAST_LEN + jnp.arange(x_sample.shape[1], dtype=jnp.int32)
    xp, xs = x_prompt, x_sample
    rows_p = [[], [], [], []]
    rows_s = [[], [], [], []]
    for i in range(DEPTH):
        lw = dict(g_attn=g_attn[i], w_in=w_in[i], b_in=b_in[i], sinks=sinks[i], g_out_a=g_out_a[i],
                  g_out_b=g_out_b[i], w_out=w_out[i], b_out=b_out[i], g_ffn=g_ffn[i],
                  w_router=w_router[i], b_router=b_router[i], w_gate_up=w_gate_up[i],
                  b_gate_up=b_gate_up[i], w_down=w_down[i], b_down=b_down[i], g_ple=g_ple[i],
                  w_ple_gate=w_ple_gate[i], w_ple_proj=w_ple_proj[i])
        xp, rp = decoder_layer(xp, p_prompt[i], pos_prompt, None, lw)
        xs, rs = decoder_layer(xs, p_sample[i], pos_sample,
                               (cache_a_k[i], cache_a_v[i], cache_b_k[i], cache_b_v[i]), lw)
        for j in range(4):
            rows_p[j].append(rp[j])
            rows_s[j].append(rs[j])
    y_prompt = rmsnorm(xp, g_final)
    y_sample = rmsnorm(xs, g_final)
    a_k_prompt, a_v_prompt, b_k_prompt, b_v_prompt = [jnp.stack(r, axis=0) for r in rows_p]
    a_k_sample, a_v_sample, b_k_sample, b_v_sample = [jnp.stack(r, axis=0) for r in rows_s]
    return (y_prompt, y_sample, a_k_prompt, a_v_prompt, b_k_prompt, b_v_prompt,
            a_k_sample, a_v_sample, b_k_sample, b_v_sample)
```

```python
import functools

import jax
import jax.numpy as jnp
from jax import lax
from jax.experimental import pallas as pl
from jax.experimental.pallas import tpu as pltpu

F32 = jnp.float32
BF16 = jnp.bfloat16
I32 = jnp.int32

D_MODEL = 2048
HEAD_DIM = 64
N_HEADS = 16
WIDTH = N_HEADS * HEAD_DIM
N_KV_B = 2
KV_WIDTH_B = N_KV_B * HEAD_DIM
DILATIONS = ((128, 1), (512, 4), (2048, 16))
WINDOW_B = 128
BLOCK = 128
N_EXPERTS = 32
TOP_K = 4
D_FF = 2048
SWIGLU_ALPHA = 1.702
SWIGLU_LIMIT = 7.0
PLE_DIM = 256
ROPE_THETA = 10000.0
NORM_EPS = 1e-5
MASK_VALUE = -1e30
SCALE = HEAD_DIM ** -0.5
PAST_LEN = 16384
QKV4_COLS = 4 * WIDTH
IN_COLS = QKV4_COLS + 2 * KV_WIDTH_B

LANES = 128
VMEM_LIMIT_CAP = 60 * 1024 * 1024

MOE_TF = 256
MOE_F = D_FF // MOE_TF
MOE_SUB = 128
MOE_MAX_SUB = 16
DISPATCH_ROWS = 256
TAIL_ROWS = 128


def _cparams(semantics, vmem_mb):
    return pltpu.CompilerParams(
        dimension_semantics=semantics,
        vmem_limit_bytes=min(vmem_mb * 1024 * 1024, VMEM_LIMIT_CAP))


def _rmsnorm(x, g):
    ms = jnp.mean(x * x, axis=-1, keepdims=True)
    return x * lax.rsqrt(ms + NORM_EPS) * g


def _cast_kernel(x_ref, o_ref):
    o_ref[...] = x_ref[...].astype(o_ref.dtype)


def _cast_bf16(w, rows_per_step=256):
    r, c = w.shape
    return pl.pallas_call(
        _cast_kernel,
        out_shape=jax.ShapeDtypeStruct((r, c), BF16),
        grid=(r // rows_per_step,),
        in_specs=[pl.BlockSpec((rows_per_step, c), lambda i: (i, 0))],
        out_specs=pl.BlockSpec((rows_per_step, c), lambda i: (i, 0)),
        compiler_params=_cparams(("parallel",), 32),
        name="cast_bf16",
    )(w)


def _rope_store(dst_ref, y, cos, sin_lo, sin_hi):
    for c in range(y.shape[1] // LANES):
        yc = y[:, c * LANES:(c + 1) * LANES]
        dst_ref[:, c * LANES:(c + 1) * LANES] = (
            yc * cos + pltpu.roll(yc, LANES - 32, 1) * sin_lo + pltpu.roll(yc, 32, 1) * sin_hi)


def _qkv_kernel(x_ref, g_ref, w_ref, wkv_ref, b_ref, bkv_ref, cos_ref, slo_ref, shi_ref,
                o4_ref, okv_ref, h_scr):
    j = pl.program_id(1)

    @pl.when(j == 0)
    def _():
        h_scr[...] = _rmsnorm(x_ref[...], g_ref[...]).astype(BF16)

    h = h_scr[...]
    y = jnp.dot(h, w_ref[...], preferred_element_type=F32) + b_ref[...]
    cos, slo, shi = cos_ref[...], slo_ref[...], shi_ref[...]

    @pl.when(j != 2)
    def _():
        _rope_store(o4_ref, y, cos, slo, shi)

    @pl.when(j == 2)
    def _():
        o4_ref[...] = y

    @pl.when(j == 3)
    def _():
        ykv = jnp.dot(h, wkv_ref[...], preferred_element_type=F32) + bkv_ref[...]
        _rope_store(okv_ref.at[:, :KV_WIDTH_B], ykv[:, :KV_WIDTH_B], cos, slo, shi)
        okv_ref[:, KV_WIDTH_B:] = ykv[:, KV_WIDTH_B:]


def _qkv_proj(x, g_attn, w_in_bf, b_in, rope_tabs, tm, tab_blocks):
    t = x.shape[0]
    cos, slo, shi = rope_tabs
    tab_map = lambda i, j: (i % tab_blocks, 0)
    return pl.pallas_call(
        _qkv_kernel,
        out_shape=(jax.ShapeDtypeStruct((t, QKV4_COLS), F32),
                   jax.ShapeDtypeStruct((t, 2 * KV_WIDTH_B), F32)),
        grid=(t // tm, 4),
        in_specs=[
            pl.BlockSpec((tm, D_MODEL), lambda i, j: (i, 0)),
            pl.BlockSpec((1, D_MODEL), lambda i, j: (0, 0)),
            pl.BlockSpec((D_MODEL, WIDTH), lambda i, j: (0, j)),
            pl.BlockSpec((D_MODEL, 2 * KV_WIDTH_B), lambda i, j: (0, QKV4_COLS // (2 * KV_WIDTH_B))),
            pl.BlockSpec((1, WIDTH), lambda i, j: (0, j)),
            pl.BlockSpec((1, 2 * KV_WIDTH_B), lambda i, j: (0, QKV4_COLS // (2 * KV_WIDTH_B))),
            pl.BlockSpec((tm, LANES), tab_map),
            pl.BlockSpec((tm, LANES), tab_map),
            pl.BlockSpec((tm, LANES), tab_map),
        ],
        out_specs=(pl.BlockSpec((tm, WIDTH), lambda i, j: (i, j)),
                   pl.BlockSpec((tm, 2 * KV_WIDTH_B), lambda i, j: (i, 0))),
        scratch_shapes=[pltpu.VMEM((tm, D_MODEL), BF16)],
        compiler_params=_cparams(("parallel", "arbitrary"), 48),
        name="qkv_proj",
    )(x, g_attn, w_in_bf, w_in_bf, b_in, b_in, cos, slo, shi)


def _rope_tables(pos):
    half = HEAD_DIM // 2
    inv = ROPE_THETA ** (-jnp.arange(half, dtype=F32) / half)
    ang = pos.astype(F32)[:, None] * inv[None, :]
    cos = jnp.tile(jnp.cos(ang), (1, LANES // half))
    sin = jnp.tile(jnp.sin(ang), (1, LANES // half))
    first = (jnp.arange(LANES) % HEAD_DIM) < half
    return cos, jnp.where(first, -sin, 0.0), jnp.where(first, 0.0, sin)


def _band_mask(mb):
    qi = lax.broadcasted_iota(I32, (BLOCK, 2 * BLOCK), 0)
    si = lax.broadcasted_iota(I32, (BLOCK, 2 * BLOCK), 1)
    dist = qi - si + BLOCK
    return (dist >= 0) & (dist <= BLOCK) & ((si >= BLOCK) | (mb > 0))


def _head_softmax_pv(qm, k2, v2, valid):
    s = lax.dot_general(qm, k2, (((1,), (1,)), ((), ())), preferred_element_type=F32)
    s = jnp.where(valid, s, MASK_VALUE)
    m = jnp.max(s, axis=1, keepdims=True)
    p = jnp.exp(s - m)
    l = jnp.sum(p, axis=1, keepdims=True)
    o = jnp.dot(p.astype(BF16), v2, preferred_element_type=F32) / l
    return o, m + jnp.log(l)


def _attn_a_kernel(q_ref, kc_ref, kp_ref, vc_ref, vp_ref, o_ref, lse_ref):
    valid = _band_mask(pl.program_id(2))
    first = lax.broadcasted_iota(I32, (BLOCK, LANES), 1) < HEAD_DIM
    for i in range(WIDTH // LANES):
        sl = slice(i * LANES, (i + 1) * LANES)
        qp = q_ref[:, sl] * SCALE
        k2 = jnp.concatenate([kp_ref[:, sl], kc_ref[:, sl]], axis=0).astype(BF16)
        v2 = jnp.concatenate([vp_ref[:, sl], vc_ref[:, sl]], axis=0).astype(BF16)
        o0, l0 = _head_softmax_pv(jnp.where(first, qp, 0.0).astype(BF16), k2, v2, valid)
        o1, l1 = _head_softmax_pv(jnp.where(first, 0.0, qp).astype(BF16), k2, v2, valid)
        o_ref[:, sl] = jnp.where(first, o0, o1)
        lse_ref[:, sl] = jnp.where(first, l0, l1)


def _attn_b_kernel(q_ref, kvc_ref, kvp_ref, sink_ref, o_ref):
    valid = _band_mask(pl.program_id(1))
    first = lax.broadcasted_iota(I32, (BLOCK, LANES), 1) < HEAD_DIM
    k2 = jnp.concatenate([kvp_ref[:, :KV_WIDTH_B], kvc_ref[:, :KV_WIDTH_B]], axis=0).astype(BF16)
    v2 = jnp.concatenate([kvp_ref[:, KV_WIDTH_B:], kvc_ref[:, KV_WIDTH_B:]], axis=0).astype(BF16)
    pairs_per_kv = (N_HEADS // N_KV_B) // 2
    for i in range(WIDTH // LANES):
        sl = slice(i * LANES, (i + 1) * LANES)
        kv_first = (i // pairs_per_kv) == 0
        live = first if kv_first else jnp.logical_not(first)
        qp = q_ref[:, sl] * SCALE
        qsw = pltpu.roll(qp, HEAD_DIM, 1)
        q_even = qp if kv_first else qsw
        q_odd = qsw if kv_first else qp
        oe, le = _head_softmax_pv(jnp.where(live, q_even, 0.0).astype(BF16), k2, v2, valid)
        oo, lo = _head_softmax_pv(jnp.where(live, q_odd, 0.0).astype(BF16), k2, v2, valid)
        if kv_first:
            o = jnp.where(first, oe, pltpu.roll(oo, HEAD_DIM, 1))
        else:
            o = jnp.where(first, pltpu.roll(oe, HEAD_DIM, 1), oo)
        lse = jnp.where(first, le, lo)
        o_ref[:, sl] = o * jax.nn.sigmoid(lse - sink_ref[:, sl])


def _attn_a_prompt(qkv4, batch, seq, d):
    t = batch * seq
    nblk = seq // d // BLOCK
    view = qkv4.reshape(t // d, d * QKV4_COLS)
    cur = lambda c: (lambda b, r, mb: (b * nblk + mb, 4 * r + c))
    prev = lambda c: (lambda b, r, mb: (b * nblk + jnp.maximum(mb - 1, 0), 4 * r + c))
    blk = lambda f: pl.BlockSpec((BLOCK, WIDTH), f)
    out_map = lambda b, r, mb: (b * nblk + mb, r)
    o, lse = pl.pallas_call(
        _attn_a_kernel,
        out_shape=(jax.ShapeDtypeStruct((t // d, d * WIDTH), F32),
                   jax.ShapeDtypeStruct((t // d, d * WIDTH), F32)),
        grid=(batch, d, nblk),
        in_specs=[blk(cur(0)), blk(cur(1)), blk(prev(1)), blk(cur(2)), blk(prev(2))],
        out_specs=(blk(out_map), blk(out_map)),
        compiler_params=_cparams(("parallel", "parallel", "arbitrary"), 32),
        name=f"attn_a_d{d}",
    )(view, view, view, view, view)
    return o.reshape(t, WIDTH), lse.reshape(t, WIDTH)


def _attn_b_prompt(qkv4, kvb, sinks_lanes, batch, seq):
    t = batch * seq
    nblk = seq // BLOCK
    cur = lambda b, mb: (b * nblk + mb, 0)
    prev = lambda b, mb: (b * nblk + jnp.maximum(mb - 1, 0), 0)
    return pl.pallas_call(
        _attn_b_kernel,
        out_shape=jax.ShapeDtypeStruct((t, WIDTH), F32),
        grid=(batch, nblk),
        in_specs=[
            pl.BlockSpec((BLOCK, WIDTH), lambda b, mb: (b * nblk + mb, 3)),
            pl.BlockSpec((BLOCK, 2 * KV_WIDTH_B), cur),
            pl.BlockSpec((BLOCK, 2 * KV_WIDTH_B), prev),
            pl.BlockSpec((1, WIDTH), lambda b, mb: (0, 0)),
        ],
        out_specs=pl.BlockSpec((BLOCK, WIDTH), cur),
        compiler_params=_cparams(("parallel", "arbitrary"), 32),
        name="attn_b",
    )(qkv4, kvb, kvb, sinks_lanes)


def _hi_dot(a, b, dims=None):
    if dims is None:
        return jnp.dot(a, b, preferred_element_type=F32, precision=lax.Precision.HIGHEST)
    return lax.dot_general(a, b, dims, preferred_element_type=F32, precision=lax.Precision.HIGHEST)


def _attn_sample_kernel(qa_ref, kan_ref, van_ref, qb_ref, kvn_ref, sink_ref, seg_ref, segt_ref,
                        k1_ref, k4_ref, k16_ref, v1_ref, v4_ref, v16_ref, ckb_ref, cvb_ref,
                        oa_ref, ob_ref):
    seg = seg_ref[...]
    segt = segt_ref[...]
    qs = qa_ref[0] * SCALE
    kn, vn = kan_ref[0], van_ref[0]
    s_new = _hi_dot(kn * qs, seg)
    ks = (k1_ref[0], k4_ref[0], k16_ref[0])
    vs = (v1_ref[0], v4_ref[0], v16_ref[0])
    scores = [_hi_dot(k * qs, seg) for k in ks]
    m = s_new
    for s in scores:
        m = jnp.maximum(m, jnp.max(s, axis=0, keepdims=True))
    n_pat = float(len(DILATIONS))
    e_new = jnp.exp(s_new - m)
    den = n_pat * e_new
    acc = n_pat * _hi_dot(e_new, segt) * vn
    for s, v in zip(scores, vs):
        e = jnp.exp(s - m)
        den = den + jnp.sum(e, axis=0, keepdims=True)
        acc = acc + jnp.sum(_hi_dot(e, segt) * v, axis=0, keepdims=True)
    oa_ref[0] = acc / _hi_dot(den, segt)

    qb = qb_ref[0] * SCALE
    kvn = kvn_ref[0]
    ck, cv = ckb_ref[0], cvb_ref[0]
    group = N_HEADS // N_KV_B
    nt = (((1,), (1,)), ((), ()))
    for kvh in range(N_KV_B):
        hs = slice(kvh * group, (kvh + 1) * group)
        ls = slice(kvh * HEAD_DIM, (kvh + 1) * HEAD_DIM)
        qh = qb[hs, :]
        s = _hi_dot(qh, ck[:, ls], nt)
        sn = jnp.sum(qh * kvn[:, ls], axis=1, keepdims=True)
        mm = jnp.maximum(jnp.max(s, axis=1, keepdims=True), sn)
        e = jnp.exp(s - mm)
        en = jnp.exp(sn - mm)
        l = jnp.sum(e, axis=1, keepdims=True) + en
        vnew = kvn[:, KV_WIDTH_B + kvh * HEAD_DIM:KV_WIDTH_B + (kvh + 1) * HEAD_DIM]
        o = (_hi_dot(e, cv[:, ls]) + en * vnew) / l
        lse = mm + jnp.log(l)
        ob_ref[0, hs, :] = o * jax.nn.sigmoid(lse - sink_ref[hs, :])


def _attn_sample(qkv4_s, kvb_s, cache_a_k, cache_a_v, cache_b_k, cache_b_v, sinks_col, seg, segt):
    nb = qkv4_s.shape[0]
    wa = cache_a_k.shape[1]
    q3 = qkv4_s.reshape(nb, 1, QKV4_COLS)
    qh = qkv4_s.reshape(nb, QKV4_COLS // HEAD_DIM, HEAD_DIM)
    kv3 = kvb_s.reshape(nb, 1, 2 * KV_WIDTH_B)
    row = lambda c: pl.BlockSpec((1, 1, WIDTH), lambda b: (b, 0, c))

    def cache_views(c):
        c2 = c.reshape(nb, wa, WIDTH)
        out = []
        for window, d in DILATIONS:
            assert wa >= window and wa % d == 0 and (wa - window) % (d * BLOCK) == 0
            v = c2.reshape(nb, wa // d, d * WIDTH)
            out.append((v, (wa - window) // d // BLOCK))
        return out

    kviews, vviews = cache_views(cache_a_k), cache_views(cache_a_v)
    cspec = lambda rb: pl.BlockSpec((1, BLOCK, WIDTH), lambda b: (b, rb, 0))
    oa, ob = pl.pallas_call(
        _attn_sample_kernel,
        out_shape=(jax.ShapeDtypeStruct((nb, 1, WIDTH), F32),
                   jax.ShapeDtypeStruct((nb, N_HEADS, HEAD_DIM), F32)),
        grid=(nb,),
        in_specs=[
            row(0), row(1), row(2),
            pl.BlockSpec((1, N_HEADS, HEAD_DIM), lambda b: (b, 3, 0)),
            pl.BlockSpec((1, 1, 2 * KV_WIDTH_B), lambda b: (b, 0, 0)),
            pl.BlockSpec((N_HEADS, 1), lambda b: (0, 0)),
            pl.BlockSpec((WIDTH, LANES), lambda b: (0, 0)),
            pl.BlockSpec((LANES, WIDTH), lambda b: (0, 0)),
            cspec(kviews[0][1]), cspec(kviews[1][1]), cspec(kviews[2][1]),
            cspec(vviews[0][1]), cspec(vviews[1][1]), cspec(vviews[2][1]),
            pl.BlockSpec((1, WINDOW_B, KV_WIDTH_B), lambda b: (b, 0, 0)),
            pl.BlockSpec((1, WINDOW_B, KV_WIDTH_B), lambda b: (b, 0, 0)),
        ],
        out_specs=(pl.BlockSpec((1, 1, WIDTH), lambda b: (b, 0, 0)),
                   pl.BlockSpec((1, N_HEADS, HEAD_DIM), lambda b: (b, 0, 0))),
        compiler_params=_cparams(("parallel",), 32),
        name="attn_sample",
    )(q3, q3, q3, qh, kv3, sinks_col, seg, segt,
      kviews[0][0], kviews[1][0], kviews[2][0], vviews[0][0], vviews[1][0], vviews[2][0],
      cache_b_k.reshape(nb, WINDOW_B, KV_WIDTH_B), cache_b_v.reshape(nb, WINDOW_B, KV_WIDTH_B))
    return oa.reshape(nb, WIDTH), ob.reshape(nb, WIDTH)


def _split_bf16(x):
    hi = x.astype(BF16)
    lo = (x - hi.astype(F32)).astype(BF16)
    return hi, lo


def _post_kernel(n_pat, *refs):
    pat_refs = refs[:2 * n_pat] if n_pat > 1 else refs[:1]
    n_in = len(pat_refs)
    (ob_ref, x_ref, ga_ref, gb_ref, w_ref, b_ref, gf_ref, wr_ref, br_ref,
     x1_ref, h2_ref, ridx_ref, rgate_ref) = refs[n_in:]
    if n_pat > 1:
        outs = [pat_refs[2 * p][...] for p in range(n_pat)]
        lses = [pat_refs[2 * p + 1][...] for p in range(n_pat)]
        m = functools.reduce(jnp.maximum, lses)
        ws = [jnp.exp(l - m) for l in lses]
        oa = sum(w * o for w, o in zip(ws, outs)) / sum(ws)
    else:
        oa = pat_refs[0][...]
    na = _rmsnorm(oa, ga_ref[...]).astype(BF16)
    nb = _rmsnorm(ob_ref[...], gb_ref[...]).astype(BF16)
    c = jnp.concatenate([na, nb], axis=1)
    x1 = x_ref[...] + jnp.dot(c, w_ref[...], preferred_element_type=F32) + b_ref[...]
    x1_ref[...] = x1
    h2 = _rmsnorm(x1, gf_ref[...])
    h2_ref[...] = h2

    h_hi, h_lo = _split_bf16(h2)
    w_hi, w_lo = _split_bf16(wr_ref[...])
    dot = lambda a, b: jnp.dot(a, b, preferred_element_type=F32)
    logits = dot(h_hi, w_hi) + (dot(h_hi, w_lo) + dot(h_lo, w_hi)) + br_ref[...]

    tm = logits.shape[0]
    eidx = lax.broadcasted_iota(I32, (tm, N_EXPERTS), 1)
    lane = lax.broadcasted_iota(I32, (tm, LANES), 1)
    work = logits
    vals, idxs = [], []
    for _ in range(TOP_K):
        v = jnp.max(work, axis=1, keepdims=True)
        i = jnp.min(jnp.where(work == v, eidx, N_EXPERTS), axis=1, keepdims=True)
        vals.append(v)
        idxs.append(i)
        work = jnp.where(eidx == i, -jnp.inf, work)
    es = [jnp.exp(v - vals[0]) for v in vals]
    den = sum(es)
    ridx = jnp.zeros((tm, LANES), I32)
    rgate = jnp.zeros((tm, LANES), F32)
    for k in range(TOP_K):
        ridx = jnp.where(lane == k, idxs[k], ridx)
        rgate = jnp.where(lane == k, es[k] / den, rgate)
    ridx_ref[...] = ridx
    rgate_ref[...] = rgate


def _post_attention(pats, ob, x, g_out_a, g_out_b, w_out_bf, b_out, g_ffn, w_router, b_router, tm):
    t = x.shape[0]
    n_pat = len(pats)
    pat_arrays = [a for pair in pats for a in pair] if n_pat > 1 else [pats[0][0]]
    rowblk = lambda w: pl.BlockSpec((tm, w), lambda i: (i, 0))
    full = lambda r, c: pl.BlockSpec((r, c), lambda i: (0, 0))
    return pl.pallas_call(
        functools.partial(_post_kernel, n_pat),
        out_shape=(jax.ShapeDtypeStruct((t, D_MODEL), F32), jax.ShapeDtypeStruct((t, D_MODEL), F32),
                   jax.ShapeDtypeStruct((t, LANES), I32), jax.ShapeDtypeStruct((t, LANES), F32)),
        grid=(t // tm,),
        in_specs=[rowblk(WIDTH)] * len(pat_arrays) + [
            rowblk(WIDTH), rowblk(D_MODEL), full(1, WIDTH), full(1, WIDTH),
            full(D_MODEL, D_MODEL), full(1, D_MODEL), full(1, D_MODEL),
            full(D_MODEL, N_EXPERTS), full(1, N_EXPERTS)],
        out_specs=(rowblk(D_MODEL), rowblk(D_MODEL), rowblk(LANES), rowblk(LANES)),
        compiler_params=_cparams(("parallel",), 56),
        name=f"post_attention_{n_pat}",
    )(*pat_arrays, ob, x, g_out_a, g_out_b, w_out_bf, b_out, g_ffn, w_router, b_router)


def _dispatch_kernel(tok_ref, h_hbm, o_ref, buf, sem):
    base = pl.program_id(0) * DISPATCH_ROWS

    def issue(r, carry):
        pltpu.make_async_copy(h_hbm.at[pl.ds(tok_ref[base + r], 1)], buf.at[pl.ds(r, 1)], sem).start()
        return carry

    lax.fori_loop(0, DISPATCH_ROWS, issue, 0)

    def drain(r, carry):
        pltpu.make_async_copy(h_hbm.at[pl.ds(0, 1)], buf.at[pl.ds(r, 1)], sem).wait()
        return carry

    lax.fori_loop(0, DISPATCH_ROWS, drain, 0)
    o_ref[...] = buf[...].astype(BF16)


def _dispatch(slot_tok, h2):
    p = slot_tok.shape[0]
    return pl.pallas_call(
        _dispatch_kernel,
        out_shape=jax.ShapeDtypeStruct((p, D_MODEL), BF16),
        grid_spec=pltpu.PrefetchScalarGridSpec(
            num_scalar_prefetch=1,
            grid=(p // DISPATCH_ROWS,),
            in_specs=[pl.BlockSpec(memory_space=pl.ANY)],
            out_specs=pl.BlockSpec((DISPATCH_ROWS, D_MODEL), lambda i, tok: (i, 0)),
            scratch_shapes=[pltpu.VMEM((DISPATCH_ROWS, D_MODEL), F32), pltpu.SemaphoreType.DMA],
        ),
        compiler_params=_cparams(("arbitrary",), 32),
        name="moe_dispatch",
    )(slot_tok, h2)


def _moe_kernel(ie_ref, iblk_ref, insub_ref, iact_ref, used_ref,
                x_hbm, wg_ref, wu_ref, wd_ref, bg_ref, bu_ref, bd_ref, y_hbm,
                xbuf, act, wa_bf, wb_bf, ystage, xsem, ysem):
    it = pl.program_id(0)
    s = pl.program_id(1)
    nsub = insub_ref[it]
    row0 = iblk_ref[it] * MOE_SUB

    @pl.when((it == pl.num_programs(0) - 1) & (s == pl.num_programs(1) - 1))
    def _():
        n_blocks = y_hbm.shape[0] // MOE_SUB
        ystage[pl.ds(0, MOE_SUB), :] = jnp.zeros((MOE_SUB, MOE_TF), F32)

        def z_copy(b, n):
            return pltpu.make_async_copy(
                ystage.at[pl.ds(0, MOE_SUB), :],
                y_hbm.at[pl.ds(b * MOE_SUB, MOE_SUB), pl.ds(n * MOE_TF, MOE_TF)], ysem)

        def fill(b, c):
            for n in range(MOE_F):
                z_copy(b, n).start()
            for n in range(MOE_F):
                z_copy(b, n).wait()
            return c
        lax.fori_loop(used_ref[0], n_blocks, fill, 0)

    def x_copy(j):
        return pltpu.make_async_copy(x_hbm.at[pl.ds(row0 + j * MOE_SUB, MOE_SUB)],
                                     xbuf.at[pl.ds(j * MOE_SUB, MOE_SUB)], xsem)

    @pl.when(s == 0)
    def _():
        def start(j, c):
            x_copy(j).start()
            return c
        lax.fori_loop(0, nsub, start, 0)

        def wait(j, c):
            x_copy(j).wait()
            return c
        lax.fori_loop(0, nsub, wait, 0)

    @pl.when(s < MOE_F)
    def _():
        wa_bf[...] = wg_ref[...].astype(BF16)
        wb_bf[...] = wu_ref[...].astype(BF16)
        bg, bu = bg_ref[...], bu_ref[...]

        def body(j, c):
            r0 = pl.multiple_of(j * MOE_SUB, MOE_SUB)
            x = xbuf[pl.ds(r0, MOE_SUB), :]
            g = jnp.dot(x, wa_bf[...], preferred_element_type=F32) + bg
            u = jnp.dot(x, wb_bf[...], preferred_element_type=F32) + bu
            g = jnp.minimum(g, SWIGLU_LIMIT)
            u = jnp.clip(u, -SWIGLU_LIMIT, SWIGLU_LIMIT)
            a = g * jax.nn.sigmoid(SWIGLU_ALPHA * g) * (u + 1.0)
            act[s, pl.ds(r0, MOE_SUB), :] = a.astype(BF16)
            return c
        lax.fori_loop(0, nsub, body, 0)

    @pl.when(s >= MOE_F)
    def _():
        n = s - MOE_F
        wa_bf[...] = wd_ref[...].astype(BF16)
        bd = bd_ref[...]

        def y_copy(j):
            r0 = pl.multiple_of(j * MOE_SUB, MOE_SUB)
            c0 = pl.multiple_of(n * MOE_TF, MOE_TF)
            return pltpu.make_async_copy(
                ystage.at[pl.ds(r0, MOE_SUB), :],
                y_hbm.at[pl.ds(row0 + r0, MOE_SUB), pl.ds(c0, MOE_TF)], ysem)

        def body(j, c):
            r0 = pl.multiple_of(j * MOE_SUB, MOE_SUB)
            y = bd
            for f in range(MOE_F):
                y = y + jnp.dot(act[f, pl.ds(r0, MOE_SUB), :], wa_bf[f * MOE_TF:(f + 1) * MOE_TF, :],
                                preferred_element_type=F32)
            ystage[pl.ds(r0, MOE_SUB), :] = y
            y_copy(j).start()
            return c
        lax.fori_loop(0, nsub, body, 0)

        def wait(j, c):
            y_copy(j).wait()
            return c
        lax.fori_loop(0, nsub, wait, 0)


def _moe_experts(items, x_sorted, w_gate_up, b_gate_up, w_down, b_down):
    item_e, item_blk, item_nsub, item_act, used_blocks = items
    n_items = item_e.shape[0]
    p = x_sorted.shape[0]
    rmax = MOE_MAX_SUB * MOE_SUB
    last = MOE_F - 1
    f_of = lambda s, act: jnp.where(act == 1, jnp.minimum(s, last), last)
    n_of = lambda s, act: jnp.where(act == 1, jnp.maximum(s - MOE_F, 0), last)
    wg_map = lambda i, s, ie, ib, ins, ia, iu: (ie[i], 0, f_of(s, ia[i]))
    wu_map = lambda i, s, ie, ib, ins, ia, iu: (ie[i], 0, MOE_F + f_of(s, ia[i]))
    wd_map = lambda i, s, ie, ib, ins, ia, iu: (ie[i], 0, n_of(s, ia[i]))
    bgu = b_gate_up.reshape(N_EXPERTS, 1, 2 * D_FF)
    bd = b_down.reshape(N_EXPERTS, 1, D_MODEL)
    return pl.pallas_call(
        _moe_kernel,
        out_shape=jax.ShapeDtypeStruct((p, D_MODEL), F32),
        grid_spec=pltpu.PrefetchScalarGridSpec(
            num_scalar_prefetch=5,
            grid=(n_items, 2 * MOE_F),
            in_specs=[
                pl.BlockSpec(memory_space=pl.ANY),
                pl.BlockSpec((None, D_MODEL, MOE_TF), wg_map),
                pl.BlockSpec((None, D_MODEL, MOE_TF), wu_map),
                pl.BlockSpec((None, D_FF, MOE_TF), wd_map),
                pl.BlockSpec((None, 1, MOE_TF), wg_map),
                pl.BlockSpec((None, 1, MOE_TF), wu_map),
                pl.BlockSpec((None, 1, MOE_TF), wd_map),
            ],
            out_specs=pl.BlockSpec(memory_space=pl.ANY),
            scratch_shapes=[
                pltpu.VMEM((rmax, D_MODEL), BF16),
                pltpu.VMEM((MOE_F, rmax, MOE_TF), BF16),
                pltpu.VMEM((D_MODEL, MOE_TF), BF16),
                pltpu.VMEM((D_MODEL, MOE_TF), BF16),
                pltpu.VMEM((rmax, MOE_TF), F32),
                pltpu.SemaphoreType.DMA,
                pltpu.SemaphoreType.DMA,
            ],
        ),
        compiler_params=_cparams(("arbitrary", "arbitrary"), 56),
        name="moe_experts",
    )(item_e, item_blk, item_nsub, item_act, used_blocks, x_sorted, w_gate_up, w_gate_up, w_down, bgu, bgu, bd)


def _tail_kernel(dest_ref, y_hbm, gate_ref, x1_ref, p_ref, wg_ref, wp_ref, gp_ref, gf_ref, o_ref, ybuf, sem):
    base = pl.program_id(0) * TAIL_ROWS * TOP_K

    def issue(r, carry):
        for k in range(TOP_K):
            pltpu.make_async_copy(y_hbm.at[pl.ds(dest_ref[base + r * TOP_K + k], 1)],
                                  ybuf.at[k, pl.ds(r, 1)], sem).start()
        return carry

    lax.fori_loop(0, TAIL_ROWS, issue, 0)

    def drain(r, carry):
        for k in range(TOP_K):
            pltpu.make_async_copy(y_hbm.at[pl.ds(0, 1)], ybuf.at[k, pl.ds(r, 1)], sem).wait()
        return carry

    lax.fori_loop(0, TAIL_ROWS, drain, 0)

    gate = gate_ref[...]
    moe = ybuf[0] * gate[:, 0:1]
    for k in range(1, TOP_K):
        moe = moe + ybuf[k] * gate[:, k:k + 1]
    x2 = x1_ref[...] + moe
    h3 = _rmsnorm(x2, gp_ref[...]).astype(BF16)
    ple_gate = jax.nn.sigmoid(jnp.dot(h3, wg_ref[...], preferred_element_type=F32))
    ple = jnp.dot(p_ref[...].astype(BF16), wp_ref[...], preferred_element_type=F32)
    x3 = x2 + ple_gate * ple
    o_ref[...] = _rmsnorm(x3, gf_ref[...])


def _tail(dest, y_sorted, rgate, x1, p_all, w_ple_gate_bf, w_ple_proj_bf, g_ple, g_final):
    t = x1.shape[0]
    rowblk = lambda w: pl.BlockSpec((TAIL_ROWS, w), lambda i, d: (i, 0))
    full = lambda r, c: pl.BlockSpec((r, c), lambda i, d: (0, 0))
    return pl.pallas_call(
        _tail_kernel,
        out_shape=jax.ShapeDtypeStruct((t, D_MODEL), F32),
        grid_spec=pltpu.PrefetchScalarGridSpec(
            num_scalar_prefetch=1,
            grid=(t // TAIL_ROWS,),
            in_specs=[
                pl.BlockSpec(memory_space=pl.ANY),
                rowblk(LANES), rowblk(D_MODEL), rowblk(PLE_DIM),
                full(D_MODEL, D_MODEL), full(PLE_DIM, D_MODEL), full(1, D_MODEL), full(1, D_MODEL)],
            out_specs=rowblk(D_MODEL),
            scratch_shapes=[pltpu.VMEM((TOP_K, TAIL_ROWS, D_MODEL), F32), pltpu.SemaphoreType.DMA],
        ),
        compiler_params=_cparams(("arbitrary",), 48),
        name="moe_tail",
    )(dest, y_sorted, rgate, x1, p_all, w_ple_gate_bf, w_ple_proj_bf, g_ple, g_final)


def _routing(ridx, rgate_unused, t):
    del rgate_unused
    m = t * TOP_K
    flat_e = ridx[:, :TOP_K].reshape(m)
    onehot = (flat_e[:, None] == jnp.arange(N_EXPERTS, dtype=I32)[None, :]).astype(I32)
    csum = jnp.cumsum(onehot, axis=0)
    rank = jnp.sum((csum - onehot) * onehot, axis=1)
    counts = csum[-1]
    nblk = (counts + MOE_SUB - 1) // MOE_SUB
    blk_end = jnp.cumsum(nblk)
    blk_start = blk_end - nblk
    dest = blk_start[flat_e] * MOE_SUB + rank
    n_blocks = -(-m // MOE_SUB) + N_EXPERTS
    n_blocks = -(-n_blocks // (DISPATCH_ROWS // MOE_SUB)) * (DISPATCH_ROWS // MOE_SUB)
    p = n_blocks * MOE_SUB
    flat_tok = jnp.arange(m, dtype=I32) // TOP_K
    slot_tok = jnp.zeros((p,), I32).at[dest].set(flat_tok)

    n_items = N_EXPERTS + -(-n_blocks // MOE_MAX_SUB)
    items_per_e = (nblk + MOE_MAX_SUB - 1) // MOE_MAX_SUB
    item_end = jnp.cumsum(items_per_e)
    total = item_end[-1]
    ids = jnp.arange(n_items, dtype=I32)
    active = ids < total
    ids_c = jnp.minimum(ids, total - 1)
    item_e = jnp.searchsorted(item_end, ids_c, side="right").astype(I32)
    local = ids_c - (item_end - items_per_e)[item_e]
    item_blk = blk_start[item_e] + local * MOE_MAX_SUB
    item_nsub = jnp.where(active, jnp.minimum(MOE_MAX_SUB, nblk[item_e] - local * MOE_MAX_SUB), 0)
    items = (item_e, item_blk.astype(I32), item_nsub.astype(I32), active.astype(I32),
             blk_end[-1:].astype(I32))
    return slot_tok, dest.astype(I32), items


def kernel(x_prompt, x_sample, cache_a_k, cache_a_v, cache_b_k, cache_b_v, p_prompt, p_sample, g_attn, w_in, b_in, sinks, g_out_a, g_out_b, w_out, b_out, g_ffn, w_router, b_router, w_gate_up, b_gate_up, w_down, b_down, g_ple, w_ple_gate, w_ple_proj, g_final):
    batch, seq, _ = x_prompt.shape
    nb, nseq, _ = x_sample.shape
    assert nseq == 1 and g_attn.shape[0] == 1
    tp, ts = batch * seq, nb * nseq
    row = lambda v: v.reshape(1, -1)

    w_in_bf = _cast_bf16(w_in[0])
    w_out_bf = _cast_bf16(w_out[0])
    w_pg_bf = _cast_bf16(w_ple_gate[0])
    w_pp_bf = _cast_bf16(w_ple_proj[0])

    tabs_p = _rope_tables(jnp.arange(seq, dtype=I32))
    tabs_s = _rope_tables(jnp.full((ts,), PAST_LEN, I32))
    sinks_lanes = jnp.repeat(sinks[0], HEAD_DIM).reshape(1, WIDTH)
    sinks_col = sinks[0].reshape(N_HEADS, 1)
    head_of_lane = jnp.arange(WIDTH, dtype=I32) // HEAD_DIM
    seg = (head_of_lane[:, None] == jnp.arange(LANES, dtype=I32)[None, :]).astype(F32)
    segt = seg.T

    tm_p = 512
    xp = x_prompt.reshape(tp, D_MODEL)
    xs = x_sample.reshape(ts, D_MODEL)
    qkv4_p, kvb_p = _qkv_proj(xp, row(g_attn), w_in_bf, row(b_in), tabs_p, tm_p, seq // tm_p)
    qkv4_s, kvb_s = _qkv_proj(xs, row(g_attn), w_in_bf, row(b_in), tabs_s, ts, 1)

    pats = [_attn_a_prompt(qkv4_p, batch, seq, d) for _, d in DILATIONS]
    ob_p = _attn_b_prompt(qkv4_p, kvb_p, sinks_lanes, batch, seq)
    oa_s, ob_s = _attn_sample(qkv4_s, kvb_s, cache_a_k[0], cache_a_v[0], cache_b_k[0], cache_b_v[0],
                              sinks_col, seg, segt)

    post = functools.partial(_post_attention, g_out_a=row(g_out_a), g_out_b=row(g_out_b), w_out_bf=w_out_bf,
                             b_out=row(b_out), g_ffn=row(g_ffn), w_router=w_router[0], b_router=row(b_router))
    x1_p, h2_p, ridx_p, rgate_p = post(pats, ob_p, xp, tm=256)
    x1_s, h2_s, ridx_s, rgate_s = post([(oa_s, None)], ob_s, xs, tm=ts)

    cat = lambda a, b: jnp.concatenate([a, b], axis=0)
    x1, h2, ridx, rgate = cat(x1_p, x1_s), cat(h2_p, h2_s), cat(ridx_p, ridx_s), cat(rgate_p, rgate_s)
    t = tp + ts
    slot_tok, dest, items = _routing(ridx, rgate, t)
    x_sorted = _dispatch(slot_tok, h2)
    y_sorted = _moe_experts(items, x_sorted, w_gate_up[0], b_gate_up[0], w_down[0], b_down[0])
    p_all = cat(p_prompt[0].reshape(tp, PLE_DIM), p_sample[0].reshape(ts, PLE_DIM))
    y = _tail(dest, y_sorted, rgate, x1, p_all, w_pg_bf, w_pp_bf, row(g_ple), row(g_final))

    y_prompt = y[:tp].reshape(batch, seq, D_MODEL)
    y_sample = y[tp:].reshape(nb, nseq, D_MODEL)
    na, nbw = min(DILATIONS[-1][0], seq), min(WINDOW_B, seq)
    q4 = qkv4_p.reshape(batch, seq, 4, N_HEADS, HEAD_DIM)
    kv = kvb_p.reshape(batch, seq, 2, N_KV_B, HEAD_DIM)
    a_k_prompt = q4[:, seq - na:, 1][None]
    a_v_prompt = q4[:, seq - na:, 2][None]
    b_k_prompt = kv[:, seq - nbw:, 0][None]
    b_v_prompt = kv[:, seq - nbw:, 1][None]
    q4s = qkv4_s.reshape(nb, nseq, 4, N_HEADS, HEAD_DIM)
    kvs = kvb_s.reshape(nb, nseq, 2, N_KV_B, HEAD_DIM)
    return (y_prompt, y_sample, a_k_prompt, a_v_prompt, b_k_prompt, b_v_prompt,
            q4s[:, :, 1][None], q4s[:, :, 2][None], kvs[:, :, 0][None], kvs[:, :, 1][None])
```

```python
import functools

import jax
import jax.numpy as jnp
from jax import lax
from jax.experimental import pallas as pl
from jax.experimental.pallas import tpu as pltpu

F32 = jnp.float32
BF16 = jnp.bfloat16
I32 = jnp.int32

D_MODEL = 2048
HEAD_DIM = 64
N_HEADS = 16
WIDTH = N_HEADS * HEAD_DIM
N_KV_B = 2
KV_WIDTH_B = N_KV_B * HEAD_DIM
DILATIONS = ((128, 1), (512, 4), (2048, 16))
WINDOW_B = 128
BLOCK = 128
N_EXPERTS = 32
TOP_K = 4
D_FF = 2048
SWIGLU_ALPHA = 1.702
SWIGLU_LIMIT = 7.0
PLE_DIM = 256
ROPE_THETA = 10000.0
NORM_EPS = 1e-5
MASK_VALUE = -1e30
SCALE = HEAD_DIM ** -0.5
PAST_LEN = 16384
QKV4_COLS = 4 * WIDTH
IN_COLS = QKV4_COLS + 2 * KV_WIDTH_B

LANES = 128
VMEM_LIMIT_CAP = 60 * 1024 * 1024

MOE_TF = 256
MOE_F = D_FF // MOE_TF
MOE_SUB = 128
MOE_MAX_SUB = 16
DISPATCH_ROWS = 256
TAIL_ROWS = 128


def _cparams(semantics, vmem_mb):
    return pltpu.CompilerParams(
        dimension_semantics=semantics,
        vmem_limit_bytes=min(vmem_mb * 1024 * 1024, VMEM_LIMIT_CAP))


def _rmsnorm(x, g):
    ms = jnp.mean(x * x, axis=-1, keepdims=True)
    return x * lax.rsqrt(ms + NORM_EPS) * g


def _cast_kernel(x_ref, o_ref):
    o_ref[...] = x_ref[...].astype(o_ref.dtype)


def _cast_bf16(w, rows_per_step=256):
    r, c = w.shape
    return pl.pallas_call(
        _cast_kernel,
        out_shape=jax.ShapeDtypeStruct((r, c), BF16),
        grid=(r // rows_per_step,),
        in_specs=[pl.BlockSpec((rows_per_step, c), lambda i: (i, 0))],
        out_specs=pl.BlockSpec((rows_per_step, c), lambda i: (i, 0)),
        compiler_params=_cparams(("parallel",), 32),
        name="cast_bf16",
    )(w)


def _rope_store(dst_ref, y, cos, sin_lo, sin_hi):
    for c in range(y.shape[1] // LANES):
        yc = y[:, c * LANES:(c + 1) * LANES]
        dst_ref[:, c * LANES:(c + 1) * LANES] = (
            yc * cos + pltpu.roll(yc, LANES - 32, 1) * sin_lo + pltpu.roll(yc, 32, 1) * sin_hi)


def _qkv_kernel(x_ref, g_ref, w_ref, wkv_ref, b_ref, bkv_ref, cos_ref, slo_ref, shi_ref,
                o4_ref, okv_ref, h_scr):
    j = pl.program_id(1)

    @pl.when(j == 0)
    def _():
        h_scr[...] = _rmsnorm(x_ref[...], g_ref[...]).astype(BF16)

    h = h_scr[...]
    y = jnp.dot(h, w_ref[...], preferred_element_type=F32) + b_ref[...]
    cos, slo, shi = cos_ref[...], slo_ref[...], shi_ref[...]

    @pl.when(j != 2)
    def _():
        _rope_store(o4_ref, y, cos, slo, shi)

    @pl.when(j == 2)
    def _():
        o4_ref[...] = y

    @pl.when(j == 3)
    def _():
        ykv = jnp.dot(h, wkv_ref[...], preferred_element_type=F32) + bkv_ref[...]
        _rope_store(okv_ref.at[:, :KV_WIDTH_B], ykv[:, :KV_WIDTH_B], cos, slo, shi)
        okv_ref[:, KV_WIDTH_B:] = ykv[:, KV_WIDTH_B:]


def _qkv_proj(x, g_attn, w_in_bf, b_in, rope_tabs, tm, tab_blocks):
    t = x.shape[0]
    cos, slo, shi = rope_tabs
    tab_map = lambda i, j: (i % tab_blocks, 0)
    return pl.pallas_call(
        _qkv_kernel,
        out_shape=(jax.ShapeDtypeStruct((t, QKV4_COLS), F32),
                   jax.ShapeDtypeStruct((t, 2 * KV_WIDTH_B), F32)),
        grid=(t // tm, 4),
        in_specs=[
            pl.BlockSpec((tm, D_MODEL), lambda i, j: (i, 0)),
            pl.BlockSpec((1, D_MODEL), lambda i, j: (0, 0)),
            pl.BlockSpec((D_MODEL, WIDTH), lambda i, j: (0, j)),
            pl.BlockSpec((D_MODEL, 2 * KV_WIDTH_B), lambda i, j: (0, QKV4_COLS // (2 * KV_WIDTH_B))),
            pl.BlockSpec((1, WIDTH), lambda i, j: (0, j)),
            pl.BlockSpec((1, 2 * KV_WIDTH_B), lambda i, j: (0, QKV4_COLS // (2 * KV_WIDTH_B))),
            pl.BlockSpec((tm, LANES), tab_map),
            pl.BlockSpec((tm, LANES), tab_map),
            pl.BlockSpec((tm, LANES), tab_map),
        ],
        out_specs=(pl.BlockSpec((tm, WIDTH), lambda i, j: (i, j)),
                   pl.BlockSpec((tm, 2 * KV_WIDTH_B), lambda i, j: (i, 0))),
        scratch_shapes=[pltpu.VMEM((tm, D_MODEL), BF16)],
        compiler_params=_cparams(("parallel", "arbitrary"), 48),
        name="qkv_proj",
    )(x, g_attn, w_in_bf, w_in_bf, b_in, b_in, cos, slo, shi)


def _rope_tables(pos):
    half = HEAD_DIM // 2
    inv = ROPE_THETA ** (-jnp.arange(half, dtype=F32) / half)
    ang = pos.astype(F32)[:, None] * inv[None, :]
    cos = jnp.tile(jnp.cos(ang), (1, LANES // half))
    sin = jnp.tile(jnp.sin(ang), (1, LANES // half))
    first = (jnp.arange(LANES) % HEAD_DIM) < half
    return cos, jnp.where(first, -sin, 0.0), jnp.where(first, 0.0, sin)


def _band_mask(mb):
    qi = lax.broadcasted_iota(I32, (BLOCK, 2 * BLOCK), 0)
    si = lax.broadcasted_iota(I32, (BLOCK, 2 * BLOCK), 1)
    dist = qi - si + BLOCK
    return (dist >= 0) & (dist <= BLOCK) & ((si >= BLOCK) | (mb > 0))


def _head_softmax_pv(qm, k2, v2, valid):
    s = lax.dot_general(qm, k2, (((1,), (1,)), ((), ())), preferred_element_type=F32)
    s = jnp.where(valid, s, MASK_VALUE)
    m = jnp.max(s, axis=1, keepdims=True)
    p = jnp.exp(s - m)
    l = jnp.sum(p, axis=1, keepdims=True)
    o = jnp.dot(p.astype(BF16), v2, preferred_element_type=F32) / l
    return o, m + jnp.log(l)


def _attn_a_kernel(q_ref, kc_ref, kp_ref, vc_ref, vp_ref, o_ref, lse_ref):
    valid = _band_mask(pl.program_id(2))
    first = lax.broadcasted_iota(I32, (BLOCK, LANES), 1) < HEAD_DIM
    for i in range(WIDTH // LANES):
        sl = slice(i * LANES, (i + 1) * LANES)
        qp = q_ref[:, sl] * SCALE
        k2 = jnp.concatenate([kp_ref[:, sl], kc_ref[:, sl]], axis=0).astype(BF16)
        v2 = jnp.concatenate([vp_ref[:, sl], vc_ref[:, sl]], axis=0).astype(BF16)
        o0, l0 = _head_softmax_pv(jnp.where(first, qp, 0.0).astype(BF16), k2, v2, valid)
        o1, l1 = _head_softmax_pv(jnp.where(first, 0.0, qp).astype(BF16), k2, v2, valid)
        o_ref[:, sl] = jnp.where(first, o0, o1)
        lse_ref[:, sl] = jnp.where(first, l0, l1)


def _attn_b_kernel(q_ref, kvc_ref, kvp_ref, sink_ref, o_ref):
    valid = _band_mask(pl.program_id(1))
    first = lax.broadcasted_iota(I32, (BLOCK, LANES), 1) < HEAD_DIM
    k2 = jnp.concatenate([kvp_ref[:, :KV_WIDTH_B], kvc_ref[:, :KV_WIDTH_B]], axis=0).astype(BF16)
    v2 = jnp.concatenate([kvp_ref[:, KV_WIDTH_B:], kvc_ref[:, KV_WIDTH_B:]], axis=0).astype(BF16)
    pairs_per_kv = (N_HEADS // N_KV_B) // 2
    for i in range(WIDTH // LANES):
        sl = slice(i * LANES, (i + 1) * LANES)
        kv_first = (i // pairs_per_kv) == 0
        live = first if kv_first else jnp.logical_not(first)
        qp = q_ref[:, sl] * SCALE
        qsw = pltpu.roll(qp, HEAD_DIM, 1)
        q_even = qp if kv_first else qsw
        q_odd = qsw if kv_first else qp
        oe, le = _head_softmax_pv(jnp.where(live, q_even, 0.0).astype(BF16), k2, v2, valid)
        oo, lo = _head_softmax_pv(jnp.where(live, q_odd, 0.0).astype(BF16), k2, v2, valid)
        if kv_first:
            o = jnp.where(first, oe, pltpu.roll(oo, HEAD_DIM, 1))
        else:
            o = jnp.where(first, pltpu.roll(oe, HEAD_DIM, 1), oo)
        lse = jnp.where(first, le, lo)
        o_ref[:, sl] = o * jax.nn.sigmoid(lse - sink_ref[:, sl])


def _attn_a_prompt(qkv4, batch, seq, d):
    t = batch * seq
    nblk = seq // d // BLOCK
    view = qkv4.reshape(t // d, d * QKV4_COLS)
    cur = lambda c: (lambda b, r, mb: (b * nblk + mb, 4 * r + c))
    prev = lambda c: (lambda b, r, mb: (b * nblk + jnp.maximum(mb - 1, 0), 4 * r + c))
    blk = lambda f: pl.BlockSpec((BLOCK, WIDTH), f)
    out_map = lambda b, r, mb: (b * nblk + mb, r)
    o, lse = pl.pallas_call(
        _attn_a_kernel,
        out_shape=(jax.ShapeDtypeStruct((t // d, d * WIDTH), F32),
                   jax.ShapeDtypeStruct((t // d, d * WIDTH), F32)),
        grid=(batch, d, nblk),
        in_specs=[blk(cur(0)), blk(cur(1)), blk(prev(1)), blk(cur(2)), blk(prev(2))],
        out_specs=(blk(out_map), blk(out_map)),
        compiler_params=_cparams(("parallel", "parallel", "arbitrary"), 32),
        name=f"attn_a_d{d}",
    )(view, view, view, view, view)
    return o.reshape(t, WIDTH), lse.reshape(t, WIDTH)


def _attn_b_prompt(qkv4, kvb, sinks_lanes, batch, seq):
    t = batch * seq
    nblk = seq // BLOCK
    cur = lambda b, mb: (b * nblk + mb, 0)
    prev = lambda b, mb: (b * nblk + jnp.maximum(mb - 1, 0), 0)
    return pl.pallas_call(
        _attn_b_kernel,
        out_shape=jax.ShapeDtypeStruct((t, WIDTH), F32),
        grid=(batch, nblk),
        in_specs=[
            pl.BlockSpec((BLOCK, WIDTH), lambda b, mb: (b * nblk + mb, 3)),
            pl.BlockSpec((BLOCK, 2 * KV_WIDTH_B), cur),
            pl.BlockSpec((BLOCK, 2 * KV_WIDTH_B), prev),
            pl.BlockSpec((1, WIDTH), lambda b, mb: (0, 0)),
        ],
        out_specs=pl.BlockSpec((BLOCK, WIDTH), cur),
        compiler_params=_cparams(("parallel", "arbitrary"), 32),
        name="attn_b",
    )(qkv4, kvb, kvb, sinks_lanes)


def _hi_dot(a, b, dims=None):
    if dims is None:
        return jnp.dot(a, b, preferred_element_type=F32, precision=lax.Precision.HIGHEST)
    return lax.dot_general(a, b, dims, preferred_element_type=F32, precision=lax.Precision.HIGHEST)


def _attn_sample_kernel(q4_ref, kvn_ref, sink_ref,
                        k1_ref, k4_ref, k16_ref, v1_ref, v4_ref, v16_ref, ckb_ref, cvb_ref,
                        oa_ref, ob_ref):
    q4 = q4_ref[0]
    qa = q4[0:N_HEADS] * SCALE
    kn = q4[N_HEADS:2 * N_HEADS]
    vn = q4[2 * N_HEADS:3 * N_HEADS]
    qb = q4[3 * N_HEADS:4 * N_HEADS] * SCALE

    s_new = jnp.sum(kn * qa, axis=-1, keepdims=True)
    ks = (k1_ref[0], k4_ref[0], k16_ref[0])
    vs = (v1_ref[0], v4_ref[0], v16_ref[0])
    scores = [jnp.sum(k * qa[None], axis=-1, keepdims=True) for k in ks]
    m = s_new
    for s in scores:
        m = jnp.maximum(m, jnp.max(s, axis=0))
    n_pat = float(len(DILATIONS))
    e_new = jnp.exp(s_new - m)
    den = n_pat * e_new
    acc = (n_pat * e_new) * vn
    for s, v in zip(scores, vs):
        e = jnp.exp(s - m[None])
        den = den + jnp.sum(e, axis=0)
        acc = acc + jnp.sum(e * v, axis=0)
    oa_ref[0] = acc / den

    kvn = kvn_ref[0]
    group = N_HEADS // N_KV_B
    nt = (((1,), (1,)), ((), ()))
    for kvh in range(N_KV_B):
        hs = slice(kvh * group, (kvh + 1) * group)
        qh = qb[hs, :]
        ck = ckb_ref[0, :, kvh, :]
        cv = cvb_ref[0, :, kvh, :]
        s = _hi_dot(qh, ck, nt)
        sn = jnp.sum(qh * kvn[kvh:kvh + 1, :], axis=1, keepdims=True)
        mm = jnp.maximum(jnp.max(s, axis=1, keepdims=True), sn)
        e = jnp.exp(s - mm)
        en = jnp.exp(sn - mm)
        l = jnp.sum(e, axis=1, keepdims=True) + en
        o = (_hi_dot(e, cv) + en * kvn[N_KV_B + kvh:N_KV_B + kvh + 1, :]) / l
        lse = mm + jnp.log(l)
        ob_ref[0, hs, :] = o * jax.nn.sigmoid(lse - sink_ref[hs, :])


def _attn_sample(qkv4_s, kvb_s, cache_a_k, cache_a_v, cache_b_k, cache_b_v, sinks_col):
    nb = qkv4_s.shape[0]
    wa = cache_a_k.shape[1]
    q4 = qkv4_s.reshape(nb, QKV4_COLS // HEAD_DIM, HEAD_DIM)
    kvn = kvb_s.reshape(nb, 2 * N_KV_B, HEAD_DIM)

    def strided(c):
        out = []
        for window, d in DILATIONS:
            assert wa >= window and wa % d == 0 and (wa - window) % (d * BLOCK) == 0
            out.append((c.reshape(nb, wa // d, d, N_HEADS, HEAD_DIM), (wa - window) // d // BLOCK))
        return out

    kviews, vviews = strided(cache_a_k), strided(cache_a_v)
    cspec = lambda rb: pl.BlockSpec((1, BLOCK, None, N_HEADS, HEAD_DIM), lambda b: (b, rb, 0, 0, 0))
    bspec = pl.BlockSpec((1, WINDOW_B, N_KV_B, HEAD_DIM), lambda b: (b, 0, 0, 0))
    oa, ob = pl.pallas_call(
        _attn_sample_kernel,
        out_shape=(jax.ShapeDtypeStruct((nb, N_HEADS, HEAD_DIM), F32),
                   jax.ShapeDtypeStruct((nb, N_HEADS, HEAD_DIM), F32)),
        grid=(nb,),
        in_specs=[
            pl.BlockSpec((1, QKV4_COLS // HEAD_DIM, HEAD_DIM), lambda b: (b, 0, 0)),
            pl.BlockSpec((1, 2 * N_KV_B, HEAD_DIM), lambda b: (b, 0, 0)),
            pl.BlockSpec((N_HEADS, 1), lambda b: (0, 0)),
            cspec(kviews[0][1]), cspec(kviews[1][1]), cspec(kviews[2][1]),
            cspec(vviews[0][1]), cspec(vviews[1][1]), cspec(vviews[2][1]),
            bspec, bspec,
        ],
        out_specs=(pl.BlockSpec((1, N_HEADS, HEAD_DIM), lambda b: (b, 0, 0)),
                   pl.BlockSpec((1, N_HEADS, HEAD_DIM), lambda b: (b, 0, 0))),
        compiler_params=_cparams(("parallel",), 32),
        name="attn_sample",
    )(q4, kvn, sinks_col,
      kviews[0][0], kviews[1][0], kviews[2][0], vviews[0][0], vviews[1][0], vviews[2][0],
      cache_b_k, cache_b_v)
    return oa.reshape(nb, WIDTH), ob.reshape(nb, WIDTH)


def _split_bf16(x):
    hi = x.astype(BF16)
    lo = (x - hi.astype(F32)).astype(BF16)
    return hi, lo


def _post_kernel(n_pat, *refs):
    pat_refs = refs[:2 * n_pat] if n_pat > 1 else refs[:1]
    n_in = len(pat_refs)
    (ob_ref, x_ref, ga_ref, gb_ref, w_ref, b_ref, gf_ref, wr_ref, br_ref,
     x1_ref, h2_ref, ridx_ref, rgate_ref) = refs[n_in:]
    if n_pat > 1:
        outs = [pat_refs[2 * p][...] for p in range(n_pat)]
        lses = [pat_refs[2 * p + 1][...] for p in range(n_pat)]
        m = functools.reduce(jnp.maximum, lses)
        ws = [jnp.exp(l - m) for l in lses]
        oa = sum(w * o for w, o in zip(ws, outs)) / sum(ws)
    else:
        oa = pat_refs[0][...]
    na = _rmsnorm(oa, ga_ref[...]).astype(BF16)
    nb = _rmsnorm(ob_ref[...], gb_ref[...]).astype(BF16)
    c = jnp.concatenate([na, nb], axis=1)
    x1 = x_ref[...] + jnp.dot(c, w_ref[...], preferred_element_type=F32) + b_ref[...]
    x1_ref[...] = x1
    h2 = _rmsnorm(x1, gf_ref[...])
    h2_ref[...] = h2

    h_hi, h_lo = _split_bf16(h2)
    w_hi, w_lo = _split_bf16(wr_ref[...])
    dot = lambda a, b: jnp.dot(a, b, preferred_element_type=F32)
    logits = dot(h_hi, w_hi) + (dot(h_hi, w_lo) + dot(h_lo, w_hi)) + br_ref[...]

    tm = logits.shape[0]
    eidx = lax.broadcasted_iota(I32, (tm, N_EXPERTS), 1)
    lane = lax.broadcasted_iota(I32, (tm, LANES), 1)
    work = logits
    vals, idxs = [], []
    for _ in range(TOP_K):
        v = jnp.max(work, axis=1, keepdims=True)
        i = jnp.min(jnp.where(work == v, eidx, N_EXPERTS), axis=1, keepdims=True)
        vals.append(v)
        idxs.append(i)
        work = jnp.where(eidx == i, -jnp.inf, work)
    es = [jnp.exp(v - vals[0]) for v in vals]
    den = sum(es)
    ridx = jnp.zeros((tm, LANES), I32)
    rgate = jnp.zeros((tm, LANES), F32)
    for k in range(TOP_K):
        ridx = jnp.where(lane == k, idxs[k], ridx)
        rgate = jnp.where(lane == k, es[k] / den, rgate)
    ridx_ref[...] = ridx
    rgate_ref[...] = rgate


def _post_attention(pats, ob, x, g_out_a, g_out_b, w_out_bf, b_out, g_ffn, w_router, b_router, tm):
    t = x.shape[0]
    n_pat = len(pats)
    pat_arrays = [a for pair in pats for a in pair] if n_pat > 1 else [pats[0][0]]
    rowblk = lambda w: pl.BlockSpec((tm, w), lambda i: (i, 0))
    full = lambda r, c: pl.BlockSpec((r, c), lambda i: (0, 0))
    return pl.pallas_call(
        functools.partial(_post_kernel, n_pat),
        out_shape=(jax.ShapeDtypeStruct((t, D_MODEL), F32), jax.ShapeDtypeStruct((t, D_MODEL), F32),
                   jax.ShapeDtypeStruct((t, LANES), I32), jax.ShapeDtypeStruct((t, LANES), F32)),
        grid=(t // tm,),
        in_specs=[rowblk(WIDTH)] * len(pat_arrays) + [
            rowblk(WIDTH), rowblk(D_MODEL), full(1, WIDTH), full(1, WIDTH),
            full(D_MODEL, D_MODEL), full(1, D_MODEL), full(1, D_MODEL),
            full(D_MODEL, N_EXPERTS), full(1, N_EXPERTS)],
        out_specs=(rowblk(D_MODEL), rowblk(D_MODEL), rowblk(LANES), rowblk(LANES)),
        compiler_params=_cparams(("parallel",), 56),
        name=f"post_attention_{n_pat}",
    )(*pat_arrays, ob, x, g_out_a, g_out_b, w_out_bf, b_out, g_ffn, w_router, b_router)


def _dispatch_kernel(tok_ref, h_hbm, o_ref, buf, sem):
    i = pl.program_id(0)

    def issue(step, slot):
        base = step * DISPATCH_ROWS

        def body(r, carry):
            pltpu.make_async_copy(h_hbm.at[pl.ds(tok_ref[base + r], 1)],
                                  buf.at[slot, pl.ds(r, 1)], sem.at[slot]).start()
            return carry
        lax.fori_loop(0, DISPATCH_ROWS, body, 0, unroll=8)

    @pl.when(i == 0)
    def _():
        issue(0, 0)

    @pl.when(i + 1 < pl.num_programs(0))
    def _():
        issue(i + 1, (i + 1) % 2)

    slot = i % 2
    pltpu.make_async_copy(h_hbm.at[pl.ds(0, DISPATCH_ROWS)], buf.at[slot], sem.at[slot]).wait()
    o_ref[...] = buf[slot].astype(BF16)


def _dispatch(slot_tok, h2):
    p = slot_tok.shape[0]
    return pl.pallas_call(
        _dispatch_kernel,
        out_shape=jax.ShapeDtypeStruct((p, D_MODEL), BF16),
        grid_spec=pltpu.PrefetchScalarGridSpec(
            num_scalar_prefetch=1,
            grid=(p // DISPATCH_ROWS,),
            in_specs=[pl.BlockSpec(memory_space=pl.ANY)],
            out_specs=pl.BlockSpec((DISPATCH_ROWS, D_MODEL), lambda i, tok: (i, 0)),
            scratch_shapes=[pltpu.VMEM((2, DISPATCH_ROWS, D_MODEL), F32), pltpu.SemaphoreType.DMA((2,))],
        ),
        compiler_params=_cparams(("arbitrary",), 32),
        name="moe_dispatch",
    )(slot_tok, h2)


def _moe_kernel(ie_ref, iblk_ref, insub_ref, iact_ref, used_ref,
                x_hbm, wg_ref, wu_ref, wd_ref, bg_ref, bu_ref, bd_ref, y_hbm,
                xbuf, act, wa_bf, wb_bf, ystage, xsem, ysem):
    it = pl.program_id(0)
    s = pl.program_id(1)
    nsub = insub_ref[it]
    row0 = iblk_ref[it] * MOE_SUB

    @pl.when((it == pl.num_programs(0) - 1) & (s == pl.num_programs(1) - 1))
    def _():
        n_blocks = y_hbm.shape[0] // MOE_SUB
        ystage[pl.ds(0, MOE_SUB), :] = jnp.zeros((MOE_SUB, MOE_TF), F32)

        def z_copy(b, n):
            return pltpu.make_async_copy(
                ystage.at[pl.ds(0, MOE_SUB), :],
                y_hbm.at[pl.ds(b * MOE_SUB, MOE_SUB), pl.ds(n * MOE_TF, MOE_TF)], ysem)

        def fill(b, c):
            for n in range(MOE_F):
                z_copy(b, n).start()
            for n in range(MOE_F):
                z_copy(b, n).wait()
            return c
        lax.fori_loop(used_ref[0], n_blocks, fill, 0)

    def x_copy(j):
        return pltpu.make_async_copy(x_hbm.at[pl.ds(row0 + j * MOE_SUB, MOE_SUB)],
                                     xbuf.at[pl.ds(j * MOE_SUB, MOE_SUB)], xsem)

    def for_row_chunks(fn):
        big = nsub // 4

        def body(j, c):
            fn(pl.multiple_of(j * (4 * MOE_SUB), 4 * MOE_SUB), 4 * MOE_SUB)
            return c
        lax.fori_loop(0, big, body, 0)
        rem = nsub - 4 * big
        has2 = rem >= 2

        @pl.when(has2)
        def _():
            fn(pl.multiple_of(big * (4 * MOE_SUB), MOE_SUB), 2 * MOE_SUB)

        @pl.when((rem & 1) == 1)
        def _():
            fn(pl.multiple_of((4 * big + jnp.where(has2, 2, 0)) * MOE_SUB, MOE_SUB), MOE_SUB)

    @pl.when((s == 0) & (nsub > 0))
    def _():
        def start(j, c):
            x_copy(j).start()
            return c
        lax.fori_loop(0, nsub, start, 0)

        def wait(j, c):
            x_copy(j).wait()
            return c
        lax.fori_loop(0, nsub, wait, 0)

    @pl.when((s < MOE_F) & (nsub > 0))
    def _():
        wa_bf[...] = wg_ref[...].astype(BF16)
        wb_bf[...] = wu_ref[...].astype(BF16)
        bg, bu = bg_ref[...], bu_ref[...]

        def gate_up(r0, rows):
            x = xbuf[pl.ds(r0, rows), :]
            g = jnp.dot(x, wa_bf[...], preferred_element_type=F32) + bg
            u = jnp.dot(x, wb_bf[...], preferred_element_type=F32) + bu
            g = jnp.minimum(g, SWIGLU_LIMIT)
            u = jnp.clip(u, -SWIGLU_LIMIT, SWIGLU_LIMIT)
            a = g * jax.nn.sigmoid(SWIGLU_ALPHA * g) * (u + 1.0)
            act[s, pl.ds(r0, rows), :] = a.astype(BF16)
        for_row_chunks(gate_up)

    @pl.when((s >= MOE_F) & (nsub > 0))
    def _():
        n = s - MOE_F
        wa_bf[...] = wd_ref[...].astype(BF16)
        bd = bd_ref[...]
        c0 = pl.multiple_of(n * MOE_TF, MOE_TF)

        def y_copy(r0, rows):
            return pltpu.make_async_copy(
                ystage.at[pl.ds(r0, rows), :],
                y_hbm.at[pl.ds(row0 + r0, rows), pl.ds(c0, MOE_TF)], ysem)

        def down(r0, rows):
            a = jnp.concatenate([act[f, pl.ds(r0, rows), :] for f in range(MOE_F)], axis=1)
            ystage[pl.ds(r0, rows), :] = jnp.dot(a, wa_bf[...], preferred_element_type=F32) + bd
            y_copy(r0, rows).start()
        for_row_chunks(down)
        for_row_chunks(lambda r0, rows: y_copy(r0, rows).wait())


def _moe_experts(items, x_sorted, w_gate_up, b_gate_up, w_down, b_down):
    item_e, item_blk, item_nsub, item_act, used_blocks = items
    n_items = item_e.shape[0]
    p = x_sorted.shape[0]
    rmax = MOE_MAX_SUB * MOE_SUB
    last = MOE_F - 1
    f_of = lambda s, act: jnp.where(act == 1, jnp.minimum(s, last), last)
    n_of = lambda s, act: jnp.where(act == 1, jnp.maximum(s - MOE_F, 0), last)
    wg_map = lambda i, s, ie, ib, ins, ia, iu: (ie[i], 0, f_of(s, ia[i]))
    wu_map = lambda i, s, ie, ib, ins, ia, iu: (ie[i], 0, MOE_F + f_of(s, ia[i]))
    wd_map = lambda i, s, ie, ib, ins, ia, iu: (ie[i], 0, n_of(s, ia[i]))
    bgu = b_gate_up.reshape(N_EXPERTS, 1, 2 * D_FF)
    bd = b_down.reshape(N_EXPERTS, 1, D_MODEL)
    return pl.pallas_call(
        _moe_kernel,
        out_shape=jax.ShapeDtypeStruct((p, D_MODEL), F32),
        grid_spec=pltpu.PrefetchScalarGridSpec(
            num_scalar_prefetch=5,
            grid=(n_items, 2 * MOE_F),
            in_specs=[
                pl.BlockSpec(memory_space=pl.ANY),
                pl.BlockSpec((None, D_MODEL, MOE_TF), wg_map),
                pl.BlockSpec((None, D_MODEL, MOE_TF), wu_map),
                pl.BlockSpec((None, D_FF, MOE_TF), wd_map),
                pl.BlockSpec((None, 1, MOE_TF), wg_map),
                pl.BlockSpec((None, 1, MOE_TF), wu_map),
                pl.BlockSpec((None, 1, MOE_TF), wd_map),
            ],
            out_specs=pl.BlockSpec(memory_space=pl.ANY),
            scratch_shapes=[
                pltpu.VMEM((rmax, D_MODEL), BF16),
                pltpu.VMEM((MOE_F, rmax, MOE_TF), BF16),
                pltpu.VMEM((D_MODEL, MOE_TF), BF16),
                pltpu.VMEM((D_MODEL, MOE_TF), BF16),
                pltpu.VMEM((rmax, MOE_TF), F32),
                pltpu.SemaphoreType.DMA,
                pltpu.SemaphoreType.DMA,
            ],
        ),
        compiler_params=_cparams(("arbitrary", "arbitrary"), 56),
        name="moe_experts",
    )(item_e, item_blk, item_nsub, item_act, used_blocks, x_sorted, w_gate_up, w_gate_up, w_down, bgu, bgu, bd)


def _tail_kernel(dest_ref, y_hbm, gate_ref, x1_ref, p_ref, wg_ref, wp_ref, gp_ref, gf_ref, o_ref, ybuf, sem):
    i = pl.program_id(0)

    def issue(step, slot):
        base = step * (TAIL_ROWS * TOP_K)

        def body(r, carry):
            for k in range(TOP_K):
                pltpu.make_async_copy(y_hbm.at[pl.ds(dest_ref[base + r * TOP_K + k], 1)],
                                      ybuf.at[slot, k, pl.ds(r, 1)], sem.at[slot]).start()
            return carry
        lax.fori_loop(0, TAIL_ROWS, body, 0, unroll=2)

    @pl.when(i == 0)
    def _():
        issue(0, 0)

    @pl.when(i + 1 < pl.num_programs(0))
    def _():
        issue(i + 1, (i + 1) % 2)

    slot = i % 2
    for k in range(TOP_K):
        pltpu.make_async_copy(y_hbm.at[pl.ds(0, TAIL_ROWS)], ybuf.at[slot, k], sem.at[slot]).wait()

    gate = gate_ref[...]
    moe = ybuf[slot, 0] * gate[:, 0:1]
    for k in range(1, TOP_K):
        moe = moe + ybuf[slot, k] * gate[:, k:k + 1]
    x2 = x1_ref[...] + moe
    h3 = _rmsnorm(x2, gp_ref[...]).astype(BF16)
    ple_gate = jax.nn.sigmoid(jnp.dot(h3, wg_ref[...], preferred_element_type=F32))
    ple = jnp.dot(p_ref[...].astype(BF16), wp_ref[...], preferred_element_type=F32)
    x3 = x2 + ple_gate * ple
    o_ref[...] = _rmsnorm(x3, gf_ref[...])


def _tail(dest, y_sorted, rgate, x1, p_all, w_ple_gate_bf, w_ple_proj_bf, g_ple, g_final):
    t = x1.shape[0]
    rowblk = lambda w: pl.BlockSpec((TAIL_ROWS, w), lambda i, d: (i, 0))
    full = lambda r, c: pl.BlockSpec((r, c), lambda i, d: (0, 0))
    return pl.pallas_call(
        _tail_kernel,
        out_shape=jax.ShapeDtypeStruct((t, D_MODEL), F32),
        grid_spec=pltpu.PrefetchScalarGridSpec(
            num_scalar_prefetch=1,
            grid=(t // TAIL_ROWS,),
            in_specs=[
                pl.BlockSpec(memory_space=pl.ANY),
                rowblk(LANES), rowblk(D_MODEL), rowblk(PLE_DIM),
                full(D_MODEL, D_MODEL), full(PLE_DIM, D_MODEL), full(1, D_MODEL), full(1, D_MODEL)],
            out_specs=rowblk(D_MODEL),
            scratch_shapes=[pltpu.VMEM((2, TOP_K, TAIL_ROWS, D_MODEL), F32), pltpu.SemaphoreType.DMA((2,))],
        ),
        compiler_params=_cparams(("arbitrary",), 48),
        name="moe_tail",
    )(dest, y_sorted, rgate, x1, p_all, w_ple_gate_bf, w_ple_proj_bf, g_ple, g_final)


def _routing(ridx, rgate_unused, t):
    del rgate_unused
    m = t * TOP_K
    flat_e = ridx[:, :TOP_K].reshape(m)
    onehot = (flat_e[:, None] == jnp.arange(N_EXPERTS, dtype=I32)[None, :]).astype(I32)
    csum = jnp.cumsum(onehot, axis=0)
    rank = jnp.sum((csum - onehot) * onehot, axis=1)
    counts = csum[-1]
    nblk = (counts + MOE_SUB - 1) // MOE_SUB
    blk_end = jnp.cumsum(nblk)
    blk_start = blk_end - nblk
    dest = blk_start[flat_e] * MOE_SUB + rank
    n_blocks = -(-m // MOE_SUB) + N_EXPERTS
    n_blocks = -(-n_blocks // (DISPATCH_ROWS // MOE_SUB)) * (DISPATCH_ROWS // MOE_SUB)
    p = n_blocks * MOE_SUB
    flat_tok = jnp.arange(m, dtype=I32) // TOP_K
    slot_tok = jnp.zeros((p,), I32).at[dest].set(flat_tok)

    n_items = N_EXPERTS + -(-n_blocks // MOE_MAX_SUB)
    items_per_e = (nblk + MOE_MAX_SUB - 1) // MOE_MAX_SUB
    item_end = jnp.cumsum(items_per_e)
    total = item_end[-1]
    ids = jnp.arange(n_items, dtype=I32)
    active = ids < total
    ids_c = jnp.minimum(ids, total - 1)
    item_e = jnp.searchsorted(item_end, ids_c, side="right").astype(I32)
    local = ids_c - (item_end - items_per_e)[item_e]
    item_blk = blk_start[item_e] + local * MOE_MAX_SUB
    item_nsub = jnp.where(active, jnp.minimum(MOE_MAX_SUB, nblk[item_e] - local * MOE_MAX_SUB), 0)
    items = (item_e, item_blk.astype(I32), item_nsub.astype(I32), active.astype(I32),
             blk_end[-1:].astype(I32))
    return slot_tok, dest.astype(I32), items


def kernel(x_prompt, x_sample, cache_a_k, cache_a_v, cache_b_k, cache_b_v, p_prompt, p_sample, g_attn, w_in, b_in, sinks, g_out_a, g_out_b, w_out, b_out, g_ffn, w_router, b_router, w_gate_up, b_gate_up, w_down, b_down, g_ple, w_ple_gate, w_ple_proj, g_final):
    batch, seq, _ = x_prompt.shape
    nb, nseq, _ = x_sample.shape
    assert nseq == 1 and g_attn.shape[0] == 1
    tp, ts = batch * seq, nb * nseq
    row = lambda v: v.reshape(1, -1)

    w_in_bf = _cast_bf16(w_in[0])
    w_out_bf = _cast_bf16(w_out[0])
    w_pg_bf = _cast_bf16(w_ple_gate[0])
    w_pp_bf = _cast_bf16(w_ple_proj[0])

    tabs_p = _rope_tables(jnp.arange(seq, dtype=I32))
    tabs_s = _rope_tables(jnp.full((ts,), PAST_LEN, I32))
    sinks_lanes = jnp.repeat(sinks[0], HEAD_DIM).reshape(1, WIDTH)
    sinks_col = sinks[0].reshape(N_HEADS, 1)

    tm_p = 512
    xp = x_prompt.reshape(tp, D_MODEL)
    xs = x_sample.reshape(ts, D_MODEL)
    qkv4_p, kvb_p = _qkv_proj(xp, row(g_attn), w_in_bf, row(b_in), tabs_p, tm_p, seq // tm_p)
    qkv4_s, kvb_s = _qkv_proj(xs, row(g_attn), w_in_bf, row(b_in), tabs_s, ts, 1)

    pats = [_attn_a_prompt(qkv4_p, batch, seq, d) for _, d in DILATIONS]
    ob_p = _attn_b_prompt(qkv4_p, kvb_p, sinks_lanes, batch, seq)
    oa_s, ob_s = _attn_sample(qkv4_s, kvb_s, cache_a_k[0], cache_a_v[0], cache_b_k[0], cache_b_v[0],
                              sinks_col)

    post = functools.partial(_post_attention, g_out_a=row(g_out_a), g_out_b=row(g_out_b), w_out_bf=w_out_bf,
                             b_out=row(b_out), g_ffn=row(g_ffn), w_router=w_router[0], b_router=row(b_router))
    x1_p, h2_p, ridx_p, rgate_p = post(pats, ob_p, xp, tm=256)
    x1_s, h2_s, ridx_s, rgate_s = post([(oa_s, None)], ob_s, xs, tm=ts)

    cat = lambda a, b: jnp.concatenate([a, b], axis=0)
    x1, h2, ridx, rgate = cat(x1_p, x1_s), cat(h2_p, h2_s), cat(ridx_p, ridx_s), cat(rgate_p, rgate_s)
    t = tp + ts
    slot_tok, dest, items = _routing(ridx, rgate, t)
    x_sorted = _dispatch(slot_tok, h2)
    y_sorted = _moe_experts(items, x_sorted, w_gate_up[0], b_gate_up[0], w_down[0], b_down[0])
    p_all = cat(p_prompt[0].reshape(tp, PLE_DIM), p_sample[0].reshape(ts, PLE_DIM))
    y = _tail(dest, y_sorted, rgate, x1, p_all, w_pg_bf, w_pp_bf, row(g_ple), row(g_final))

    y_prompt = y[:tp].reshape(batch, seq, D_MODEL)
    y_sample = y[tp:].reshape(nb, nseq, D_MODEL)
    na, nbw = min(DILATIONS[-1][0], seq), min(WINDOW_B, seq)
    q4 = qkv4_p.reshape(batch, seq, 4, N_HEADS, HEAD_DIM)
    kv = kvb_p.reshape(batch, seq, 2, N_KV_B, HEAD_DIM)
    a_k_prompt = q4[:, seq - na:, 1][None]
    a_v_prompt = q4[:, seq - na:, 2][None]
    b_k_prompt = kv[:, seq - nbw:, 0][None]
    b_v_prompt = kv[:, seq - nbw:, 1][None]
    q4s = qkv4_s.reshape(nb, nseq, 4, N_HEADS, HEAD_DIM)
    kvs = kvb_s.reshape(nb, nseq, 2, N_KV_B, HEAD_DIM)
    return (y_prompt, y_sample, a_k_prompt, a_v_prompt, b_k_prompt, b_v_prompt,
            q4s[:, :, 1][None], q4s[:, :, 2][None], kvs[:, :, 0][None], kvs[:, :, 1][None])
```

```python
import functools

import jax
import jax.numpy as jnp
import numpy as np
from jax import lax
from jax.experimental import pallas as pl
from jax.experimental.pallas import tpu as pltpu

F32 = jnp.float32
BF16 = jnp.bfloat16
I32 = jnp.int32

D_MODEL = 2048
HEAD_DIM = 64
N_HEADS = 16
WIDTH = N_HEADS * HEAD_DIM
N_KV_B = 2
KV_WIDTH_B = N_KV_B * HEAD_DIM
DILATIONS = ((128, 1), (512, 4), (2048, 16))
WINDOW_B = 128
BLOCK = 128
N_EXPERTS = 32
TOP_K = 4
D_FF = 2048
SWIGLU_ALPHA = 1.702
SWIGLU_LIMIT = 7.0
PLE_DIM = 256
ROPE_THETA = 10000.0
NORM_EPS = 1e-5
MASK_VALUE = -1e30
SCALE = HEAD_DIM ** -0.5
PAST_LEN = 16384
QKV4_COLS = 4 * WIDTH
IN_COLS = QKV4_COLS + 2 * KV_WIDTH_B

LANES = 128
VMEM_LIMIT_CAP = 60 * 1024 * 1024

MOE_TF = 256
MOE_F = D_FF // MOE_TF
MOE_SUB = 128
MOE_MAX_SUB = 16
DISPATCH_ROWS = 256
TAIL_ROWS = 128


def _cparams(semantics, vmem_mb):
    return pltpu.CompilerParams(
        dimension_semantics=semantics,
        vmem_limit_bytes=min(vmem_mb * 1024 * 1024, VMEM_LIMIT_CAP))


def _rmsnorm(x, g):
    ms = jnp.mean(x * x, axis=-1, keepdims=True)
    return x * lax.rsqrt(ms + NORM_EPS) * g


def _cast_kernel(x_ref, o_ref):
    o_ref[...] = x_ref[...].astype(o_ref.dtype)


def _cast_bf16(w, rows_per_step=256):
    r, c = w.shape
    return pl.pallas_call(
        _cast_kernel,
        out_shape=jax.ShapeDtypeStruct((r, c), BF16),
        grid=(r // rows_per_step,),
        in_specs=[pl.BlockSpec((rows_per_step, c), lambda i: (i, 0))],
        out_specs=pl.BlockSpec((rows_per_step, c), lambda i: (i, 0)),
        compiler_params=_cparams(("parallel",), 32),
        name="cast_bf16",
    )(w)


def _rope_store(dst_ref, y, cos, sin_lo, sin_hi):
    for c in range(y.shape[1] // LANES):
        yc = y[:, c * LANES:(c + 1) * LANES]
        dst_ref[:, c * LANES:(c + 1) * LANES] = (
            yc * cos + pltpu.roll(yc, LANES - 32, 1) * sin_lo + pltpu.roll(yc, 32, 1) * sin_hi)


def _qkv_kernel(tiles_per_seq, first_win_tile, win_b,
                x_ref, g_ref, w_ref, wkv_ref, b_ref, bkv_ref, cos_ref, slo_ref, shi_ref,
                o4_ref, okv_ref, kt_ref, vt_ref, kvbt_ref, h_scr):
    i = pl.program_id(0)
    j = pl.program_id(1)
    tile_in_seq = i % tiles_per_seq
    in_window = tile_in_seq >= first_win_tile

    @pl.when(j == 0)
    def _():
        h_scr[...] = _rmsnorm(x_ref[...], g_ref[...]).astype(BF16)

    h = h_scr[...]
    y = jnp.dot(h, w_ref[...], preferred_element_type=F32) + b_ref[...]
    cos, slo, shi = cos_ref[...], slo_ref[...], shi_ref[...]

    @pl.when(j != 2)
    def _():
        _rope_store(o4_ref, y, cos, slo, shi)

    @pl.when((j == 1) & in_window)
    def _():
        kt_ref[0] = o4_ref[...].T

    @pl.when(j == 2)
    def _():
        o4_ref[...] = y

    @pl.when((j == 2) & in_window)
    def _():
        vt_ref[0] = y.T

    @pl.when(j == 3)
    def _():
        ykv = jnp.dot(h, wkv_ref[...], preferred_element_type=F32) + bkv_ref[...]
        _rope_store(okv_ref.at[:, :KV_WIDTH_B], ykv[:, :KV_WIDTH_B], cos, slo, shi)
        okv_ref[:, KV_WIDTH_B:] = ykv[:, KV_WIDTH_B:]

    @pl.when((j == 3) & (tile_in_seq == tiles_per_seq - 1))
    def _():
        kvbt_ref[0] = okv_ref[okv_ref.shape[0] - win_b:, :].T


def _qkv_proj(x, g_attn, w_in_bf, b_in, rope_tabs, tm, n_seq, win_a, win_b):
    t = x.shape[0]
    seq = t // n_seq
    tiles_per_seq = seq // tm
    assert seq % tm == 0 and win_a % tm == 0 and win_a <= seq and win_b <= tm
    first_win_tile = tiles_per_seq - win_a // tm
    cos, slo, shi = rope_tabs
    tab_blocks = cos.shape[0] // tm
    tab_map = lambda i, j: (i % tab_blocks, 0)
    win_map = lambda i, j: (i // tiles_per_seq, 0, jnp.maximum(i % tiles_per_seq - first_win_tile, 0))
    return pl.pallas_call(
        functools.partial(_qkv_kernel, tiles_per_seq, first_win_tile, win_b),
        out_shape=(jax.ShapeDtypeStruct((t, QKV4_COLS), F32),
                   jax.ShapeDtypeStruct((t, 2 * KV_WIDTH_B), F32),
                   jax.ShapeDtypeStruct((n_seq, WIDTH, win_a), F32),
                   jax.ShapeDtypeStruct((n_seq, WIDTH, win_a), F32),
                   jax.ShapeDtypeStruct((n_seq, 2 * KV_WIDTH_B, win_b), F32)),
        grid=(t // tm, 4),
        in_specs=[
            pl.BlockSpec((tm, D_MODEL), lambda i, j: (i, 0)),
            pl.BlockSpec((1, D_MODEL), lambda i, j: (0, 0)),
            pl.BlockSpec((D_MODEL, WIDTH), lambda i, j: (0, j)),
            pl.BlockSpec((D_MODEL, 2 * KV_WIDTH_B), lambda i, j: (0, QKV4_COLS // (2 * KV_WIDTH_B))),
            pl.BlockSpec((1, WIDTH), lambda i, j: (0, j)),
            pl.BlockSpec((1, 2 * KV_WIDTH_B), lambda i, j: (0, QKV4_COLS // (2 * KV_WIDTH_B))),
            pl.BlockSpec((tm, LANES), tab_map),
            pl.BlockSpec((tm, LANES), tab_map),
            pl.BlockSpec((tm, LANES), tab_map),
        ],
        out_specs=(pl.BlockSpec((tm, WIDTH), lambda i, j: (i, j)),
                   pl.BlockSpec((tm, 2 * KV_WIDTH_B), lambda i, j: (i, 0)),
                   pl.BlockSpec((1, WIDTH, tm), win_map),
                   pl.BlockSpec((1, WIDTH, tm), win_map),
                   pl.BlockSpec((1, 2 * KV_WIDTH_B, win_b), lambda i, j: (i // tiles_per_seq, 0, 0))),
        scratch_shapes=[pltpu.VMEM((tm, D_MODEL), BF16)],
        compiler_params=_cparams(("arbitrary", "arbitrary"), 48),
        name="qkv_proj",
    )(x, g_attn, w_in_bf, w_in_bf, b_in, b_in, cos, slo, shi)


def _rope_tables(pos):
    half = HEAD_DIM // 2
    inv = ROPE_THETA ** (-jnp.arange(half, dtype=F32) / half)
    ang = pos.astype(F32)[:, None] * inv[None, :]
    cos = jnp.tile(jnp.cos(ang), (1, LANES // half))
    sin = jnp.tile(jnp.sin(ang), (1, LANES // half))
    first = (jnp.arange(LANES) % HEAD_DIM) < half
    return cos, jnp.where(first, -sin, 0.0), jnp.where(first, 0.0, sin)


def _band_mask(mb):
    qi = lax.broadcasted_iota(I32, (BLOCK, 2 * BLOCK), 0)
    si = lax.broadcasted_iota(I32, (BLOCK, 2 * BLOCK), 1)
    dist = qi - si + BLOCK
    return (dist >= 0) & (dist <= BLOCK) & ((si >= BLOCK) | (mb > 0))


def _head_softmax_pv(qm, k2, v2, valid):
    s = lax.dot_general(qm, k2, (((1,), (1,)), ((), ())), preferred_element_type=F32)
    s = jnp.where(valid, s, MASK_VALUE)
    m = jnp.max(s, axis=1, keepdims=True)
    p = jnp.exp(s - m)
    l = jnp.sum(p, axis=1, keepdims=True)
    o = jnp.dot(p.astype(BF16), v2, preferred_element_type=F32) / l
    return o, m + jnp.log(l)


def _attn_a_kernel(q_ref, kc_ref, kp_ref, vc_ref, vp_ref, o_ref, lse_ref):
    valid = _band_mask(pl.program_id(2))
    first = lax.broadcasted_iota(I32, (BLOCK, LANES), 1) < HEAD_DIM
    for i in range(WIDTH // LANES):
        sl = slice(i * LANES, (i + 1) * LANES)
        qp = q_ref[:, sl] * SCALE
        k2 = jnp.concatenate([kp_ref[:, sl], kc_ref[:, sl]], axis=0).astype(BF16)
        v2 = jnp.concatenate([vp_ref[:, sl], vc_ref[:, sl]], axis=0).astype(BF16)
        o0, l0 = _head_softmax_pv(jnp.where(first, qp, 0.0).astype(BF16), k2, v2, valid)
        o1, l1 = _head_softmax_pv(jnp.where(first, 0.0, qp).astype(BF16), k2, v2, valid)
        o_ref[:, sl] = jnp.where(first, o0, o1)
        lse_ref[:, sl] = jnp.where(first, l0, l1)


def _attn_b_kernel(q_ref, kvc_ref, kvp_ref, sink_ref, o_ref):
    valid = _band_mask(pl.program_id(1))
    first = lax.broadcasted_iota(I32, (BLOCK, LANES), 1) < HEAD_DIM
    k2 = jnp.concatenate([kvp_ref[:, :KV_WIDTH_B], kvc_ref[:, :KV_WIDTH_B]], axis=0).astype(BF16)
    v2 = jnp.concatenate([kvp_ref[:, KV_WIDTH_B:], kvc_ref[:, KV_WIDTH_B:]], axis=0).astype(BF16)
    pairs_per_kv = (N_HEADS // N_KV_B) // 2
    for i in range(WIDTH // LANES):
        sl = slice(i * LANES, (i + 1) * LANES)
        kv_first = (i // pairs_per_kv) == 0
        live = first if kv_first else jnp.logical_not(first)
        qp = q_ref[:, sl] * SCALE
        qsw = pltpu.roll(qp, HEAD_DIM, 1)
        q_even = qp if kv_first else qsw
        q_odd = qsw if kv_first else qp
        oe, le = _head_softmax_pv(jnp.where(live, q_even, 0.0).astype(BF16), k2, v2, valid)
        oo, lo = _head_softmax_pv(jnp.where(live, q_odd, 0.0).astype(BF16), k2, v2, valid)
        if kv_first:
            o = jnp.where(first, oe, pltpu.roll(oo, HEAD_DIM, 1))
        else:
            o = jnp.where(first, pltpu.roll(oe, HEAD_DIM, 1), oo)
        lse = jnp.where(first, le, lo)
        o_ref[:, sl] = o * jax.nn.sigmoid(lse - sink_ref[:, sl])


def _attn_a_prompt(qkv4, batch, seq, d):
    t = batch * seq
    nblk = seq // d // BLOCK
    view = qkv4.reshape(t // d, d * QKV4_COLS)
    cur = lambda c: (lambda b, r, mb: (b * nblk + mb, 4 * r + c))
    prev = lambda c: (lambda b, r, mb: (b * nblk + jnp.maximum(mb - 1, 0), 4 * r + c))
    blk = lambda f: pl.BlockSpec((BLOCK, WIDTH), f)
    out_map = lambda b, r, mb: (b * nblk + mb, r)
    o, lse = pl.pallas_call(
        _attn_a_kernel,
        out_shape=(jax.ShapeDtypeStruct((t // d, d * WIDTH), F32),
                   jax.ShapeDtypeStruct((t // d, d * WIDTH), F32)),
        grid=(batch, d, nblk),
        in_specs=[blk(cur(0)), blk(cur(1)), blk(prev(1)), blk(cur(2)), blk(prev(2))],
        out_specs=(blk(out_map), blk(out_map)),
        compiler_params=_cparams(("parallel", "parallel", "arbitrary"), 32),
        name=f"attn_a_d{d}",
    )(view, view, view, view, view)
    return o.reshape(t, WIDTH), lse.reshape(t, WIDTH)


def _attn_b_prompt(qkv4, kvb, sinks_lanes, batch, seq):
    t = batch * seq
    nblk = seq // BLOCK
    cur = lambda b, mb: (b * nblk + mb, 0)
    prev = lambda b, mb: (b * nblk + jnp.maximum(mb - 1, 0), 0)
    return pl.pallas_call(
        _attn_b_kernel,
        out_shape=jax.ShapeDtypeStruct((t, WIDTH), F32),
        grid=(batch, nblk),
        in_specs=[
            pl.BlockSpec((BLOCK, WIDTH), lambda b, mb: (b * nblk + mb, 3)),
            pl.BlockSpec((BLOCK, 2 * KV_WIDTH_B), cur),
            pl.BlockSpec((BLOCK, 2 * KV_WIDTH_B), prev),
            pl.BlockSpec((1, WIDTH), lambda b, mb: (0, 0)),
        ],
        out_specs=pl.BlockSpec((BLOCK, WIDTH), cur),
        compiler_params=_cparams(("parallel", "arbitrary"), 32),
        name="attn_b",
    )(qkv4, kvb, kvb, sinks_lanes)


def _hi_dot(a, b, dims=(((1,), (0,)), ((), ()))):
    return lax.dot_general(a, b, dims, preferred_element_type=F32, precision=lax.Precision.HIGHEST)


def _attn_sample_kernel(cnt_ref, qa_ref, kn_ref, vn_ref, qb_ref, kbn_ref, vbn_ref, sink_ref,
                        kt_ref, vt_ref, kbt_ref, vbt_ref, oa_ref, ob_ref):
    hps = N_HEADS // N_KV_B
    nt = (((1,), (1,)), ((), ()))
    head = lax.broadcasted_iota(I32, (hps, 1), 0)
    cnt = cnt_ref[...]
    n_pat = float(len(DILATIONS))

    qa = qa_ref[0] * SCALE
    qa_bf = qa.astype(BF16)
    s = jnp.zeros((hps, cnt.shape[1]), F32)
    for hl in range(hps):
        s_hl = jnp.dot(qa_bf, kt_ref[0, hl].astype(BF16), preferred_element_type=F32)
        s = jnp.where(head == hl, s_hl, s)
    s = jnp.where(cnt > 0.0, s, MASK_VALUE)
    s_new = jnp.sum(qa * kn_ref[0], axis=1, keepdims=True)
    m = jnp.maximum(jnp.max(s, axis=1, keepdims=True), s_new)
    e = cnt * jnp.exp(s - m)
    e_new = n_pat * jnp.exp(s_new - m)
    den = jnp.sum(e, axis=1, keepdims=True) + e_new
    e_bf = e.astype(BF16)
    pv = jnp.zeros((hps, HEAD_DIM), F32)
    for hl in range(hps):
        o_hl = lax.dot_general(e_bf, vt_ref[0, hl].astype(BF16), nt, preferred_element_type=F32)
        pv = jnp.where(head == hl, o_hl, pv)
    oa_ref[0] = (pv + e_new * vn_ref[0]) / den

    qb = qb_ref[0] * SCALE
    sb = _hi_dot(qb, kbt_ref[0, 0])
    sn = jnp.sum(qb * kbn_ref[0, 0], axis=1, keepdims=True)
    mm = jnp.maximum(jnp.max(sb, axis=1, keepdims=True), sn)
    eb = jnp.exp(sb - mm)
    en = jnp.exp(sn - mm)
    l = jnp.sum(eb, axis=1, keepdims=True) + en
    ob = (_hi_dot(eb, vbt_ref[0, 0], nt) + en * vbn_ref[0, 0]) / l
    lse = mm + jnp.log(l)
    ob_ref[0] = ob * jax.nn.sigmoid(lse - sink_ref[0])


def _attn_sample(qkv4_s, kvb_s, cache_a_k, cache_a_v, cache_b_k, cache_b_v, sinks):
    nb = qkv4_s.shape[0]
    wa = cache_a_k.shape[1]
    hps = N_HEADS // N_KV_B
    assert cache_b_k.shape[1] == WINDOW_B and all(wa >= w and wa % d == 0 for w, d in DILATIONS)
    dist = wa - np.arange(wa)
    cnt = sum(((dist % d == 0) & (dist <= w)).astype(np.float32) for w, d in DILATIONS).reshape(1, wa)
    q4 = qkv4_s.reshape(nb, 4 * N_HEADS, HEAD_DIM)
    kvn = kvb_s.reshape(nb, 2 * N_KV_B, 1, HEAD_DIM)
    to_row_minor = lambda c: jnp.transpose(c, (0, 2, 3, 1))
    q_spec = lambda comp: pl.BlockSpec((1, hps, HEAD_DIM), lambda b, hh: (b, comp * N_KV_B + hh, 0))
    n_spec = lambda comp: pl.BlockSpec((1, 1, 1, HEAD_DIM), lambda b, hh: (b, comp * N_KV_B + hh, 0, 0))
    a_spec = pl.BlockSpec((1, hps, HEAD_DIM, wa), lambda b, hh: (b, hh, 0, 0))
    b_spec = pl.BlockSpec((1, 1, HEAD_DIM, WINDOW_B), lambda b, hh: (b, hh, 0, 0))
    o_spec = pl.BlockSpec((1, hps, HEAD_DIM), lambda b, hh: (b, hh, 0))
    oa, ob = pl.pallas_call(
        _attn_sample_kernel,
        out_shape=(jax.ShapeDtypeStruct((nb, N_HEADS, HEAD_DIM), F32),
                   jax.ShapeDtypeStruct((nb, N_HEADS, HEAD_DIM), F32)),
        grid=(nb, N_KV_B),
        in_specs=[
            pl.BlockSpec((1, wa), lambda b, hh: (0, 0)),
            q_spec(0), q_spec(1), q_spec(2), q_spec(3), n_spec(0), n_spec(1),
            pl.BlockSpec((1, hps, 1), lambda b, hh: (hh, 0, 0)),
            a_spec, a_spec, b_spec, b_spec,
        ],
        out_specs=(o_spec, o_spec),
        compiler_params=_cparams(("parallel", "parallel"), 40),
        name="attn_sample",
    )(jnp.asarray(cnt), q4, q4, q4, q4, kvn, kvn, sinks.reshape(N_KV_B, hps, 1),
      to_row_minor(cache_a_k), to_row_minor(cache_a_v), to_row_minor(cache_b_k), to_row_minor(cache_b_v))
    return oa.reshape(nb, WIDTH), ob.reshape(nb, WIDTH)


def _split_bf16(x):
    hi = x.astype(BF16)
    lo = (x - hi.astype(F32)).astype(BF16)
    return hi, lo


def _post_kernel(n_pat, *refs):
    pat_refs = refs[:2 * n_pat] if n_pat > 1 else refs[:1]
    n_in = len(pat_refs)
    (ob_ref, x_ref, ga_ref, gb_ref, w_ref, b_ref, gf_ref, wr_ref, br_ref,
     x1_ref, h2_ref, ridx_ref, rgate_ref) = refs[n_in:]
    if n_pat > 1:
        outs = [pat_refs[2 * p][...] for p in range(n_pat)]
        lses = [pat_refs[2 * p + 1][...] for p in range(n_pat)]
        m = functools.reduce(jnp.maximum, lses)
        ws = [jnp.exp(l - m) for l in lses]
        oa = sum(w * o for w, o in zip(ws, outs)) / sum(ws)
    else:
        oa = pat_refs[0][...]
    na = _rmsnorm(oa, ga_ref[...]).astype(BF16)
    nb = _rmsnorm(ob_ref[...], gb_ref[...]).astype(BF16)
    c = jnp.concatenate([na, nb], axis=1)
    x1 = x_ref[...] + jnp.dot(c, w_ref[...], preferred_element_type=F32) + b_ref[...]
    x1_ref[...] = x1
    h2 = _rmsnorm(x1, gf_ref[...])
    h2_ref[...] = h2

    h_hi, h_lo = _split_bf16(h2)
    w_hi, w_lo = _split_bf16(wr_ref[...])
    dot = lambda a, b: jnp.dot(a, b, preferred_element_type=F32)
    logits = dot(h_hi, w_hi) + (dot(h_hi, w_lo) + dot(h_lo, w_hi)) + br_ref[...]

    tm = logits.shape[0]
    eidx = lax.broadcasted_iota(I32, (tm, N_EXPERTS), 1)
    lane = lax.broadcasted_iota(I32, (tm, LANES), 1)
    work = logits
    vals, idxs = [], []
    for _ in range(TOP_K):
        v = jnp.max(work, axis=1, keepdims=True)
        i = jnp.min(jnp.where(work == v, eidx, N_EXPERTS), axis=1, keepdims=True)
        vals.append(v)
        idxs.append(i)
        work = jnp.where(eidx == i, -jnp.inf, work)
    es = [jnp.exp(v - vals[0]) for v in vals]
    den = sum(es)
    ridx = jnp.zeros((tm, LANES), I32)
    rgate = jnp.zeros((tm, LANES), F32)
    for k in range(TOP_K):
        ridx = jnp.where(lane == k, idxs[k], ridx)
        rgate = jnp.where(lane == k, es[k] / den, rgate)
    ridx_ref[...] = ridx
    rgate_ref[...] = rgate


def _post_attention(pats, ob, x, g_out_a, g_out_b, w_out_bf, b_out, g_ffn, w_router, b_router, tm):
    t = x.shape[0]
    n_pat = len(pats)
    pat_arrays = [a for pair in pats for a in pair] if n_pat > 1 else [pats[0][0]]
    rowblk = lambda w: pl.BlockSpec((tm, w), lambda i: (i, 0))
    full = lambda r, c: pl.BlockSpec((r, c), lambda i: (0, 0))
    return pl.pallas_call(
        functools.partial(_post_kernel, n_pat),
        out_shape=(jax.ShapeDtypeStruct((t, D_MODEL), F32), jax.ShapeDtypeStruct((t, D_MODEL), F32),
                   jax.ShapeDtypeStruct((t, LANES), I32), jax.ShapeDtypeStruct((t, LANES), F32)),
        grid=(t // tm,),
        in_specs=[rowblk(WIDTH)] * len(pat_arrays) + [
            rowblk(WIDTH), rowblk(D_MODEL), full(1, WIDTH), full(1, WIDTH),
            full(D_MODEL, D_MODEL), full(1, D_MODEL), full(1, D_MODEL),
            full(D_MODEL, N_EXPERTS), full(1, N_EXPERTS)],
        out_specs=(rowblk(D_MODEL), rowblk(D_MODEL), rowblk(LANES), rowblk(LANES)),
        compiler_params=_cparams(("parallel",), 56),
        name=f"post_attention_{n_pat}",
    )(*pat_arrays, ob, x, g_out_a, g_out_b, w_out_bf, b_out, g_ffn, w_router, b_router)


def _dispatch_kernel(tok_ref, h_hbm, o_ref, buf, sem):
    i = pl.program_id(0)

    def issue(step, slot):
        base = step * DISPATCH_ROWS

        def body(r, carry):
            pltpu.make_async_copy(h_hbm.at[pl.ds(tok_ref[base + r], 1)],
                                  buf.at[slot, pl.ds(r, 1)], sem.at[slot]).start()
            return carry
        lax.fori_loop(0, DISPATCH_ROWS, body, 0, unroll=8)

    @pl.when(i == 0)
    def _():
        issue(0, 0)

    @pl.when(i + 1 < pl.num_programs(0))
    def _():
        issue(i + 1, (i + 1) % 2)

    slot = i % 2
    pltpu.make_async_copy(h_hbm.at[pl.ds(0, DISPATCH_ROWS)], buf.at[slot], sem.at[slot]).wait()
    o_ref[...] = buf[slot].astype(BF16)


def _dispatch(slot_tok, h2):
    p = slot_tok.shape[0]
    return pl.pallas_call(
        _dispatch_kernel,
        out_shape=jax.ShapeDtypeStruct((p, D_MODEL), BF16),
        grid_spec=pltpu.PrefetchScalarGridSpec(
            num_scalar_prefetch=1,
            grid=(p // DISPATCH_ROWS,),
            in_specs=[pl.BlockSpec(memory_space=pl.ANY)],
            out_specs=pl.BlockSpec((DISPATCH_ROWS, D_MODEL), lambda i, tok: (i, 0)),
            scratch_shapes=[pltpu.VMEM((2, DISPATCH_ROWS, D_MODEL), F32), pltpu.SemaphoreType.DMA((2,))],
        ),
        compiler_params=_cparams(("arbitrary",), 32),
        name="moe_dispatch",
    )(slot_tok, h2)


def _moe_kernel(ie_ref, iblk_ref, insub_ref, iact_ref, used_ref,
                x_hbm, wg_ref, wu_ref, wd_ref, bg_ref, bu_ref, bd_ref, y_hbm,
                xbuf, act, wa_bf, wb_bf, ystage, xsem, ysem):
    it = pl.program_id(0)
    s = pl.program_id(1)
    nsub = insub_ref[it]
    row0 = iblk_ref[it] * MOE_SUB

    @pl.when((it == pl.num_programs(0) - 1) & (s == pl.num_programs(1) - 1))
    def _():
        n_blocks = y_hbm.shape[0] // MOE_SUB
        ystage[pl.ds(0, MOE_SUB), :] = jnp.zeros((MOE_SUB, MOE_TF), F32)

        def z_copy(b, n):
            return pltpu.make_async_copy(
                ystage.at[pl.ds(0, MOE_SUB), :],
                y_hbm.at[pl.ds(b * MOE_SUB, MOE_SUB), pl.ds(n * MOE_TF, MOE_TF)], ysem)

        def fill(b, c):
            for n in range(MOE_F):
                z_copy(b, n).start()
            for n in range(MOE_F):
                z_copy(b, n).wait()
            return c
        lax.fori_loop(used_ref[0], n_blocks, fill, 0)

    def x_copy(j):
        return pltpu.make_async_copy(x_hbm.at[pl.ds(row0 + j * MOE_SUB, MOE_SUB)],
                                     xbuf.at[pl.ds(j * MOE_SUB, MOE_SUB)], xsem)

    def for_row_chunks(fn):
        big = nsub // 4

        def body(j, c):
            fn(pl.multiple_of(j * (4 * MOE_SUB), 4 * MOE_SUB), 4 * MOE_SUB)
            return c
        lax.fori_loop(0, big, body, 0)
        rem = nsub - 4 * big
        has2 = rem >= 2

        @pl.when(has2)
        def _():
            fn(pl.multiple_of(big * (4 * MOE_SUB), MOE_SUB), 2 * MOE_SUB)

        @pl.when((rem & 1) == 1)
        def _():
            fn(pl.multiple_of((4 * big + jnp.where(has2, 2, 0)) * MOE_SUB, MOE_SUB), MOE_SUB)

    @pl.when((s == 0) & (nsub > 0))
    def _():
        def start(j, c):
            x_copy(j).start()
            return c
        lax.fori_loop(0, nsub, start, 0)

        def wait(j, c):
            x_copy(j).wait()
            return c
        lax.fori_loop(0, nsub, wait, 0)

    @pl.when((s < MOE_F) & (nsub > 0))
    def _():
        wa_bf[...] = wg_ref[...].astype(BF16)
        wb_bf[...] = wu_ref[...].astype(BF16)
        bg, bu = bg_ref[...], bu_ref[...]

        def gate_up(r0, rows):
            x = xbuf[pl.ds(r0, rows), :]
            g = jnp.dot(x, wa_bf[...], preferred_element_type=F32) + bg
            u = jnp.dot(x, wb_bf[...], preferred_element_type=F32) + bu
            g = jnp.minimum(g, SWIGLU_LIMIT)
            u = jnp.clip(u, -SWIGLU_LIMIT, SWIGLU_LIMIT)
            a = g * jax.nn.sigmoid(SWIGLU_ALPHA * g) * (u + 1.0)
            act[s, pl.ds(r0, rows), :] = a.astype(BF16)
        for_row_chunks(gate_up)

    @pl.when((s >= MOE_F) & (nsub > 0))
    def _():
        n = s - MOE_F
        wa_bf[...] = wd_ref[...].astype(BF16)
        bd = bd_ref[...]
        c0 = pl.multiple_of(n * MOE_TF, MOE_TF)

        def y_copy(r0, rows):
            return pltpu.make_async_copy(
                ystage.at[pl.ds(r0, rows), :],
                y_hbm.at[pl.ds(row0 + r0, rows), pl.ds(c0, MOE_TF)], ysem)

        def down(r0, rows):
            a = jnp.concatenate([act[f, pl.ds(r0, rows), :] for f in range(MOE_F)], axis=1)
            ystage[pl.ds(r0, rows), :] = jnp.dot(a, wa_bf[...], preferred_element_type=F32) + bd
            y_copy(r0, rows).start()
        for_row_chunks(down)
        for_row_chunks(lambda r0, rows: y_copy(r0, rows).wait())


def _moe_experts(items, x_sorted, w_gate_up, b_gate_up, w_down, b_down):
    item_e, item_blk, item_nsub, item_act, used_blocks = items
    n_items = item_e.shape[0]
    p = x_sorted.shape[0]
    rmax = MOE_MAX_SUB * MOE_SUB
    last = MOE_F - 1
    f_of = lambda s, act: jnp.where(act == 1, jnp.minimum(s, last), last)
    n_of = lambda s, act: jnp.where(act == 1, jnp.maximum(s - MOE_F, 0), last)
    wg_map = lambda i, s, ie, ib, ins, ia, iu: (ie[i], 0, f_of(s, ia[i]))
    wu_map = lambda i, s, ie, ib, ins, ia, iu: (ie[i], 0, MOE_F + f_of(s, ia[i]))
    wd_map = lambda i, s, ie, ib, ins, ia, iu: (ie[i], 0, n_of(s, ia[i]))
    bgu = b_gate_up.reshape(N_EXPERTS, 1, 2 * D_FF)
    bd = b_down.reshape(N_EXPERTS, 1, D_MODEL)
    return pl.pallas_call(
        _moe_kernel,
        out_shape=jax.ShapeDtypeStruct((p, D_MODEL), F32),
        grid_spec=pltpu.PrefetchScalarGridSpec(
            num_scalar_prefetch=5,
            grid=(n_items, 2 * MOE_F),
            in_specs=[
                pl.BlockSpec(memory_space=pl.ANY),
                pl.BlockSpec((None, D_MODEL, MOE_TF), wg_map),
                pl.BlockSpec((None, D_MODEL, MOE_TF), wu_map),
                pl.BlockSpec((None, D_FF, MOE_TF), wd_map),
                pl.BlockSpec((None, 1, MOE_TF), wg_map),
                pl.BlockSpec((None, 1, MOE_TF), wu_map),
                pl.BlockSpec((None, 1, MOE_TF), wd_map),
            ],
            out_specs=pl.BlockSpec(memory_space=pl.ANY),
            scratch_shapes=[
                pltpu.VMEM((rmax, D_MODEL), BF16),
                pltpu.VMEM((MOE_F, rmax, MOE_TF), BF16),
                pltpu.VMEM((D_MODEL, MOE_TF), BF16),
                pltpu.VMEM((D_MODEL, MOE_TF), BF16),
                pltpu.VMEM((rmax, MOE_TF), F32),
                pltpu.SemaphoreType.DMA,
                pltpu.SemaphoreType.DMA,
            ],
        ),
        compiler_params=_cparams(("arbitrary", "arbitrary"), 56),
        name="moe_experts",
    )(item_e, item_blk, item_nsub, item_act, used_blocks, x_sorted, w_gate_up, w_gate_up, w_down, bgu, bgu, bd)


def _tail_kernel(dest_ref, y_hbm, gate_ref, x1_ref, p_ref, wg_ref, wp_ref, gp_ref, gf_ref, o_ref, ybuf, sem):
    i = pl.program_id(0)

    def issue(step, slot):
        base = step * (TAIL_ROWS * TOP_K)

        def body(r, carry):
            for k in range(TOP_K):
                pltpu.make_async_copy(y_hbm.at[pl.ds(dest_ref[base + r * TOP_K + k], 1)],
                                      ybuf.at[slot, k, pl.ds(r, 1)], sem.at[slot]).start()
            return carry
        lax.fori_loop(0, TAIL_ROWS, body, 0, unroll=2)

    @pl.when(i == 0)
    def _():
        issue(0, 0)

    @pl.when(i + 1 < pl.num_programs(0))
    def _():
        issue(i + 1, (i + 1) % 2)

    slot = i % 2
    for k in range(TOP_K):
        pltpu.make_async_copy(y_hbm.at[pl.ds(0, TAIL_ROWS)], ybuf.at[slot, k], sem.at[slot]).wait()

    gate = gate_ref[...]
    moe = ybuf[slot, 0] * gate[:, 0:1]
    for k in range(1, TOP_K):
        moe = moe + ybuf[slot, k] * gate[:, k:k + 1]
    x2 = x1_ref[...] + moe
    h3 = _rmsnorm(x2, gp_ref[...]).astype(BF16)
    ple_gate = jax.nn.sigmoid(jnp.dot(h3, wg_ref[...], preferred_element_type=F32))
    ple = jnp.dot(p_ref[...].astype(BF16), wp_ref[...], preferred_element_type=F32)
    x3 = x2 + ple_gate * ple
    o_ref[...] = _rmsnorm(x3, gf_ref[...])


def _tail(dest, y_sorted, rgate, x1, p_all, w_ple_gate_bf, w_ple_proj_bf, g_ple, g_final):
    t = x1.shape[0]
    rowblk = lambda w: pl.BlockSpec((TAIL_ROWS, w), lambda i, d: (i, 0))
    full = lambda r, c: pl.BlockSpec((r, c), lambda i, d: (0, 0))
    return pl.pallas_call(
        _tail_kernel,
        out_shape=jax.ShapeDtypeStruct((t, D_MODEL), F32),
        grid_spec=pltpu.PrefetchScalarGridSpec(
            num_scalar_prefetch=1,
            grid=(t // TAIL_ROWS,),
            in_specs=[
                pl.BlockSpec(memory_space=pl.ANY),
                rowblk(LANES), rowblk(D_MODEL), rowblk(PLE_DIM),
                full(D_MODEL, D_MODEL), full(PLE_DIM, D_MODEL), full(1, D_MODEL), full(1, D_MODEL)],
            out_specs=rowblk(D_MODEL),
            scratch_shapes=[pltpu.VMEM((2, TOP_K, TAIL_ROWS, D_MODEL), F32), pltpu.SemaphoreType.DMA((2,))],
        ),
        compiler_params=_cparams(("arbitrary",), 48),
        name="moe_tail",
    )(dest, y_sorted, rgate, x1, p_all, w_ple_gate_bf, w_ple_proj_bf, g_ple, g_final)


def _routing(ridx, rgate_unused, t):
    del rgate_unused
    m = t * TOP_K
    flat_e = ridx[:, :TOP_K].reshape(m)
    onehot = (flat_e[:, None] == jnp.arange(N_EXPERTS, dtype=I32)[None, :]).astype(I32)
    csum = jnp.cumsum(onehot, axis=0)
    rank = jnp.sum((csum - onehot) * onehot, axis=1)
    counts = csum[-1]
    nblk = (counts + MOE_SUB - 1) // MOE_SUB
    blk_end = jnp.cumsum(nblk)
    blk_start = blk_end - nblk
    dest = blk_start[flat_e] * MOE_SUB + rank
    n_blocks = -(-m // MOE_SUB) + N_EXPERTS
    n_blocks = -(-n_blocks // (DISPATCH_ROWS // MOE_SUB)) * (DISPATCH_ROWS // MOE_SUB)
    p = n_blocks * MOE_SUB
    flat_tok = jnp.arange(m, dtype=I32) // TOP_K
    slot_tok = jnp.zeros((p,), I32).at[dest].set(flat_tok)

    n_items = N_EXPERTS + -(-n_blocks // MOE_MAX_SUB)
    items_per_e = (nblk + MOE_MAX_SUB - 1) // MOE_MAX_SUB
    item_end = jnp.cumsum(items_per_e)
    total = item_end[-1]
    ids = jnp.arange(n_items, dtype=I32)
    active = ids < total
    ids_c = jnp.minimum(ids, total - 1)
    item_e = jnp.searchsorted(item_end, ids_c, side="right").astype(I32)
    local = ids_c - (item_end - items_per_e)[item_e]
    item_blk = blk_start[item_e] + local * MOE_MAX_SUB
    item_nsub = jnp.where(active, jnp.minimum(MOE_MAX_SUB, nblk[item_e] - local * MOE_MAX_SUB), 0)
    items = (item_e, item_blk.astype(I32), item_nsub.astype(I32), active.astype(I32),
             blk_end[-1:].astype(I32))
    return slot_tok, dest.astype(I32), items


def kernel(x_prompt, x_sample, cache_a_k, cache_a_v, cache_b_k, cache_b_v, p_prompt, p_sample, g_attn, w_in, b_in, sinks, g_out_a, g_out_b, w_out, b_out, g_ffn, w_router, b_router, w_gate_up, b_gate_up, w_down, b_down, g_ple, w_ple_gate, w_ple_proj, g_final):
    batch, seq, _ = x_prompt.shape
    nb, nseq, _ = x_sample.shape
    assert nseq == 1 and g_attn.shape[0] == 1
    tp, ts = batch * seq, nb * nseq
    row = lambda v: v.reshape(1, -1)

    w_in_bf = _cast_bf16(w_in[0])
    w_out_bf = _cast_bf16(w_out[0])
    w_pg_bf = _cast_bf16(w_ple_gate[0])
    w_pp_bf = _cast_bf16(w_ple_proj[0])

    tabs_p = _rope_tables(jnp.arange(seq, dtype=I32))
    tabs_s = _rope_tables(jnp.full((ts,), PAST_LEN, I32))
    sinks_lanes = jnp.repeat(sinks[0], HEAD_DIM).reshape(1, WIDTH)

    tm_p = 512
    xp = x_prompt.reshape(tp, D_MODEL)
    xs = x_sample.reshape(ts, D_MODEL)
    win_a, win_b = min(DILATIONS[-1][0], seq), min(WINDOW_B, seq)
    qkv4_p, kvb_p, kt_p, vt_p, kvbt_p = _qkv_proj(xp, row(g_attn), w_in_bf, row(b_in), tabs_p, tm_p,
                                                   batch, win_a, win_b)
    qkv4_s, kvb_s, kt_s, vt_s, kvbt_s = _qkv_proj(xs, row(g_attn), w_in_bf, row(b_in), tabs_s, ts, 1, ts, ts)

    pats = [_attn_a_prompt(qkv4_p, batch, seq, d) for _, d in DILATIONS]
    ob_p = _attn_b_prompt(qkv4_p, kvb_p, sinks_lanes, batch, seq)
    oa_s, ob_s = _attn_sample(qkv4_s, kvb_s, cache_a_k[0], cache_a_v[0], cache_b_k[0], cache_b_v[0], sinks[0])

    post = functools.partial(_post_attention, g_out_a=row(g_out_a), g_out_b=row(g_out_b), w_out_bf=w_out_bf,
                             b_out=row(b_out), g_ffn=row(g_ffn), w_router=w_router[0], b_router=row(b_router))
    x1_p, h2_p, ridx_p, rgate_p = post(pats, ob_p, xp, tm=256)
    x1_s, h2_s, ridx_s, rgate_s = post([(oa_s, None)], ob_s, xs, tm=ts)

    cat = lambda a, b: jnp.concatenate([a, b], axis=0)
    x1, h2, ridx, rgate = cat(x1_p, x1_s), cat(h2_p, h2_s), cat(ridx_p, ridx_s), cat(rgate_p, rgate_s)
    t = tp + ts
    slot_tok, dest, items = _routing(ridx, rgate, t)
    x_sorted = _dispatch(slot_tok, h2)
    y_sorted = _moe_experts(items, x_sorted, w_gate_up[0], b_gate_up[0], w_down[0], b_down[0])
    p_all = cat(p_prompt[0].reshape(tp, PLE_DIM), p_sample[0].reshape(ts, PLE_DIM))
    y = _tail(dest, y_sorted, rgate, x1, p_all, w_pg_bf, w_pp_bf, row(g_ple), row(g_final))

    y_prompt = y[:tp].reshape(batch, seq, D_MODEL)
    y_sample = y[tp:].reshape(nb, nseq, D_MODEL)

    def rows_out(t3, heads):
        n, _, rows = t3.shape
        return t3.reshape(n, heads, HEAD_DIM, rows).transpose(0, 3, 1, 2)[None]

    def sample_out(t3, heads):
        return t3.reshape(heads, HEAD_DIM, nb).transpose(2, 0, 1)[None, :, None]

    return (y_prompt, y_sample,
            rows_out(kt_p, N_HEADS), rows_out(vt_p, N_HEADS),
            rows_out(kvbt_p[:, :KV_WIDTH_B], N_KV_B), rows_out(kvbt_p[:, KV_WIDTH_B:], N_KV_B),
            sample_out(kt_s, N_HEADS), sample_out(vt_s, N_HEADS),
            sample_out(kvbt_s[:, :KV_WIDTH_B], N_KV_B), sample_out(kvbt_s[:, KV_WIDTH_B:], N_KV_B))
```

```python
import functools

import jax
import jax.numpy as jnp
import numpy as np
from jax import lax
from jax.experimental import pallas as pl
from jax.experimental.pallas import tpu as pltpu

F32 = jnp.float32
BF16 = jnp.bfloat16
I32 = jnp.int32

D_MODEL = 2048
HEAD_DIM = 64
N_HEADS = 16
WIDTH = N_HEADS * HEAD_DIM
N_KV_B = 2
KV_WIDTH_B = N_KV_B * HEAD_DIM
DILATIONS = ((128, 1), (512, 4), (2048, 16))
WINDOW_B = 128
BLOCK = 128
N_EXPERTS = 32
TOP_K = 4
D_FF = 2048
SWIGLU_ALPHA = 1.702
SWIGLU_LIMIT = 7.0
PLE_DIM = 256
ROPE_THETA = 10000.0
NORM_EPS = 1e-5
MASK_VALUE = -1e30
SCALE = HEAD_DIM ** -0.5
PAST_LEN = 16384
QKV4_COLS = 4 * WIDTH
IN_COLS = QKV4_COLS + 2 * KV_WIDTH_B

LANES = 128
VMEM_LIMIT_CAP = 60 * 1024 * 1024

MOE_TF = 512
MOE_F = D_FF // MOE_TF
MOE_SUB = 128
MOE_MAX_SUB = 12
DISPATCH_ROWS = 256
TAIL_ROWS = 128


def _cparams(semantics, vmem_mb):
    return pltpu.CompilerParams(
        dimension_semantics=semantics,
        vmem_limit_bytes=min(vmem_mb * 1024 * 1024, VMEM_LIMIT_CAP))


def _rmsnorm(x, g):
    ms = jnp.mean(x * x, axis=-1, keepdims=True)
    return x * lax.rsqrt(ms + NORM_EPS) * g


def _cast_kernel(x_ref, o_ref):
    o_ref[...] = x_ref[...].astype(o_ref.dtype)


def _cast_bf16(w, rows_per_step=256):
    r, c = w.shape
    return pl.pallas_call(
        _cast_kernel,
        out_shape=jax.ShapeDtypeStruct((r, c), BF16),
        grid=(r // rows_per_step,),
        in_specs=[pl.BlockSpec((rows_per_step, c), lambda i: (i, 0))],
        out_specs=pl.BlockSpec((rows_per_step, c), lambda i: (i, 0)),
        compiler_params=_cparams(("parallel",), 32),
        name="cast_bf16",
    )(w)


def _rope(yc, cos, sin_lo, sin_hi):
    return yc * cos + pltpu.roll(yc, LANES - 32, 1) * sin_lo + pltpu.roll(yc, 32, 1) * sin_hi


def _qkv_kernel(tiles_per_seq, first_win_tile, win_b,
                x_ref, g_ref, w_ref, wkv_ref, b_ref, bkv_ref, cos_ref, slo_ref, shi_ref,
                qkv_ref, kvb_ref, kt_ref, vt_ref, kvbt_ref, h_scr):
    i = pl.program_id(0)
    j = pl.program_id(1)
    tile_in_seq = i % tiles_per_seq
    in_window = tile_in_seq >= first_win_tile
    n_chunks = WIDTH // LANES
    chunk = lambda v, c: v[:, c * LANES:(c + 1) * LANES]

    @pl.when(j == 0)
    def _():
        h_scr[...] = _rmsnorm(x_ref[...], g_ref[...]).astype(BF16)

    h = h_scr[...]
    y = jnp.dot(h, w_ref[...], preferred_element_type=F32) + b_ref[...]
    cos, slo, shi = cos_ref[...], slo_ref[...], shi_ref[...]

    @pl.when(j != 2)
    def _():
        for c in range(n_chunks):
            qkv_ref[c] = _rope(chunk(y, c), cos, slo, shi)

    @pl.when((j == 1) & in_window)
    def _():
        for c in range(n_chunks):
            kt_ref[0, c * LANES:(c + 1) * LANES, :] = qkv_ref[c].T

    @pl.when(j == 2)
    def _():
        for c in range(n_chunks):
            qkv_ref[c] = chunk(y, c)

    @pl.when((j == 2) & in_window)
    def _():
        vt_ref[0] = y.T

    @pl.when(j == 3)
    def _():
        ykv = jnp.dot(h, wkv_ref[...], preferred_element_type=F32) + bkv_ref[...]
        kvb_ref[0] = _rope(chunk(ykv, 0), cos, slo, shi)
        kvb_ref[1] = chunk(ykv, 1)

    @pl.when((j == 3) & (tile_in_seq == tiles_per_seq - 1))
    def _():
        tm = kvb_ref.shape[1]
        kvbt_ref[0, :KV_WIDTH_B, :] = kvb_ref[0, tm - win_b:, :].T
        kvbt_ref[0, KV_WIDTH_B:, :] = kvb_ref[1, tm - win_b:, :].T


def _qkv_proj(x, g_attn, w_in_bf, b_in, rope_tabs, tm, n_seq, win_a, win_b):
    t = x.shape[0]
    seq = t // n_seq
    tiles_per_seq = seq // tm
    assert seq % tm == 0 and win_a % tm == 0 and win_a <= seq and win_b <= tm
    first_win_tile = tiles_per_seq - win_a // tm
    cos, slo, shi = rope_tabs
    tab_blocks = cos.shape[0] // tm
    tab_map = lambda i, j: (i % tab_blocks, 0)
    win_map = lambda i, j: (i // tiles_per_seq, 0, jnp.maximum(i % tiles_per_seq - first_win_tile, 0))
    return pl.pallas_call(
        functools.partial(_qkv_kernel, tiles_per_seq, first_win_tile, win_b),
        out_shape=(jax.ShapeDtypeStruct((QKV4_COLS // LANES, t, LANES), F32),
                   jax.ShapeDtypeStruct((2, t, LANES), F32),
                   jax.ShapeDtypeStruct((n_seq, WIDTH, win_a), F32),
                   jax.ShapeDtypeStruct((n_seq, WIDTH, win_a), F32),
                   jax.ShapeDtypeStruct((n_seq, 2 * KV_WIDTH_B, win_b), F32)),
        grid=(t // tm, 4),
        in_specs=[
            pl.BlockSpec((tm, D_MODEL), lambda i, j: (i, 0)),
            pl.BlockSpec((1, D_MODEL), lambda i, j: (0, 0)),
            pl.BlockSpec((D_MODEL, WIDTH), lambda i, j: (0, j)),
            pl.BlockSpec((D_MODEL, 2 * KV_WIDTH_B), lambda i, j: (0, QKV4_COLS // (2 * KV_WIDTH_B))),
            pl.BlockSpec((1, WIDTH), lambda i, j: (0, j)),
            pl.BlockSpec((1, 2 * KV_WIDTH_B), lambda i, j: (0, QKV4_COLS // (2 * KV_WIDTH_B))),
            pl.BlockSpec((tm, LANES), tab_map),
            pl.BlockSpec((tm, LANES), tab_map),
            pl.BlockSpec((tm, LANES), tab_map),
        ],
        out_specs=(pl.BlockSpec((WIDTH // LANES, tm, LANES), lambda i, j: (j, i, 0)),
                   pl.BlockSpec((2, tm, LANES), lambda i, j: (0, i, 0)),
                   pl.BlockSpec((1, WIDTH, tm), win_map),
                   pl.BlockSpec((1, WIDTH, tm), win_map),
                   pl.BlockSpec((1, 2 * KV_WIDTH_B, win_b), lambda i, j: (i // tiles_per_seq, 0, 0))),
        scratch_shapes=[pltpu.VMEM((tm, D_MODEL), BF16)],
        compiler_params=_cparams(("arbitrary", "arbitrary"), 48),
        name="qkv_proj",
    )(x, g_attn, w_in_bf, w_in_bf, b_in, b_in, cos, slo, shi)


def _rope_tables(pos):
    half = HEAD_DIM // 2
    inv = ROPE_THETA ** (-jnp.arange(half, dtype=F32) / half)
    ang = pos.astype(F32)[:, None] * inv[None, :]
    cos = jnp.tile(jnp.cos(ang), (1, LANES // half))
    sin = jnp.tile(jnp.sin(ang), (1, LANES // half))
    first = (jnp.arange(LANES) % HEAD_DIM) < half
    return cos, jnp.where(first, -sin, 0.0), jnp.where(first, 0.0, sin)


def _head_softmax_pv(qm, k2, v2, valid):
    s = lax.dot_general(qm, k2, (((1,), (1,)), ((), ())), preferred_element_type=F32)
    s = jnp.where(valid, s, MASK_VALUE)
    m = jnp.max(s, axis=1, keepdims=True)
    p = jnp.exp(s - m)
    l = jnp.sum(p, axis=1, keepdims=True)
    o = jnp.dot(p.astype(BF16), v2, preferred_element_type=F32) / l
    return o, m + jnp.log(l)


ATT_SB = BLOCK * max(d for _, d in DILATIONS)
ATT_UNITS = ATT_SB // BLOCK
ATT_UNROLL = 8


def _attn_prompt_kernel(qa_ref, ka_ref, kap_ref, va_ref, vap_ref, qb_ref, kb_ref, kbp_ref, vb_ref, vbp_ref,
                        sink_ref, oa_ref, ob_ref, kcat, vcat, kbcat, vbcat, o_scr, lse_scr):
    has_prev = pl.program_id(1) > 0
    pair = pl.program_id(2)
    sb = ATT_SB

    kcat[0:sb, :] = kap_ref[0]
    kcat[sb:2 * sb, :] = ka_ref[0]
    vcat[0:sb, :] = vap_ref[0]
    vcat[sb:2 * sb, :] = va_ref[0]
    kbcat[0:BLOCK, :] = kbp_ref[0]
    kbcat[BLOCK:BLOCK + sb, :] = kb_ref[0]
    vbcat[0:BLOCK, :] = vbp_ref[0]
    vbcat[BLOCK:BLOCK + sb, :] = vb_ref[0]

    lane = lax.broadcasted_iota(I32, (BLOCK, LANES), 1)
    first = lane < HEAD_DIM
    qi = lax.broadcasted_iota(I32, (BLOCK, 2 * BLOCK), 0)
    si = lax.broadcasted_iota(I32, (BLOCK, 2 * BLOCK), 1)
    dist = qi - si + BLOCK
    band = (dist >= 0) & (dist <= BLOCK)
    own_block = si >= BLOCK

    def rows(start, size, d):
        if d == 1:
            return pl.ds(pl.multiple_of(start, BLOCK), size)
        return pl.ds(start, size, stride=d)

    for p, (_, d) in enumerate(DILATIONS):
        def unit(u, carry, p=p, d=d):
            blk = u // d
            qs = blk * (BLOCK * d) + u % d
            valid = band & (own_block | (blk > 0) | has_prev)
            q = qa_ref[0, rows(qs, BLOCK, d), :] * SCALE
            k2 = kcat[rows(sb + qs - BLOCK * d, 2 * BLOCK, d), :].astype(BF16)
            v2 = vcat[rows(sb + qs - BLOCK * d, 2 * BLOCK, d), :].astype(BF16)
            o0, l0 = _head_softmax_pv(jnp.where(first, q, 0.0).astype(BF16), k2, v2, valid)
            o1, l1 = _head_softmax_pv(jnp.where(first, 0.0, q).astype(BF16), k2, v2, valid)
            o_scr[p, rows(qs, BLOCK, d), :] = jnp.where(first, o0, o1)
            lse_scr[p, rows(qs, BLOCK, d), :] = jnp.where(first, l0, l1)
            return carry
        lax.fori_loop(0, ATT_UNITS, unit, 0, unroll=ATT_UNROLL)

    def mix(bk, carry):
        r = pl.ds(pl.multiple_of(bk * BLOCK, BLOCK), BLOCK)
        lses = [lse_scr[p, r, :] for p in range(len(DILATIONS))]
        m = functools.reduce(jnp.maximum, lses)
        ws = [jnp.exp(l - m) for l in lses]
        oa_ref[0, r, :] = sum(w * o_scr[p, r, :] for p, w in enumerate(ws)) / sum(ws)
        return carry
    lax.fori_loop(0, ATT_UNITS, mix, 0)

    kv_first = pair < (N_HEADS // N_KV_B) // 2
    live = jnp.where(lane >= HEAD_DIM, 1, 0) == jnp.where(kv_first, 0, 1)
    sink = sink_ref[...]

    def unit_b(blk, carry):
        r = pl.ds(pl.multiple_of(blk * BLOCK, BLOCK), BLOCK)
        r2 = pl.ds(pl.multiple_of(blk * BLOCK, BLOCK), 2 * BLOCK)
        valid = band & (own_block | (blk > 0) | has_prev)
        qp = qb_ref[0, r, :] * SCALE
        qsw = pltpu.roll(qp, HEAD_DIM, 1)
        k2 = kbcat[r2, :].astype(BF16)
        v2 = vbcat[r2, :].astype(BF16)
        q_even = jnp.where(kv_first, qp, qsw)
        q_odd = jnp.where(kv_first, qsw, qp)
        oe, le = _head_softmax_pv(jnp.where(live, q_even, 0.0).astype(BF16), k2, v2, valid)
        oo, lo = _head_softmax_pv(jnp.where(live, q_odd, 0.0).astype(BF16), k2, v2, valid)
        oe_sw, oo_sw = pltpu.roll(oe, HEAD_DIM, 1), pltpu.roll(oo, HEAD_DIM, 1)
        o = jnp.where(first, jnp.where(kv_first, oe, oe_sw), jnp.where(kv_first, oo_sw, oo))
        lse = jnp.where(first, le, lo)
        ob_ref[0, r, :] = o * jax.nn.sigmoid(lse - sink)
        return carry
    lax.fori_loop(0, ATT_UNITS, unit_b, 0, unroll=ATT_UNROLL)


def _attn_prompt(qkvc, kvbc, sinks_lanes, batch, seq):
    t = batch * seq
    assert seq % ATT_SB == 0
    n_sb = seq // ATT_SB
    n_pairs = WIDTH // LANES
    cur = lambda comp: (lambda b, s, p: (comp * n_pairs + p, b * n_sb + s, 0))
    prev = lambda comp: (lambda b, s, p: (comp * n_pairs + p, b * n_sb + jnp.maximum(s - 1, 0), 0))
    blk = lambda f: pl.BlockSpec((1, ATT_SB, LANES), f)
    kvb_cur = lambda c: pl.BlockSpec((1, ATT_SB, LANES), lambda b, s, p: (c, b * n_sb + s, 0))
    kvb_prev = lambda c: pl.BlockSpec(
        (1, BLOCK, LANES), lambda b, s, p: (c, jnp.maximum((b * n_sb + s) * ATT_UNITS - 1, b * n_sb * ATT_UNITS), 0))
    out_spec = pl.BlockSpec((1, ATT_SB, LANES), lambda b, s, p: (p, b * n_sb + s, 0))
    return pl.pallas_call(
        _attn_prompt_kernel,
        out_shape=(jax.ShapeDtypeStruct((n_pairs, t, LANES), F32),
                   jax.ShapeDtypeStruct((n_pairs, t, LANES), F32)),
        grid=(batch, n_sb, n_pairs),
        in_specs=[blk(cur(0)), blk(cur(1)), blk(prev(1)), blk(cur(2)), blk(prev(2)), blk(cur(3)),
                  kvb_cur(0), kvb_prev(0), kvb_cur(1), kvb_prev(1),
                  pl.BlockSpec((1, LANES), lambda b, s, p: (0, p))],
        out_specs=(out_spec, out_spec),
        scratch_shapes=[
            pltpu.VMEM((2 * ATT_SB, LANES), F32), pltpu.VMEM((2 * ATT_SB, LANES), F32),
            pltpu.VMEM((BLOCK + ATT_SB, LANES), F32), pltpu.VMEM((BLOCK + ATT_SB, LANES), F32),
            pltpu.VMEM((len(DILATIONS), ATT_SB, LANES), F32), pltpu.VMEM((len(DILATIONS), ATT_SB, LANES), F32)],
        compiler_params=_cparams(("parallel", "parallel", "arbitrary"), 48),
        name="attn_prompt",
    )(qkvc, qkvc, qkvc, qkvc, qkvc, qkvc, kvbc, kvbc, kvbc, kvbc, sinks_lanes)


def _hi_dot(a, b, dims=(((1,), (0,)), ((), ()))):
    return lax.dot_general(a, b, dims, preferred_element_type=F32, precision=lax.Precision.HIGHEST)


def _attn_sample_kernel(cnt_ref, qa_ref, kn_ref, vn_ref, qb_ref, kbn_ref, vbn_ref, sink_ref,
                        kt_ref, vt_ref, kbt_ref, vbt_ref, oa_ref, ob_ref):
    hps = N_HEADS // N_KV_B
    nt = (((1,), (1,)), ((), ()))
    head = lax.broadcasted_iota(I32, (hps, 1), 0)
    cnt = cnt_ref[...]
    n_pat = float(len(DILATIONS))

    qa = qa_ref[0] * SCALE
    qa_bf = qa.astype(BF16)
    s = jnp.zeros((hps, cnt.shape[1]), F32)
    for hl in range(hps):
        s_hl = jnp.dot(qa_bf, kt_ref[0, hl].astype(BF16), preferred_element_type=F32)
        s = jnp.where(head == hl, s_hl, s)
    s = jnp.where(cnt > 0.0, s, MASK_VALUE)
    s_new = jnp.sum(qa * kn_ref[0], axis=1, keepdims=True)
    m = jnp.maximum(jnp.max(s, axis=1, keepdims=True), s_new)
    e = cnt * jnp.exp(s - m)
    e_new = n_pat * jnp.exp(s_new - m)
    den = jnp.sum(e, axis=1, keepdims=True) + e_new
    e_bf = e.astype(BF16)
    pv = jnp.zeros((hps, HEAD_DIM), F32)
    for hl in range(hps):
        o_hl = lax.dot_general(e_bf, vt_ref[0, hl].astype(BF16), nt, preferred_element_type=F32)
        pv = jnp.where(head == hl, o_hl, pv)
    oa_ref[0] = (pv + e_new * vn_ref[0]) / den

    qb = qb_ref[0] * SCALE
    sb = _hi_dot(qb, kbt_ref[0, 0])
    sn = jnp.sum(qb * kbn_ref[0, 0], axis=1, keepdims=True)
    mm = jnp.maximum(jnp.max(sb, axis=1, keepdims=True), sn)
    eb = jnp.exp(sb - mm)
    en = jnp.exp(sn - mm)
    l = jnp.sum(eb, axis=1, keepdims=True) + en
    ob = (_hi_dot(eb, vbt_ref[0, 0], nt) + en * vbn_ref[0, 0]) / l
    lse = mm + jnp.log(l)
    ob_ref[0] = ob * jax.nn.sigmoid(lse - sink_ref[0])


def _attn_sample(qkvc_s, kvbc_s, cache_a_k, cache_a_v, cache_b_k, cache_b_v, sinks):
    nb = qkvc_s.shape[1]
    wa = cache_a_k.shape[1]
    hps = N_HEADS // N_KV_B
    assert cache_b_k.shape[1] == WINDOW_B and all(wa >= w and wa % d == 0 for w, d in DILATIONS)
    dist = wa - np.arange(wa)
    cnt = sum(((dist % d == 0) & (dist <= w)).astype(np.float32) for w, d in DILATIONS).reshape(1, wa)
    heads_rows = lambda c: c.reshape(c.shape[0], nb, LANES // HEAD_DIM, HEAD_DIM).transpose(1, 0, 2, 3)
    q4 = heads_rows(qkvc_s).reshape(nb, 4 * N_HEADS, HEAD_DIM)
    kvn = heads_rows(kvbc_s).reshape(nb, 2 * N_KV_B, 1, HEAD_DIM)
    to_row_minor = lambda c: jnp.transpose(c, (0, 2, 3, 1))
    q_spec = lambda comp: pl.BlockSpec((1, hps, HEAD_DIM), lambda b, hh: (b, comp * N_KV_B + hh, 0))
    n_spec = lambda comp: pl.BlockSpec((1, 1, 1, HEAD_DIM), lambda b, hh: (b, comp * N_KV_B + hh, 0, 0))
    a_spec = pl.BlockSpec((1, hps, HEAD_DIM, wa), lambda b, hh: (b, hh, 0, 0))
    b_spec = pl.BlockSpec((1, 1, HEAD_DIM, WINDOW_B), lambda b, hh: (b, hh, 0, 0))
    o_spec = pl.BlockSpec((1, hps, HEAD_DIM), lambda b, hh: (b, hh, 0))
    oa, ob = pl.pallas_call(
        _attn_sample_kernel,
        out_shape=(jax.ShapeDtypeStruct((nb, N_HEADS, HEAD_DIM), F32),
                   jax.ShapeDtypeStruct((nb, N_HEADS, HEAD_DIM), F32)),
        grid=(nb, N_KV_B),
        in_specs=[
            pl.BlockSpec((1, wa), lambda b, hh: (0, 0)),
            q_spec(0), q_spec(1), q_spec(2), q_spec(3), n_spec(0), n_spec(1),
            pl.BlockSpec((1, hps, 1), lambda b, hh: (hh, 0, 0)),
            a_spec, a_spec, b_spec, b_spec,
        ],
        out_specs=(o_spec, o_spec),
        compiler_params=_cparams(("parallel", "parallel"), 40),
        name="attn_sample",
    )(jnp.asarray(cnt), q4, q4, q4, q4, kvn, kvn, sinks.reshape(N_KV_B, hps, 1),
      to_row_minor(cache_a_k), to_row_minor(cache_a_v), to_row_minor(cache_b_k), to_row_minor(cache_b_v))
    chunked = lambda o: o.reshape(nb, WIDTH // LANES, LANES).transpose(1, 0, 2)
    return chunked(oa), chunked(ob)


def _split_bf16(x):
    hi = x.astype(BF16)
    lo = (x - hi.astype(F32)).astype(BF16)
    return hi, lo


def _post_kernel(oa_ref, ob_ref, x_ref, ga_ref, gb_ref, w_ref, b_ref, gf_ref, wr_ref, br_ref,
                 x1_ref, h2_ref, ridx_ref, rgate_ref):
    unchunk = lambda ref: jnp.concatenate([ref[c] for c in range(ref.shape[0])], axis=1)
    na = _rmsnorm(unchunk(oa_ref), ga_ref[...]).astype(BF16)
    nb = _rmsnorm(unchunk(ob_ref), gb_ref[...]).astype(BF16)
    c = jnp.concatenate([na, nb], axis=1)
    x1 = x_ref[...] + jnp.dot(c, w_ref[...], preferred_element_type=F32) + b_ref[...]
    x1_ref[...] = x1
    h2 = _rmsnorm(x1, gf_ref[...])
    h2_ref[...] = h2

    h_hi, h_lo = _split_bf16(h2)
    w_hi, w_lo = _split_bf16(wr_ref[...])
    dot = lambda a, b: jnp.dot(a, b, preferred_element_type=F32)
    logits = dot(h_hi, w_hi) + (dot(h_hi, w_lo) + dot(h_lo, w_hi)) + br_ref[...]

    tm = logits.shape[0]
    eidx = lax.broadcasted_iota(I32, (tm, N_EXPERTS), 1)
    lane = lax.broadcasted_iota(I32, (tm, LANES), 1)
    work = logits
    vals, idxs = [], []
    for _ in range(TOP_K):
        v = jnp.max(work, axis=1, keepdims=True)
        i = jnp.min(jnp.where(work == v, eidx, N_EXPERTS), axis=1, keepdims=True)
        vals.append(v)
        idxs.append(i)
        work = jnp.where(eidx == i, -jnp.inf, work)
    es = [jnp.exp(v - vals[0]) for v in vals]
    den = sum(es)
    ridx = jnp.zeros((tm, LANES), I32)
    rgate = jnp.zeros((tm, LANES), F32)
    for k in range(TOP_K):
        ridx = jnp.where(lane == k, idxs[k], ridx)
        rgate = jnp.where(lane == k, es[k] / den, rgate)
    ridx_ref[...] = ridx
    rgate_ref[...] = rgate


def _post_attention(oa, ob, x, g_out_a, g_out_b, w_out_bf, b_out, g_ffn, w_router, b_router, tm):
    t = x.shape[0]
    n_pairs = WIDTH // LANES
    rowblk = lambda w: pl.BlockSpec((tm, w), lambda i: (i, 0))
    chunked = pl.BlockSpec((n_pairs, tm, LANES), lambda i: (0, i, 0))
    full = lambda r, c: pl.BlockSpec((r, c), lambda i: (0, 0))
    return pl.pallas_call(
        _post_kernel,
        out_shape=(jax.ShapeDtypeStruct((t, D_MODEL), F32), jax.ShapeDtypeStruct((t, D_MODEL), F32),
                   jax.ShapeDtypeStruct((t, LANES), I32), jax.ShapeDtypeStruct((t, LANES), F32)),
        grid=(t // tm,),
        in_specs=[chunked, chunked, rowblk(D_MODEL), full(1, WIDTH), full(1, WIDTH),
                  full(D_MODEL, D_MODEL), full(1, D_MODEL), full(1, D_MODEL),
                  full(D_MODEL, N_EXPERTS), full(1, N_EXPERTS)],
        out_specs=(rowblk(D_MODEL), rowblk(D_MODEL), rowblk(LANES), rowblk(LANES)),
        compiler_params=_cparams(("parallel",), 48),
        name="post_attention",
    )(oa, ob, x, g_out_a, g_out_b, w_out_bf, b_out, g_ffn, w_router, b_router)


def _dispatch_kernel(tok_ref, h_hbm, o_ref, buf, sem):
    i = pl.program_id(0)

    def issue(step, slot):
        base = step * DISPATCH_ROWS

        def body(r, carry):
            pltpu.make_async_copy(h_hbm.at[pl.ds(tok_ref[base + r], 1)],
                                  buf.at[slot, pl.ds(r, 1)], sem.at[slot]).start()
            return carry
        lax.fori_loop(0, DISPATCH_ROWS, body, 0, unroll=8)

    @pl.when(i == 0)
    def _():
        issue(0, 0)

    @pl.when(i + 1 < pl.num_programs(0))
    def _():
        issue(i + 1, (i + 1) % 2)

    slot = i % 2
    pltpu.make_async_copy(h_hbm.at[pl.ds(0, DISPATCH_ROWS)], buf.at[slot], sem.at[slot]).wait()
    o_ref[...] = buf[slot].astype(BF16)


def _dispatch(slot_tok, h2):
    p = slot_tok.shape[0]
    return pl.pallas_call(
        _dispatch_kernel,
        out_shape=jax.ShapeDtypeStruct((p, D_MODEL), BF16),
        grid_spec=pltpu.PrefetchScalarGridSpec(
            num_scalar_prefetch=1,
            grid=(p // DISPATCH_ROWS,),
            in_specs=[pl.BlockSpec(memory_space=pl.ANY)],
            out_specs=pl.BlockSpec((DISPATCH_ROWS, D_MODEL), lambda i, tok: (i, 0)),
            scratch_shapes=[pltpu.VMEM((2, DISPATCH_ROWS, D_MODEL), F32), pltpu.SemaphoreType.DMA((2,))],
        ),
        compiler_params=_cparams(("arbitrary",), 32),
        name="moe_dispatch",
    )(slot_tok, h2)


def _moe_kernel(ie_ref, iblk_ref, insub_ref, iact_ref, used_ref,
                x_hbm, wg_ref, wu_ref, wd_ref, bg_ref, bu_ref, bd_ref, y_hbm,
                xbuf, act, wa_bf, wb_bf, ystage, xsem, ysem):
    it = pl.program_id(0)
    s = pl.program_id(1)
    nsub = insub_ref[it]
    row0 = iblk_ref[it] * MOE_SUB

    def x_copy(j):
        return pltpu.make_async_copy(x_hbm.at[pl.ds(row0 + j * MOE_SUB, MOE_SUB)],
                                     xbuf.at[pl.ds(j * MOE_SUB, MOE_SUB)], xsem)

    def for_row_chunks(fn):
        big = nsub // 4

        def body(j, c):
            fn(pl.multiple_of(j * (4 * MOE_SUB), 4 * MOE_SUB), 4 * MOE_SUB)
            return c
        lax.fori_loop(0, big, body, 0)
        rem = nsub - 4 * big
        has2 = rem >= 2

        @pl.when(has2)
        def _():
            fn(pl.multiple_of(big * (4 * MOE_SUB), MOE_SUB), 2 * MOE_SUB)

        @pl.when((rem & 1) == 1)
        def _():
            fn(pl.multiple_of((4 * big + jnp.where(has2, 2, 0)) * MOE_SUB, MOE_SUB), MOE_SUB)

    @pl.when((s == 0) & (nsub > 0))
    def _():
        def start(j, c):
            x_copy(j).start()
            return c
        lax.fori_loop(0, nsub, start, 0)

        def wait(j, c):
            x_copy(j).wait()
            return c
        lax.fori_loop(0, nsub, wait, 0)

    @pl.when((s < MOE_F) & (nsub > 0))
    def _():
        wa_bf[...] = wg_ref[...].astype(BF16)
        wb_bf[...] = wu_ref[...].astype(BF16)
        bg, bu = bg_ref[...], bu_ref[...]

        def gate_up(r0, rows):
            x = xbuf[pl.ds(r0, rows), :]
            g = jnp.dot(x, wa_bf[...], preferred_element_type=F32) + bg
            u = jnp.dot(x, wb_bf[...], preferred_element_type=F32) + bu
            g = jnp.minimum(g, SWIGLU_LIMIT)
            u = jnp.clip(u, -SWIGLU_LIMIT, SWIGLU_LIMIT)
            a = g * jax.nn.sigmoid(SWIGLU_ALPHA * g) * (u + 1.0)
            act[s, pl.ds(r0, rows), :] = a.astype(BF16)
        for_row_chunks(gate_up)

    @pl.when((s >= MOE_F) & (nsub > 0))
    def _():
        n = s - MOE_F
        slot = n % 2
        wa_bf[...] = wd_ref[...].astype(BF16)
        bd = bd_ref[...]

        def y_copy(step, r0, rows):
            c0 = pl.multiple_of(step * MOE_TF, MOE_TF)
            return pltpu.make_async_copy(
                ystage.at[step % 2, pl.ds(r0, rows), :],
                y_hbm.at[pl.ds(row0 + r0, rows), pl.ds(c0, MOE_TF)], ysem.at[step % 2])

        def down(r0, rows):
            a = jnp.concatenate([act[f, pl.ds(r0, rows), :] for f in range(MOE_F)], axis=1)
            ystage[slot, pl.ds(r0, rows), :] = jnp.dot(a, wa_bf[...], preferred_element_type=F32) + bd
            y_copy(n, r0, rows).start()
        for_row_chunks(down)

        @pl.when(n > 0)
        def _():
            for_row_chunks(lambda r0, rows: y_copy(n - 1, r0, rows).wait())

        @pl.when(n == MOE_F - 1)
        def _():
            for_row_chunks(lambda r0, rows: y_copy(n, r0, rows).wait())

    @pl.when((it == pl.num_programs(0) - 1) & (s == pl.num_programs(1) - 1))
    def _():
        n_blocks = y_hbm.shape[0] // MOE_SUB
        ystage[0, pl.ds(0, MOE_SUB), :] = jnp.zeros((MOE_SUB, MOE_TF), F32)

        def z_copy(b, n):
            return pltpu.make_async_copy(
                ystage.at[0, pl.ds(0, MOE_SUB), :],
                y_hbm.at[pl.ds(b * MOE_SUB, MOE_SUB), pl.ds(n * MOE_TF, MOE_TF)], ysem.at[0])

        def fill(b, c):
            for n in range(MOE_F):
                z_copy(b, n).start()
            for n in range(MOE_F):
                z_copy(b, n).wait()
            return c
        lax.fori_loop(used_ref[0], n_blocks, fill, 0)


def _moe_experts(items, x_sorted, w_gate_up, b_gate_up, w_down, b_down):
    item_e, item_blk, item_nsub, item_act, used_blocks = items
    n_items = item_e.shape[0]
    p = x_sorted.shape[0]
    rmax = MOE_MAX_SUB * MOE_SUB
    last = MOE_F - 1
    f_of = lambda s, act: jnp.where(act == 1, jnp.minimum(s, last), last)
    n_of = lambda s, act: jnp.where(act == 1, jnp.maximum(s - MOE_F, 0), last)
    wg_map = lambda i, s, ie, ib, ins, ia, iu: (ie[i], 0, f_of(s, ia[i]))
    wu_map = lambda i, s, ie, ib, ins, ia, iu: (ie[i], 0, MOE_F + f_of(s, ia[i]))
    wd_map = lambda i, s, ie, ib, ins, ia, iu: (ie[i], 0, n_of(s, ia[i]))
    bgu = b_gate_up.reshape(N_EXPERTS, 1, 2 * D_FF)
    bd = b_down.reshape(N_EXPERTS, 1, D_MODEL)
    return pl.pallas_call(
        _moe_kernel,
        out_shape=jax.ShapeDtypeStruct((p, D_MODEL), F32),
        grid_spec=pltpu.PrefetchScalarGridSpec(
            num_scalar_prefetch=5,
            grid=(n_items, 2 * MOE_F),
            in_specs=[
                pl.BlockSpec(memory_space=pl.ANY),
                pl.BlockSpec((None, D_MODEL, MOE_TF), wg_map),
                pl.BlockSpec((None, D_MODEL, MOE_TF), wu_map),
                pl.BlockSpec((None, D_FF, MOE_TF), wd_map),
                pl.BlockSpec((None, 1, MOE_TF), wg_map),
                pl.BlockSpec((None, 1, MOE_TF), wu_map),
                pl.BlockSpec((None, 1, MOE_TF), wd_map),
            ],
            out_specs=pl.BlockSpec(memory_space=pl.ANY),
            scratch_shapes=[
                pltpu.VMEM((rmax, D_MODEL), BF16),
                pltpu.VMEM((MOE_F, rmax, MOE_TF), BF16),
                pltpu.VMEM((D_MODEL, MOE_TF), BF16),
                pltpu.VMEM((D_MODEL, MOE_TF), BF16),
                pltpu.VMEM((2, rmax, MOE_TF), F32),
                pltpu.SemaphoreType.DMA,
                pltpu.SemaphoreType.DMA((2,)),
            ],
        ),
        compiler_params=_cparams(("arbitrary", "arbitrary"), 56),
        name="moe_experts",
    )(item_e, item_blk, item_nsub, item_act, used_blocks, x_sorted, w_gate_up, w_gate_up, w_down, bgu, bgu, bd)


def _tail_kernel(dest_ref, y_hbm, gate_ref, x1_ref, p_ref, wg_ref, wp_ref, gp_ref, gf_ref, o_ref, ybuf, sem):
    i = pl.program_id(0)

    def issue(step, slot):
        base = step * (TAIL_ROWS * TOP_K)

        def body(r, carry):
            for k in range(TOP_K):
                pltpu.make_async_copy(y_hbm.at[pl.ds(dest_ref[base + r * TOP_K + k], 1)],
                                      ybuf.at[slot, k, pl.ds(r, 1)], sem.at[slot]).start()
            return carry
        lax.fori_loop(0, TAIL_ROWS, body, 0, unroll=2)

    @pl.when(i == 0)
    def _():
        issue(0, 0)

    @pl.when(i + 1 < pl.num_programs(0))
    def _():
        issue(i + 1, (i + 1) % 2)

    slot = i % 2
    for k in range(TOP_K):
        pltpu.make_async_copy(y_hbm.at[pl.ds(0, TAIL_ROWS)], ybuf.at[slot, k], sem.at[slot]).wait()

    gate = gate_ref[...]
    moe = ybuf[slot, 0] * gate[:, 0:1]
    for k in range(1, TOP_K):
        moe = moe + ybuf[slot, k] * gate[:, k:k + 1]
    x2 = x1_ref[...] + moe
    h3 = _rmsnorm(x2, gp_ref[...]).astype(BF16)
    ple_gate = jax.nn.sigmoid(jnp.dot(h3, wg_ref[...], preferred_element_type=F32))
    ple = jnp.dot(p_ref[...].astype(BF16), wp_ref[...], preferred_element_type=F32)
    x3 = x2 + ple_gate * ple
    o_ref[...] = _rmsnorm(x3, gf_ref[...])


def _tail(dest, y_sorted, rgate, x1, p_all, w_ple_gate_bf, w_ple_proj_bf, g_ple, g_final):
    t = x1.shape[0]
    rowblk = lambda w: pl.BlockSpec((TAIL_ROWS, w), lambda i, d: (i, 0))
    full = lambda r, c: pl.BlockSpec((r, c), lambda i, d: (0, 0))
    return pl.pallas_call(
        _tail_kernel,
        out_shape=jax.ShapeDtypeStruct((t, D_MODEL), F32),
        grid_spec=pltpu.PrefetchScalarGridSpec(
            num_scalar_prefetch=1,
            grid=(t // TAIL_ROWS,),
            in_specs=[
                pl.BlockSpec(memory_space=pl.ANY),
                rowblk(LANES), rowblk(D_MODEL), rowblk(PLE_DIM),
                full(D_MODEL, D_MODEL), full(PLE_DIM, D_MODEL), full(1, D_MODEL), full(1, D_MODEL)],
            out_specs=rowblk(D_MODEL),
            scratch_shapes=[pltpu.VMEM((2, TOP_K, TAIL_ROWS, D_MODEL), F32), pltpu.SemaphoreType.DMA((2,))],
        ),
        compiler_params=_cparams(("arbitrary",), 48),
        name="moe_tail",
    )(dest, y_sorted, rgate, x1, p_all, w_ple_gate_bf, w_ple_proj_bf, g_ple, g_final)


def _routing(ridx, rgate_unused, t):
    del rgate_unused
    m = t * TOP_K
    flat_e = ridx[:, :TOP_K].reshape(m)
    onehot = (flat_e[:, None] == jnp.arange(N_EXPERTS, dtype=I32)[None, :]).astype(I32)
    csum = jnp.cumsum(onehot, axis=0)
    rank = jnp.sum((csum - onehot) * onehot, axis=1)
    counts = csum[-1]
    nblk = (counts + MOE_SUB - 1) // MOE_SUB
    blk_end = jnp.cumsum(nblk)
    blk_start = blk_end - nblk
    dest = blk_start[flat_e] * MOE_SUB + rank
    n_blocks = -(-m // MOE_SUB) + N_EXPERTS
    n_blocks = -(-n_blocks // (DISPATCH_ROWS // MOE_SUB)) * (DISPATCH_ROWS // MOE_SUB)
    p = n_blocks * MOE_SUB
    flat_tok = jnp.arange(m, dtype=I32) // TOP_K
    slot_tok = jnp.zeros((p,), I32).at[dest].set(flat_tok, unique_indices=True)

    n_items = N_EXPERTS + -(-n_blocks // MOE_MAX_SUB)
    items_per_e = (nblk + MOE_MAX_SUB - 1) // MOE_MAX_SUB
    item_end = jnp.cumsum(items_per_e)
    total = item_end[-1]
    ids = jnp.arange(n_items, dtype=I32)
    active = ids < total
    ids_c = jnp.minimum(ids, total - 1)
    item_e = jnp.searchsorted(item_end, ids_c, side="right").astype(I32)
    local = ids_c - (item_end - items_per_e)[item_e]
    item_blk = blk_start[item_e] + local * MOE_MAX_SUB
    item_nsub = jnp.where(active, jnp.minimum(MOE_MAX_SUB, nblk[item_e] - local * MOE_MAX_SUB), 0)
    items = (item_e, item_blk.astype(I32), item_nsub.astype(I32), active.astype(I32),
             blk_end[-1:].astype(I32))
    return slot_tok, dest.astype(I32), items


def kernel(x_prompt, x_sample, cache_a_k, cache_a_v, cache_b_k, cache_b_v, p_prompt, p_sample, g_attn, w_in, b_in, sinks, g_out_a, g_out_b, w_out, b_out, g_ffn, w_router, b_router, w_gate_up, b_gate_up, w_down, b_down, g_ple, w_ple_gate, w_ple_proj, g_final):
    batch, seq, _ = x_prompt.shape
    nb, nseq, _ = x_sample.shape
    assert nseq == 1 and g_attn.shape[0] == 1
    tp, ts = batch * seq, nb * nseq
    row = lambda v: v.reshape(1, -1)

    w_in_bf = _cast_bf16(w_in[0])
    w_out_bf = _cast_bf16(w_out[0])
    w_pg_bf = _cast_bf16(w_ple_gate[0])
    w_pp_bf = _cast_bf16(w_ple_proj[0])

    tabs_p = _rope_tables(jnp.arange(seq, dtype=I32))
    tabs_s = _rope_tables(jnp.full((ts,), PAST_LEN, I32))
    sinks_lanes = jnp.repeat(sinks[0], HEAD_DIM).reshape(1, WIDTH)

    tm_p = 512
    xp = x_prompt.reshape(tp, D_MODEL)
    xs = x_sample.reshape(ts, D_MODEL)
    win_a, win_b = min(DILATIONS[-1][0], seq), min(WINDOW_B, seq)
    qkvc_p, kvbc_p, kt_p, vt_p, kvbt_p = _qkv_proj(xp, row(g_attn), w_in_bf, row(b_in), tabs_p, tm_p,
                                                    batch, win_a, win_b)
    qkvc_s, kvbc_s, kt_s, vt_s, kvbt_s = _qkv_proj(xs, row(g_attn), w_in_bf, row(b_in), tabs_s, ts, 1, ts, ts)

    oa_p, ob_p = _attn_prompt(qkvc_p, kvbc_p, sinks_lanes, batch, seq)
    oa_s, ob_s = _attn_sample(qkvc_s, kvbc_s, cache_a_k[0], cache_a_v[0], cache_b_k[0], cache_b_v[0], sinks[0])

    post = functools.partial(_post_attention, g_out_a=row(g_out_a), g_out_b=row(g_out_b), w_out_bf=w_out_bf,
                             b_out=row(b_out), g_ffn=row(g_ffn), w_router=w_router[0], b_router=row(b_router))
    x1_p, h2_p, ridx_p, rgate_p = post(oa_p, ob_p, xp, tm=256)
    x1_s, h2_s, ridx_s, rgate_s = post(oa_s, ob_s, xs, tm=ts)

    cat = lambda a, b: jnp.concatenate([a, b], axis=0)
    x1, h2, ridx, rgate = cat(x1_p, x1_s), cat(h2_p, h2_s), cat(ridx_p, ridx_s), cat(rgate_p, rgate_s)
    t = tp + ts
    slot_tok, dest, items = _routing(ridx, rgate, t)
    x_sorted = _dispatch(slot_tok, h2)
    y_sorted = _moe_experts(items, x_sorted, w_gate_up[0], b_gate_up[0], w_down[0], b_down[0])
    p_all = cat(p_prompt[0].reshape(tp, PLE_DIM), p_sample[0].reshape(ts, PLE_DIM))
    y = _tail(dest, y_sorted, rgate, x1, p_all, w_pg_bf, w_pp_bf, row(g_ple), row(g_final))

    y_prompt = y[:tp].reshape(batch, seq, D_MODEL)
    y_sample = y[tp:].reshape(nb, nseq, D_MODEL)

    def rows_out(t3, heads):
        n, _, rows = t3.shape
        return t3.reshape(n, heads, HEAD_DIM, rows).transpose(0, 3, 1, 2)[None]

    def sample_out(t3, heads):
        return t3.reshape(heads, HEAD_DIM, nb).transpose(2, 0, 1)[None, :, None]

    return (y_prompt, y_sample,
            rows_out(kt_p, N_HEADS), rows_out(vt_p, N_HEADS),
            rows_out(kvbt_p[:, :KV_WIDTH_B], N_KV_B), rows_out(kvbt_p[:, KV_WIDTH_B:], N_KV_B),
            sample_out(kt_s, N_HEADS), sample_out(vt_s, N_HEADS),
            sample_out(kvbt_s[:, :KV_WIDTH_B], N_KV_B), sample_out(kvbt_s[:, KV_WIDTH_B:], N_KV_B))
```

```python
import functools

import jax
import jax.numpy as jnp
import numpy as np
from jax import lax
from jax.experimental import pallas as pl
from jax.experimental.pallas import tpu as pltpu

F32 = jnp.float32
BF16 = jnp.bfloat16
I32 = jnp.int32

D_MODEL = 2048
HEAD_DIM = 64
N_HEADS = 16
WIDTH = N_HEADS * HEAD_DIM
N_KV_B = 2
KV_WIDTH_B = N_KV_B * HEAD_DIM
DILATIONS = ((128, 1), (512, 4), (2048, 16))
WINDOW_B = 128
BLOCK = 128
N_EXPERTS = 32
TOP_K = 4
D_FF = 2048
SWIGLU_ALPHA = 1.702
SWIGLU_LIMIT = 7.0
PLE_DIM = 256
ROPE_THETA = 10000.0
NORM_EPS = 1e-5
MASK_VALUE = -1e30
SCALE = HEAD_DIM ** -0.5
PAST_LEN = 16384
QKV4_COLS = 4 * WIDTH
IN_COLS = QKV4_COLS + 2 * KV_WIDTH_B

LANES = 128
VMEM_LIMIT_CAP = 60 * 1024 * 1024

MOE_TF = 512
MOE_F = D_FF // MOE_TF
MOE_SUB = 128
MOE_MAX_SUB = 12
DISPATCH_ROWS = 256
TAIL_ROWS = 128


def _cparams(semantics, vmem_mb):
    return pltpu.CompilerParams(
        dimension_semantics=semantics,
        vmem_limit_bytes=min(vmem_mb * 1024 * 1024, VMEM_LIMIT_CAP))


def _rmsnorm(x, g):
    ms = jnp.mean(x * x, axis=-1, keepdims=True)
    return x * lax.rsqrt(ms + NORM_EPS) * g


def _cast_kernel(x_ref, o_ref):
    o_ref[...] = x_ref[...].astype(o_ref.dtype)


def _cast_bf16(w, rows_per_step=256):
    r, c = w.shape
    return pl.pallas_call(
        _cast_kernel,
        out_shape=jax.ShapeDtypeStruct((r, c), BF16),
        grid=(r // rows_per_step,),
        in_specs=[pl.BlockSpec((rows_per_step, c), lambda i: (i, 0))],
        out_specs=pl.BlockSpec((rows_per_step, c), lambda i: (i, 0)),
        compiler_params=_cparams(("parallel",), 32),
        name="cast_bf16",
    )(w)


def _rope(yc, cos, sin_lo, sin_hi):
    return yc * cos + pltpu.roll(yc, LANES - 32, 1) * sin_lo + pltpu.roll(yc, 32, 1) * sin_hi


def _qkv_kernel(tiles_per_seq, first_win_tile, win_b,
                x_ref, g_ref, w_ref, wkv_ref, b_ref, bkv_ref, cos_ref, slo_ref, shi_ref,
                qkv_ref, kvb_ref, kt_ref, vt_ref, kvbt_ref, h_scr):
    i = pl.program_id(0)
    j = pl.program_id(1)
    tile_in_seq = i % tiles_per_seq
    in_window = tile_in_seq >= first_win_tile
    n_chunks = WIDTH // LANES
    chunk = lambda v, c: v[:, c * LANES:(c + 1) * LANES]

    @pl.when(j == 0)
    def _():
        h_scr[...] = _rmsnorm(x_ref[...], g_ref[...]).astype(BF16)

    h = h_scr[...]
    y = jnp.dot(h, w_ref[...], preferred_element_type=F32) + b_ref[...]
    cos, slo, shi = cos_ref[...], slo_ref[...], shi_ref[...]

    @pl.when(j != 2)
    def _():
        for c in range(n_chunks):
            qkv_ref[c] = _rope(chunk(y, c), cos, slo, shi)

    @pl.when((j == 1) & in_window)
    def _():
        for c in range(n_chunks):
            kt_ref[0, c * LANES:(c + 1) * LANES, :] = qkv_ref[c].T

    @pl.when(j == 2)
    def _():
        for c in range(n_chunks):
            qkv_ref[c] = chunk(y, c)

    @pl.when((j == 2) & in_window)
    def _():
        vt_ref[0] = y.T

    @pl.when(j == 3)
    def _():
        ykv = jnp.dot(h, wkv_ref[...], preferred_element_type=F32) + bkv_ref[...]
        kvb_ref[0] = _rope(chunk(ykv, 0), cos, slo, shi)
        kvb_ref[1] = chunk(ykv, 1)

    @pl.when((j == 3) & (tile_in_seq == tiles_per_seq - 1))
    def _():
        tm = kvb_ref.shape[1]
        kvbt_ref[0, :KV_WIDTH_B, :] = kvb_ref[0, tm - win_b:, :].T
        kvbt_ref[0, KV_WIDTH_B:, :] = kvb_ref[1, tm - win_b:, :].T


def _qkv_proj(x, g_attn, w_in_bf, b_in, rope_tabs, tm, n_seq, win_a, win_b):
    t = x.shape[0]
    seq = t // n_seq
    tiles_per_seq = seq // tm
    assert seq % tm == 0 and win_a % tm == 0 and win_a <= seq and win_b <= tm
    first_win_tile = tiles_per_seq - win_a // tm
    cos, slo, shi = rope_tabs
    tab_blocks = cos.shape[0] // tm
    tab_map = lambda i, j: (i % tab_blocks, 0)
    win_map = lambda i, j: (i // tiles_per_seq, 0, jnp.maximum(i % tiles_per_seq - first_win_tile, 0))
    return pl.pallas_call(
        functools.partial(_qkv_kernel, tiles_per_seq, first_win_tile, win_b),
        out_shape=(jax.ShapeDtypeStruct((QKV4_COLS // LANES, t, LANES), F32),
                   jax.ShapeDtypeStruct((2, t, LANES), F32),
                   jax.ShapeDtypeStruct((n_seq, WIDTH, win_a), F32),
                   jax.ShapeDtypeStruct((n_seq, WIDTH, win_a), F32),
                   jax.ShapeDtypeStruct((n_seq, 2 * KV_WIDTH_B, win_b), F32)),
        grid=(t // tm, 4),
        in_specs=[
            pl.BlockSpec((tm, D_MODEL), lambda i, j: (i, 0)),
            pl.BlockSpec((1, D_MODEL), lambda i, j: (0, 0)),
            pl.BlockSpec((D_MODEL, WIDTH), lambda i, j: (0, j)),
            pl.BlockSpec((D_MODEL, 2 * KV_WIDTH_B), lambda i, j: (0, QKV4_COLS // (2 * KV_WIDTH_B))),
            pl.BlockSpec((1, WIDTH), lambda i, j: (0, j)),
            pl.BlockSpec((1, 2 * KV_WIDTH_B), lambda i, j: (0, QKV4_COLS // (2 * KV_WIDTH_B))),
            pl.BlockSpec((tm, LANES), tab_map),
            pl.BlockSpec((tm, LANES), tab_map),
            pl.BlockSpec((tm, LANES), tab_map),
        ],
        out_specs=(pl.BlockSpec((WIDTH // LANES, tm, LANES), lambda i, j: (j, i, 0)),
                   pl.BlockSpec((2, tm, LANES), lambda i, j: (0, i, 0)),
                   pl.BlockSpec((1, WIDTH, tm), win_map),
                   pl.BlockSpec((1, WIDTH, tm), win_map),
                   pl.BlockSpec((1, 2 * KV_WIDTH_B, win_b), lambda i, j: (i // tiles_per_seq, 0, 0))),
        scratch_shapes=[pltpu.VMEM((tm, D_MODEL), BF16)],
        compiler_params=_cparams(("arbitrary", "arbitrary"), 48),
        name="qkv_proj",
    )(x, g_attn, w_in_bf, w_in_bf, b_in, b_in, cos, slo, shi)


def _rope_tables(pos):
    half = HEAD_DIM // 2
    inv = ROPE_THETA ** (-jnp.arange(half, dtype=F32) / half)
    ang = pos.astype(F32)[:, None] * inv[None, :]
    cos = jnp.tile(jnp.cos(ang), (1, LANES // half))
    sin = jnp.tile(jnp.sin(ang), (1, LANES // half))
    first = (jnp.arange(LANES) % HEAD_DIM) < half
    return cos, jnp.where(first, -sin, 0.0), jnp.where(first, 0.0, sin)


def _two_head_softmax_pv(q0, q1, k2, v2, valid2):
    qm = jnp.concatenate([q0, q1], axis=0).astype(BF16)
    s = lax.dot_general(qm, k2, (((1,), (1,)), ((), ())), preferred_element_type=F32)
    s = jnp.where(valid2, s, MASK_VALUE)
    m = jnp.max(s, axis=1, keepdims=True)
    p = jnp.exp(s - m)
    l = jnp.sum(p, axis=1, keepdims=True)
    o = jnp.dot(p.astype(BF16), v2, preferred_element_type=F32) / l
    lse = m + jnp.log(l)
    return o[:BLOCK], lse[:BLOCK], o[BLOCK:], lse[BLOCK:]


ATT_SB = BLOCK * max(d for _, d in DILATIONS)
ATT_UNITS = ATT_SB // BLOCK
ATT_UNROLL = 8


def _attn_prompt_kernel(qa_ref, ka_ref, kap_ref, va_ref, vap_ref, qb_ref, kb_ref, kbp_ref, vb_ref, vbp_ref,
                        sink_ref, oa_ref, ob_ref, kcat, vcat, kbcat, vbcat, o_scr, lse_scr):
    has_prev = pl.program_id(1) > 0
    pair = pl.program_id(2)
    sb = ATT_SB

    kcat[0:sb, :] = kap_ref[0]
    kcat[sb:2 * sb, :] = ka_ref[0]
    vcat[0:sb, :] = vap_ref[0]
    vcat[sb:2 * sb, :] = va_ref[0]
    kbcat[0:BLOCK, :] = kbp_ref[0]
    kbcat[BLOCK:BLOCK + sb, :] = kb_ref[0]
    vbcat[0:BLOCK, :] = vbp_ref[0]
    vbcat[BLOCK:BLOCK + sb, :] = vb_ref[0]

    lane = lax.broadcasted_iota(I32, (BLOCK, LANES), 1)
    first = lane < HEAD_DIM
    qi = lax.broadcasted_iota(I32, (2 * BLOCK, 2 * BLOCK), 0) & (BLOCK - 1)
    si = lax.broadcasted_iota(I32, (2 * BLOCK, 2 * BLOCK), 1)
    dist = qi - si + BLOCK
    band = (dist >= 0) & (dist <= BLOCK)
    own_block = si >= BLOCK

    def rows(start, size, d):
        if d == 1:
            return pl.ds(pl.multiple_of(start, BLOCK), size)
        return pl.ds(start, size, stride=d)

    for p, (_, d) in enumerate(DILATIONS):
        def unit(u, carry, p=p, d=d):
            blk = u // d
            qs = blk * (BLOCK * d) + u % d
            valid = band & (own_block | (blk > 0) | has_prev)
            q = qa_ref[0, rows(qs, BLOCK, d), :] * SCALE
            k2 = kcat[rows(sb + qs - BLOCK * d, 2 * BLOCK, d), :].astype(BF16)
            v2 = vcat[rows(sb + qs - BLOCK * d, 2 * BLOCK, d), :].astype(BF16)
            o0, l0, o1, l1 = _two_head_softmax_pv(jnp.where(first, q, 0.0), jnp.where(first, 0.0, q), k2, v2, valid)
            o_scr[p, rows(qs, BLOCK, d), :] = jnp.where(first, o0, o1)
            lse_scr[p, rows(qs, BLOCK, d), :] = jnp.where(first, l0, l1)
            return carry
        lax.fori_loop(0, ATT_UNITS, unit, 0, unroll=ATT_UNROLL)

    def mix(bk, carry):
        r = pl.ds(pl.multiple_of(bk * BLOCK, BLOCK), BLOCK)
        lses = [lse_scr[p, r, :] for p in range(len(DILATIONS))]
        m = functools.reduce(jnp.maximum, lses)
        ws = [jnp.exp(l - m) for l in lses]
        oa_ref[0, r, :] = sum(w * o_scr[p, r, :] for p, w in enumerate(ws)) / sum(ws)
        return carry
    lax.fori_loop(0, ATT_UNITS, mix, 0)

    kv_first = pair < (N_HEADS // N_KV_B) // 2
    live = jnp.where(lane >= HEAD_DIM, 1, 0) == jnp.where(kv_first, 0, 1)
    sink = sink_ref[...]

    def unit_b(blk, carry):
        r = pl.ds(pl.multiple_of(blk * BLOCK, BLOCK), BLOCK)
        r2 = pl.ds(pl.multiple_of(blk * BLOCK, BLOCK), 2 * BLOCK)
        valid = band & (own_block | (blk > 0) | has_prev)
        qp = qb_ref[0, r, :] * SCALE
        qsw = pltpu.roll(qp, HEAD_DIM, 1)
        k2 = kbcat[r2, :].astype(BF16)
        v2 = vbcat[r2, :].astype(BF16)
        q_even = jnp.where(kv_first, qp, qsw)
        q_odd = jnp.where(kv_first, qsw, qp)
        oe, le, oo, lo = _two_head_softmax_pv(jnp.where(live, q_even, 0.0), jnp.where(live, q_odd, 0.0),
                                              k2, v2, valid)
        oe_sw, oo_sw = pltpu.roll(oe, HEAD_DIM, 1), pltpu.roll(oo, HEAD_DIM, 1)
        o = jnp.where(first, jnp.where(kv_first, oe, oe_sw), jnp.where(kv_first, oo_sw, oo))
        lse = jnp.where(first, le, lo)
        ob_ref[0, r, :] = o * jax.nn.sigmoid(lse - sink)
        return carry
    lax.fori_loop(0, ATT_UNITS, unit_b, 0, unroll=ATT_UNROLL)


def _attn_prompt(qkvc, kvbc, sinks_lanes, batch, seq):
    t = batch * seq
    assert seq % ATT_SB == 0
    n_sb = seq // ATT_SB
    n_pairs = WIDTH // LANES
    cur = lambda comp: (lambda b, s, p: (comp * n_pairs + p, b * n_sb + s, 0))
    prev = lambda comp: (lambda b, s, p: (comp * n_pairs + p, b * n_sb + jnp.maximum(s - 1, 0), 0))
    blk = lambda f: pl.BlockSpec((1, ATT_SB, LANES), f)
    kvb_cur = lambda c: pl.BlockSpec((1, ATT_SB, LANES), lambda b, s, p: (c, b * n_sb + s, 0))
    kvb_prev = lambda c: pl.BlockSpec(
        (1, BLOCK, LANES), lambda b, s, p: (c, jnp.maximum((b * n_sb + s) * ATT_UNITS - 1, b * n_sb * ATT_UNITS), 0))
    out_spec = pl.BlockSpec((1, ATT_SB, LANES), lambda b, s, p: (p, b * n_sb + s, 0))
    return pl.pallas_call(
        _attn_prompt_kernel,
        out_shape=(jax.ShapeDtypeStruct((n_pairs, t, LANES), F32),
                   jax.ShapeDtypeStruct((n_pairs, t, LANES), F32)),
        grid=(batch, n_sb, n_pairs),
        in_specs=[blk(cur(0)), blk(cur(1)), blk(prev(1)), blk(cur(2)), blk(prev(2)), blk(cur(3)),
                  kvb_cur(0), kvb_prev(0), kvb_cur(1), kvb_prev(1),
                  pl.BlockSpec((1, LANES), lambda b, s, p: (0, p))],
        out_specs=(out_spec, out_spec),
        scratch_shapes=[
            pltpu.VMEM((2 * ATT_SB, LANES), F32), pltpu.VMEM((2 * ATT_SB, LANES), F32),
            pltpu.VMEM((BLOCK + ATT_SB, LANES), F32), pltpu.VMEM((BLOCK + ATT_SB, LANES), F32),
            pltpu.VMEM((len(DILATIONS), ATT_SB, LANES), F32), pltpu.VMEM((len(DILATIONS), ATT_SB, LANES), F32)],
        compiler_params=_cparams(("parallel", "parallel", "arbitrary"), 48),
        name="attn_prompt",
    )(qkvc, qkvc, qkvc, qkvc, qkvc, qkvc, kvbc, kvbc, kvbc, kvbc, sinks_lanes)


def _hi_dot(a, b, dims=(((1,), (0,)), ((), ()))):
    return lax.dot_general(a, b, dims, preferred_element_type=F32, precision=lax.Precision.HIGHEST)


def _attn_sample_kernel(cnt_ref, qa_ref, kn_ref, vn_ref, qb_ref, kbn_ref, vbn_ref, sink_ref,
                        kt_ref, vt_ref, kbt_ref, vbt_ref, oa_ref, ob_ref):
    hps = N_HEADS // N_KV_B
    nt = (((1,), (1,)), ((), ()))
    head = lax.broadcasted_iota(I32, (hps, 1), 0)
    cnt = cnt_ref[...]
    n_pat = float(len(DILATIONS))

    qa = qa_ref[0] * SCALE
    qa_bf = qa.astype(BF16)
    s = jnp.zeros((hps, cnt.shape[1]), F32)
    for hl in range(hps):
        s_hl = jnp.dot(qa_bf, kt_ref[0, hl].astype(BF16), preferred_element_type=F32)
        s = jnp.where(head == hl, s_hl, s)
    s = jnp.where(cnt > 0.0, s, MASK_VALUE)
    s_new = jnp.sum(qa * kn_ref[0], axis=1, keepdims=True)
    m = jnp.maximum(jnp.max(s, axis=1, keepdims=True), s_new)
    e = cnt * jnp.exp(s - m)
    e_new = n_pat * jnp.exp(s_new - m)
    den = jnp.sum(e, axis=1, keepdims=True) + e_new
    e_bf = e.astype(BF16)
    pv = jnp.zeros((hps, HEAD_DIM), F32)
    for hl in range(hps):
        o_hl = lax.dot_general(e_bf, vt_ref[0, hl].astype(BF16), nt, preferred_element_type=F32)
        pv = jnp.where(head == hl, o_hl, pv)
    oa_ref[0] = (pv + e_new * vn_ref[0]) / den

    qb = qb_ref[0] * SCALE
    sb = _hi_dot(qb, kbt_ref[0, 0])
    sn = jnp.sum(qb * kbn_ref[0, 0], axis=1, keepdims=True)
    mm = jnp.maximum(jnp.max(sb, axis=1, keepdims=True), sn)
    eb = jnp.exp(sb - mm)
    en = jnp.exp(sn - mm)
    l = jnp.sum(eb, axis=1, keepdims=True) + en
    ob = (_hi_dot(eb, vbt_ref[0, 0], nt) + en * vbn_ref[0, 0]) / l
    lse = mm + jnp.log(l)
    ob_ref[0] = ob * jax.nn.sigmoid(lse - sink_ref[0])


def _attn_sample(qkvc_s, kvbc_s, cache_a_k, cache_a_v, cache_b_k, cache_b_v, sinks):
    nb = qkvc_s.shape[1]
    wa = cache_a_k.shape[1]
    hps = N_HEADS // N_KV_B
    assert cache_b_k.shape[1] == WINDOW_B and all(wa >= w and wa % d == 0 for w, d in DILATIONS)
    dist = wa - np.arange(wa)
    cnt = sum(((dist % d == 0) & (dist <= w)).astype(np.float32) for w, d in DILATIONS).reshape(1, wa)
    heads_rows = lambda c: c.reshape(c.shape[0], nb, LANES // HEAD_DIM, HEAD_DIM).transpose(1, 0, 2, 3)
    q4 = heads_rows(qkvc_s).reshape(nb, 4 * N_HEADS, HEAD_DIM)
    kvn = heads_rows(kvbc_s).reshape(nb, 2 * N_KV_B, 1, HEAD_DIM)
    to_row_minor = lambda c: jnp.transpose(c, (0, 2, 3, 1))
    q_spec = lambda comp: pl.BlockSpec((1, hps, HEAD_DIM), lambda b, hh: (b, comp * N_KV_B + hh, 0))
    n_spec = lambda comp: pl.BlockSpec((1, 1, 1, HEAD_DIM), lambda b, hh: (b, comp * N_KV_B + hh, 0, 0))
    a_spec = pl.BlockSpec((1, hps, HEAD_DIM, wa), lambda b, hh: (b, hh, 0, 0))
    b_spec = pl.BlockSpec((1, 1, HEAD_DIM, WINDOW_B), lambda b, hh: (b, hh, 0, 0))
    o_spec = pl.BlockSpec((1, hps, HEAD_DIM), lambda b, hh: (b, hh, 0))
    oa, ob = pl.pallas_call(
        _attn_sample_kernel,
        out_shape=(jax.ShapeDtypeStruct((nb, N_HEADS, HEAD_DIM), F32),
                   jax.ShapeDtypeStruct((nb, N_HEADS, HEAD_DIM), F32)),
        grid=(nb, N_KV_B),
        in_specs=[
            pl.BlockSpec((1, wa), lambda b, hh: (0, 0)),
            q_spec(0), q_spec(1), q_spec(2), q_spec(3), n_spec(0), n_spec(1),
            pl.BlockSpec((1, hps, 1), lambda b, hh: (hh, 0, 0)),
            a_spec, a_spec, b_spec, b_spec,
        ],
        out_specs=(o_spec, o_spec),
        compiler_params=_cparams(("parallel", "parallel"), 40),
        name="attn_sample",
    )(jnp.asarray(cnt), q4, q4, q4, q4, kvn, kvn, sinks.reshape(N_KV_B, hps, 1),
      to_row_minor(cache_a_k), to_row_minor(cache_a_v), to_row_minor(cache_b_k), to_row_minor(cache_b_v))
    chunked = lambda o: o.reshape(nb, WIDTH // LANES, LANES).transpose(1, 0, 2)
    return chunked(oa), chunked(ob)


def _split_bf16(x):
    hi = x.astype(BF16)
    lo = (x - hi.astype(F32)).astype(BF16)
    return hi, lo


def _post_first_kernel(n_valid, *refs):
    i = pl.program_id(0)

    @pl.when(i < n_valid)
    def _():
        _post_body(*refs)

    @pl.when(i >= n_valid)
    def _():
        for o_ref in refs[-4:]:
            o_ref[...] = jnp.zeros(o_ref.shape, o_ref.dtype)


def _post_append_kernel(x1_all, h2_all, ridx_all, rgate_all, *refs):
    del x1_all, h2_all, ridx_all, rgate_all
    _post_body(*refs)


def _post_body(oa_ref, ob_ref, x_ref, ga_ref, gb_ref, w_ref, b_ref, gf_ref, wr_ref, br_ref,
               x1_ref, h2_ref, ridx_ref, rgate_ref):
    unchunk = lambda ref: jnp.concatenate([ref[c] for c in range(ref.shape[0])], axis=1)
    na = _rmsnorm(unchunk(oa_ref), ga_ref[...]).astype(BF16)
    nb = _rmsnorm(unchunk(ob_ref), gb_ref[...]).astype(BF16)
    c = jnp.concatenate([na, nb], axis=1)
    x1 = x_ref[...] + jnp.dot(c, w_ref[...], preferred_element_type=F32) + b_ref[...]
    x1_ref[...] = x1
    h2 = _rmsnorm(x1, gf_ref[...])
    h2_ref[...] = h2

    h_hi, h_lo = _split_bf16(h2)
    w_hi, w_lo = _split_bf16(wr_ref[...])
    dot = lambda a, b: jnp.dot(a, b, preferred_element_type=F32)
    logits = dot(h_hi, w_hi) + (dot(h_hi, w_lo) + dot(h_lo, w_hi)) + br_ref[...]

    tm = logits.shape[0]
    eidx = lax.broadcasted_iota(I32, (tm, N_EXPERTS), 1)
    lane = lax.broadcasted_iota(I32, (tm, LANES), 1)
    work = logits
    vals, idxs = [], []
    for _ in range(TOP_K):
        v = jnp.max(work, axis=1, keepdims=True)
        i = jnp.min(jnp.where(work == v, eidx, N_EXPERTS), axis=1, keepdims=True)
        vals.append(v)
        idxs.append(i)
        work = jnp.where(eidx == i, -jnp.inf, work)
    es = [jnp.exp(v - vals[0]) for v in vals]
    den = sum(es)
    ridx = jnp.zeros((tm, LANES), I32)
    rgate = jnp.zeros((tm, LANES), F32)
    for k in range(TOP_K):
        ridx = jnp.where(lane == k, idxs[k], ridx)
        rgate = jnp.where(lane == k, es[k] / den, rgate)
    ridx_ref[...] = ridx
    rgate_ref[...] = rgate


def _post_attention(oa, ob, x, g_out_a, g_out_b, w_out_bf, b_out, g_ffn, w_router, b_router, tm,
                    out_rows=None, append_to=None, row0=0):
    t = x.shape[0]
    n_valid = t // tm
    n_pairs = WIDTH // LANES
    clamp = lambda i: jnp.minimum(i, n_valid - 1)
    chunked = pl.BlockSpec((n_pairs, tm, LANES), lambda i: (0, clamp(i), 0))
    full = lambda r, c: pl.BlockSpec((r, c), lambda i: (0, 0))
    in_specs = [chunked, chunked, pl.BlockSpec((tm, D_MODEL), lambda i: (clamp(i), 0)),
                full(1, WIDTH), full(1, WIDTH), full(D_MODEL, D_MODEL), full(1, D_MODEL), full(1, D_MODEL),
                full(D_MODEL, N_EXPERTS), full(1, N_EXPERTS)]
    args = (oa, ob, x, g_out_a, g_out_b, w_out_bf, b_out, g_ffn, w_router, b_router)
    if append_to is None:
        rows = t if out_rows is None else out_rows
        assert rows % tm == 0 and t % tm == 0
        kernel_fn, aliases, blk0 = functools.partial(_post_first_kernel, n_valid), {}, 0
    else:
        rows = append_to[0].shape[0]
        assert row0 % tm == 0 and t % tm == 0
        kernel_fn, aliases, blk0 = _post_append_kernel, {k: k for k in range(4)}, row0 // tm
        in_specs = [pl.BlockSpec(memory_space=pl.ANY)] * 4 + in_specs
        args = tuple(append_to) + args
    n_steps = rows // tm if append_to is None else n_valid
    outblk = lambda w: pl.BlockSpec((tm, w), lambda i: (blk0 + i, 0))
    return pl.pallas_call(
        kernel_fn,
        out_shape=(jax.ShapeDtypeStruct((rows, D_MODEL), F32), jax.ShapeDtypeStruct((rows, D_MODEL), F32),
                   jax.ShapeDtypeStruct((rows, LANES), I32), jax.ShapeDtypeStruct((rows, LANES), F32)),
        grid=(n_steps,),
        in_specs=in_specs,
        out_specs=(outblk(D_MODEL), outblk(D_MODEL), outblk(LANES), outblk(LANES)),
        input_output_aliases=aliases,
        compiler_params=_cparams(("arbitrary",), 48),
        name="post_attention",
    )(*args)


def _dispatch_kernel(tok_ref, h_hbm, o_ref, buf, sem):
    i = pl.program_id(0)

    def issue(step, slot):
        base = step * DISPATCH_ROWS

        def body(r, carry):
            pltpu.make_async_copy(h_hbm.at[pl.ds(tok_ref[base + r], 1)],
                                  buf.at[slot, pl.ds(r, 1)], sem.at[slot]).start()
            return carry
        lax.fori_loop(0, DISPATCH_ROWS, body, 0, unroll=8)

    @pl.when(i == 0)
    def _():
        issue(0, 0)

    @pl.when(i + 1 < pl.num_programs(0))
    def _():
        issue(i + 1, (i + 1) % 2)

    slot = i % 2
    pltpu.make_async_copy(h_hbm.at[pl.ds(0, DISPATCH_ROWS)], buf.at[slot], sem.at[slot]).wait()
    o_ref[...] = buf[slot].astype(BF16)


def _dispatch(slot_tok, h2):
    p = slot_tok.shape[0]
    return pl.pallas_call(
        _dispatch_kernel,
        out_shape=jax.ShapeDtypeStruct((p, D_MODEL), BF16),
        grid_spec=pltpu.PrefetchScalarGridSpec(
            num_scalar_prefetch=1,
            grid=(p // DISPATCH_ROWS,),
            in_specs=[pl.BlockSpec(memory_space=pl.ANY)],
            out_specs=pl.BlockSpec((DISPATCH_ROWS, D_MODEL), lambda i, tok: (i, 0)),
            scratch_shapes=[pltpu.VMEM((2, DISPATCH_ROWS, D_MODEL), F32), pltpu.SemaphoreType.DMA((2,))],
        ),
        compiler_params=_cparams(("arbitrary",), 32),
        name="moe_dispatch",
    )(slot_tok, h2)


def _moe_kernel(ie_ref, iblk_ref, insub_ref, iact_ref, used_ref,
                x_hbm, wg_ref, wu_ref, wd_ref, bg_ref, bu_ref, bd_ref, y_hbm,
                xbuf, act, ystage, xsem, ysem):
    it = pl.program_id(0)
    s = pl.program_id(1)
    nsub = insub_ref[it]
    row0 = iblk_ref[it] * MOE_SUB

    xslot = it % 2

    def x_copies(item, op):
        first = iblk_ref[item] * MOE_SUB

        def body(j, c):
            cp = pltpu.make_async_copy(x_hbm.at[pl.ds(first + j * MOE_SUB, MOE_SUB)],
                                       xbuf.at[item % 2, pl.ds(j * MOE_SUB, MOE_SUB)], xsem.at[item % 2])
            cp.start() if op == "start" else cp.wait()
            return c
        lax.fori_loop(0, insub_ref[item], body, 0)

    def for_row_chunks(fn):
        big = nsub // 4

        def body(j, c):
            fn(pl.multiple_of(j * (4 * MOE_SUB), 4 * MOE_SUB), 4 * MOE_SUB)
            return c
        lax.fori_loop(0, big, body, 0)
        rem = nsub - 4 * big
        has2 = rem >= 2

        @pl.when(has2)
        def _():
            fn(pl.multiple_of(big * (4 * MOE_SUB), MOE_SUB), 2 * MOE_SUB)

        @pl.when((rem & 1) == 1)
        def _():
            fn(pl.multiple_of((4 * big + jnp.where(has2, 2, 0)) * MOE_SUB, MOE_SUB), MOE_SUB)

    @pl.when((s == 0) & (it == 0))
    def _():
        x_copies(0, "start")

    @pl.when(s == 0)
    def _():
        x_copies(it, "wait")

    @pl.when((s == 1) & (it + 1 < pl.num_programs(0)))
    def _():
        x_copies(it + 1, "start")

    @pl.when((s < MOE_F) & (nsub > 0))
    def _():
        bg, bu = bg_ref[...], bu_ref[...]

        def gate_up(r0, rows):
            x = xbuf[xslot, pl.ds(r0, rows), :]
            g = jnp.dot(x, wg_ref[...].astype(BF16), preferred_element_type=F32) + bg
            u = jnp.dot(x, wu_ref[...].astype(BF16), preferred_element_type=F32) + bu
            g = jnp.minimum(g, SWIGLU_LIMIT)
            u = jnp.clip(u, -SWIGLU_LIMIT, SWIGLU_LIMIT)
            a = g * jax.nn.sigmoid(SWIGLU_ALPHA * g) * (u + 1.0)
            act[s, pl.ds(r0, rows), :] = a.astype(BF16)
        for_row_chunks(gate_up)

    @pl.when((s >= MOE_F) & (nsub > 0))
    def _():
        n = s - MOE_F
        slot = n % 2
        bd = bd_ref[...]

        def y_copy(step, r0, rows):
            c0 = pl.multiple_of(step * MOE_TF, MOE_TF)
            return pltpu.make_async_copy(
                ystage.at[step % 2, pl.ds(r0, rows), :],
                y_hbm.at[pl.ds(row0 + r0, rows), pl.ds(c0, MOE_TF)], ysem.at[step % 2])

        def down(r0, rows):
            a = jnp.concatenate([act[f, pl.ds(r0, rows), :] for f in range(MOE_F)], axis=1)
            ystage[slot, pl.ds(r0, rows), :] = (
                jnp.dot(a, wd_ref[...].astype(BF16), preferred_element_type=F32) + bd)
            y_copy(n, r0, rows).start()
        for_row_chunks(down)

        @pl.when(n > 0)
        def _():
            for_row_chunks(lambda r0, rows: y_copy(n - 1, r0, rows).wait())

        @pl.when(n == MOE_F - 1)
        def _():
            for_row_chunks(lambda r0, rows: y_copy(n, r0, rows).wait())

    @pl.when((it == pl.num_programs(0) - 1) & (s == pl.num_programs(1) - 1))
    def _():
        n_blocks = y_hbm.shape[0] // MOE_SUB
        ystage[0, pl.ds(0, MOE_SUB), :] = jnp.zeros((MOE_SUB, MOE_TF), F32)

        def z_copy(b, n):
            return pltpu.make_async_copy(
                ystage.at[0, pl.ds(0, MOE_SUB), :],
                y_hbm.at[pl.ds(b * MOE_SUB, MOE_SUB), pl.ds(n * MOE_TF, MOE_TF)], ysem.at[0])

        def fill(b, c):
            for n in range(MOE_F):
                z_copy(b, n).start()
            for n in range(MOE_F):
                z_copy(b, n).wait()
            return c
        lax.fori_loop(used_ref[0], n_blocks, fill, 0)


def _moe_experts(items, x_sorted, w_gate_up, b_gate_up, w_down, b_down):
    item_e, item_blk, item_nsub, item_act, used_blocks = items
    n_items = item_e.shape[0]
    p = x_sorted.shape[0]
    rmax = MOE_MAX_SUB * MOE_SUB
    last = MOE_F - 1
    f_of = lambda s, act: jnp.where(act == 1, jnp.minimum(s, last), last)
    n_of = lambda s, act: jnp.where(act == 1, jnp.maximum(s - MOE_F, 0), last)
    wg_map = lambda i, s, ie, ib, ins, ia, iu: (ie[i], 0, f_of(s, ia[i]))
    wu_map = lambda i, s, ie, ib, ins, ia, iu: (ie[i], 0, MOE_F + f_of(s, ia[i]))
    wd_map = lambda i, s, ie, ib, ins, ia, iu: (ie[i], 0, n_of(s, ia[i]))
    bgu = b_gate_up.reshape(N_EXPERTS, 1, 2 * D_FF)
    bd = b_down.reshape(N_EXPERTS, 1, D_MODEL)
    return pl.pallas_call(
        _moe_kernel,
        out_shape=jax.ShapeDtypeStruct((p, D_MODEL), F32),
        grid_spec=pltpu.PrefetchScalarGridSpec(
            num_scalar_prefetch=5,
            grid=(n_items, 2 * MOE_F),
            in_specs=[
                pl.BlockSpec(memory_space=pl.ANY),
                pl.BlockSpec((None, D_MODEL, MOE_TF), wg_map),
                pl.BlockSpec((None, D_MODEL, MOE_TF), wu_map),
                pl.BlockSpec((None, D_FF, MOE_TF), wd_map),
                pl.BlockSpec((None, 1, MOE_TF), wg_map),
                pl.BlockSpec((None, 1, MOE_TF), wu_map),
                pl.BlockSpec((None, 1, MOE_TF), wd_map),
            ],
            out_specs=pl.BlockSpec(memory_space=pl.ANY),
            scratch_shapes=[
                pltpu.VMEM((2, rmax, D_MODEL), BF16),
                pltpu.VMEM((MOE_F, rmax, MOE_TF), BF16),
                pltpu.VMEM((2, rmax, MOE_TF), F32),
                pltpu.SemaphoreType.DMA((2,)),
                pltpu.SemaphoreType.DMA((2,)),
            ],
        ),
        compiler_params=_cparams(("arbitrary", "arbitrary"), 56),
        name="moe_experts",
    )(item_e, item_blk, item_nsub, item_act, used_blocks, x_sorted, w_gate_up, w_gate_up, w_down, bgu, bgu, bd)


def _tail_kernel(n_first, dest_ref, y_hbm, gate_ref, x1_ref, p_ref, wg_ref, wp_ref, gp_ref, gf_ref,
                 o1_ref, o2_ref, ybuf, sem):
    i = pl.program_id(0)

    def issue(step, slot):
        base = step * (TAIL_ROWS * TOP_K)

        def body(r, carry):
            for k in range(TOP_K):
                pltpu.make_async_copy(y_hbm.at[pl.ds(dest_ref[base + r * TOP_K + k], 1)],
                                      ybuf.at[slot, k, pl.ds(r, 1)], sem.at[slot]).start()
            return carry
        lax.fori_loop(0, TAIL_ROWS, body, 0, unroll=2)

    @pl.when(i == 0)
    def _():
        issue(0, 0)

    @pl.when(i + 1 < pl.num_programs(0))
    def _():
        issue(i + 1, (i + 1) % 2)

    slot = i % 2
    for k in range(TOP_K):
        pltpu.make_async_copy(y_hbm.at[pl.ds(0, TAIL_ROWS)], ybuf.at[slot, k], sem.at[slot]).wait()

    gate = gate_ref[...]
    moe = ybuf[slot, 0] * gate[:, 0:1]
    for k in range(1, TOP_K):
        moe = moe + ybuf[slot, k] * gate[:, k:k + 1]
    x2 = x1_ref[...] + moe
    h3 = _rmsnorm(x2, gp_ref[...]).astype(BF16)
    ple_gate = jax.nn.sigmoid(jnp.dot(h3, wg_ref[...], preferred_element_type=F32))
    ple = jnp.dot(p_ref[...].astype(BF16), wp_ref[...], preferred_element_type=F32)
    x3 = x2 + ple_gate * ple
    y = _rmsnorm(x3, gf_ref[...])

    @pl.when(i < n_first)
    def _():
        o1_ref[...] = y

    @pl.when(i >= n_first)
    def _():
        o2_ref[...] = y


def _tail(dest, y_sorted, rgate, x1, p_all, w_ple_gate_bf, w_ple_proj_bf, g_ple, g_final, t_first):
    t = dest.shape[0] // TOP_K
    assert t_first % TAIL_ROWS == 0 and t % TAIL_ROWS == 0
    n_first = t_first // TAIL_ROWS
    rowblk = lambda w: pl.BlockSpec((TAIL_ROWS, w), lambda i, d: (i, 0))
    full = lambda r, c: pl.BlockSpec((r, c), lambda i, d: (0, 0))
    return pl.pallas_call(
        functools.partial(_tail_kernel, n_first),
        out_shape=(jax.ShapeDtypeStruct((t_first, D_MODEL), F32),
                   jax.ShapeDtypeStruct((t - t_first, D_MODEL), F32)),
        grid_spec=pltpu.PrefetchScalarGridSpec(
            num_scalar_prefetch=1,
            grid=(t // TAIL_ROWS,),
            in_specs=[
                pl.BlockSpec(memory_space=pl.ANY),
                rowblk(LANES), rowblk(D_MODEL), rowblk(PLE_DIM),
                full(D_MODEL, D_MODEL), full(PLE_DIM, D_MODEL), full(1, D_MODEL), full(1, D_MODEL)],
            out_specs=(
                pl.BlockSpec((TAIL_ROWS, D_MODEL), lambda i, d: (jnp.minimum(i, n_first - 1), 0)),
                pl.BlockSpec((TAIL_ROWS, D_MODEL), lambda i, d: (jnp.maximum(i - n_first, 0), 0))),
            scratch_shapes=[pltpu.VMEM((2, TOP_K, TAIL_ROWS, D_MODEL), F32), pltpu.SemaphoreType.DMA((2,))],
        ),
        compiler_params=_cparams(("arbitrary",), 48),
        name="moe_tail",
    )(dest, y_sorted, rgate, x1, p_all, w_ple_gate_bf, w_ple_proj_bf, g_ple, g_final)


def _routing(ridx, rgate_unused, t):
    del rgate_unused
    m = t * TOP_K
    flat_e = ridx[:, :TOP_K].reshape(m)
    onehot = (flat_e[:, None] == jnp.arange(N_EXPERTS, dtype=I32)[None, :]).astype(I32)
    csum = jnp.cumsum(onehot, axis=0)
    rank = jnp.sum((csum - onehot) * onehot, axis=1)
    counts = csum[-1]
    nblk = (counts + MOE_SUB - 1) // MOE_SUB
    blk_end = jnp.cumsum(nblk)
    blk_start = blk_end - nblk
    dest = blk_start[flat_e] * MOE_SUB + rank
    n_blocks = -(-m // MOE_SUB) + N_EXPERTS
    n_blocks = -(-n_blocks // (DISPATCH_ROWS // MOE_SUB)) * (DISPATCH_ROWS // MOE_SUB)
    p = n_blocks * MOE_SUB
    flat_tok = jnp.arange(m, dtype=I32) // TOP_K
    slot_tok = jnp.zeros((p,), I32).at[dest].set(flat_tok, unique_indices=True)

    n_items = N_EXPERTS + -(-n_blocks // MOE_MAX_SUB)
    items_per_e = (nblk + MOE_MAX_SUB - 1) // MOE_MAX_SUB
    item_end = jnp.cumsum(items_per_e)
    total = item_end[-1]
    ids = jnp.arange(n_items, dtype=I32)
    active = ids < total
    ids_c = jnp.minimum(ids, total - 1)
    item_e = jnp.searchsorted(item_end, ids_c, side="right").astype(I32)
    local = ids_c - (item_end - items_per_e)[item_e]
    item_blk = blk_start[item_e] + local * MOE_MAX_SUB
    item_nsub = jnp.where(active, jnp.minimum(MOE_MAX_SUB, nblk[item_e] - local * MOE_MAX_SUB), 0)
    items = (item_e, item_blk.astype(I32), item_nsub.astype(I32), active.astype(I32),
             blk_end[-1:].astype(I32))
    return slot_tok, dest.astype(I32), items


def kernel(x_prompt, x_sample, cache_a_k, cache_a_v, cache_b_k, cache_b_v, p_prompt, p_sample, g_attn, w_in, b_in, sinks, g_out_a, g_out_b, w_out, b_out, g_ffn, w_router, b_router, w_gate_up, b_gate_up, w_down, b_down, g_ple, w_ple_gate, w_ple_proj, g_final):
    batch, seq, _ = x_prompt.shape
    nb, nseq, _ = x_sample.shape
    assert nseq == 1 and g_attn.shape[0] == 1
    tp, ts = batch * seq, nb * nseq
    row = lambda v: v.reshape(1, -1)

    w_in_bf = _cast_bf16(w_in[0])
    w_out_bf = _cast_bf16(w_out[0])
    w_pg_bf = _cast_bf16(w_ple_gate[0])
    w_pp_bf = _cast_bf16(w_ple_proj[0])

    tabs_p = _rope_tables(jnp.arange(seq, dtype=I32))
    tabs_s = _rope_tables(jnp.full((ts,), PAST_LEN, I32))
    sinks_lanes = jnp.repeat(sinks[0], HEAD_DIM).reshape(1, WIDTH)

    tm_p = 512
    xp = x_prompt.reshape(tp, D_MODEL)
    xs = x_sample.reshape(ts, D_MODEL)
    win_a, win_b = min(DILATIONS[-1][0], seq), min(WINDOW_B, seq)
    qkvc_p, kvbc_p, kt_p, vt_p, kvbt_p = _qkv_proj(xp, row(g_attn), w_in_bf, row(b_in), tabs_p, tm_p,
                                                    batch, win_a, win_b)
    qkvc_s, kvbc_s, kt_s, vt_s, kvbt_s = _qkv_proj(xs, row(g_attn), w_in_bf, row(b_in), tabs_s, ts, 1, ts, ts)

    oa_p, ob_p = _attn_prompt(qkvc_p, kvbc_p, sinks_lanes, batch, seq)
    oa_s, ob_s = _attn_sample(qkvc_s, kvbc_s, cache_a_k[0], cache_a_v[0], cache_b_k[0], cache_b_v[0], sinks[0])

    post = functools.partial(_post_attention, g_out_a=row(g_out_a), g_out_b=row(g_out_b), w_out_bf=w_out_bf,
                             b_out=row(b_out), g_ffn=row(g_ffn), w_router=w_router[0], b_router=row(b_router))
    t = tp + ts
    tm_post = 256
    merged = post(oa_p, ob_p, xp, tm=tm_post, out_rows=-(-t // tm_post) * tm_post)
    x1, h2, ridx, rgate = post(oa_s, ob_s, xs, tm=ts, append_to=merged, row0=tp)

    cat = lambda a, b: jnp.concatenate([a, b], axis=0)
    slot_tok, dest, items = _routing(ridx[:t], rgate, t)
    x_sorted = _dispatch(slot_tok, h2)
    y_sorted = _moe_experts(items, x_sorted, w_gate_up[0], b_gate_up[0], w_down[0], b_down[0])
    p_all = cat(p_prompt[0].reshape(tp, PLE_DIM), p_sample[0].reshape(ts, PLE_DIM))
    y_p, y_s = _tail(dest, y_sorted, rgate, x1, p_all, w_pg_bf, w_pp_bf, row(g_ple), row(g_final), tp)

    y_prompt = y_p.reshape(batch, seq, D_MODEL)
    y_sample = y_s.reshape(nb, nseq, D_MODEL)

    def rows_out(t3, heads):
        n, _, rows = t3.shape
        return t3.reshape(n, heads, HEAD_DIM, rows).transpose(0, 3, 1, 2)[None]

    def sample_out(t3, heads):
        return t3.reshape(heads, HEAD_DIM, nb).transpose(2, 0, 1)[None, :, None]

    return (y_prompt, y_sample,
            rows_out(kt_p, N_HEADS), rows_out(vt_p, N_HEADS),
            rows_out(kvbt_p[:, :KV_WIDTH_B], N_KV_B), rows_out(kvbt_p[:, KV_WIDTH_B:], N_KV_B),
            sample_out(kt_s, N_HEADS), sample_out(vt_s, N_HEADS),
            sample_out(kvbt_s[:, :KV_WIDTH_B], N_KV_B), sample_out(kvbt_s[:, KV_WIDTH_B:], N_KV_B))
```

```python
import functools

import jax
import jax.numpy as jnp
import numpy as np
from jax import lax
from jax.experimental import pallas as pl
from jax.experimental.pallas import tpu as pltpu

F32 = jnp.float32
BF16 = jnp.bfloat16
I32 = jnp.int32

D_MODEL = 2048
HEAD_DIM = 64
N_HEADS = 16
WIDTH = N_HEADS * HEAD_DIM
N_KV_B = 2
KV_WIDTH_B = N_KV_B * HEAD_DIM
DILATIONS = ((128, 1), (512, 4), (2048, 16))
WINDOW_B = 128
BLOCK = 128
N_EXPERTS = 32
TOP_K = 4
D_FF = 2048
SWIGLU_ALPHA = 1.702
SWIGLU_LIMIT = 7.0
PLE_DIM = 256
ROPE_THETA = 10000.0
NORM_EPS = 1e-5
MASK_VALUE = -1e30
SCALE = HEAD_DIM ** -0.5
PAST_LEN = 16384
QKV4_COLS = 4 * WIDTH
IN_COLS = QKV4_COLS + 2 * KV_WIDTH_B

LANES = 128
VMEM_LIMIT_CAP = 60 * 1024 * 1024

MOE_TF = 512
MOE_F = D_FF // MOE_TF
MOE_SUB = 128
MOE_MAX_SUB = 12
DISPATCH_ROWS = 256
TAIL_ROWS = 128


def _cparams(semantics, vmem_mb):
    return pltpu.CompilerParams(
        dimension_semantics=semantics,
        vmem_limit_bytes=min(vmem_mb * 1024 * 1024, VMEM_LIMIT_CAP))


def _rmsnorm(x, g):
    ms = jnp.mean(x * x, axis=-1, keepdims=True)
    return x * lax.rsqrt(ms + NORM_EPS) * g


def _cast_kernel(x_ref, o_ref):
    o_ref[...] = x_ref[...].astype(o_ref.dtype)


def _cast_bf16(w, rows_per_step=256):
    r, c = w.shape
    return pl.pallas_call(
        _cast_kernel,
        out_shape=jax.ShapeDtypeStruct((r, c), BF16),
        grid=(r // rows_per_step,),
        in_specs=[pl.BlockSpec((rows_per_step, c), lambda i: (i, 0))],
        out_specs=pl.BlockSpec((rows_per_step, c), lambda i: (i, 0)),
        compiler_params=_cparams(("parallel",), 32),
        name="cast_bf16",
    )(w)


def _rope(yc, cos, sin_lo, sin_hi):
    return yc * cos + pltpu.roll(yc, LANES - 32, 1) * sin_lo + pltpu.roll(yc, 32, 1) * sin_hi


def _qkv_kernel(tiles_per_seq, first_win_tile, win_b,
                x_ref, g_ref, w_ref, wkv_ref, b_ref, bkv_ref, cos_ref, slo_ref, shi_ref,
                qkv_ref, kvb_ref, kt_ref, vt_ref, kvbt_ref, h_scr):
    i = pl.program_id(0)
    j = pl.program_id(1)
    tile_in_seq = i % tiles_per_seq
    in_window = tile_in_seq >= first_win_tile
    n_chunks = WIDTH // LANES
    chunk = lambda v, c: v[:, c * LANES:(c + 1) * LANES]

    @pl.when(j == 0)
    def _():
        h_scr[...] = _rmsnorm(x_ref[...], g_ref[...]).astype(BF16)

    h = h_scr[...]
    cos, slo, shi = cos_ref[...], slo_ref[...], shi_ref[...]
    is_v = j == 2
    for c2 in range(n_chunks // 2):
        cols = slice(2 * c2 * LANES, 2 * (c2 + 1) * LANES)
        y2 = jnp.dot(h, w_ref[:, cols], preferred_element_type=F32) + b_ref[:, cols]
        for k in range(2):
            yc = chunk(y2, k)
            qkv_ref[2 * c2 + k] = jnp.where(is_v, yc, _rope(yc, cos, slo, shi))

    @pl.when((j == 1) & in_window)
    def _():
        for c in range(n_chunks):
            kt_ref[0, c * LANES:(c + 1) * LANES, :] = qkv_ref[c].T

    @pl.when(is_v & in_window)
    def _():
        for c in range(n_chunks):
            vt_ref[0, c * LANES:(c + 1) * LANES, :] = qkv_ref[c].T

    @pl.when(j == 3)
    def _():
        ykv = jnp.dot(h, wkv_ref[...], preferred_element_type=F32) + bkv_ref[...]
        kvb_ref[0] = _rope(chunk(ykv, 0), cos, slo, shi)
        kvb_ref[1] = chunk(ykv, 1)

    @pl.when((j == 3) & (tile_in_seq == tiles_per_seq - 1))
    def _():
        tm = kvb_ref.shape[1]
        kvbt_ref[0, :KV_WIDTH_B, :] = kvb_ref[0, tm - win_b:, :].T
        kvbt_ref[0, KV_WIDTH_B:, :] = kvb_ref[1, tm - win_b:, :].T


def _qkv_proj(x, g_attn, w_in_bf, b_in, rope_tabs, tm, n_seq, win_a, win_b):
    t = x.shape[0]
    seq = t // n_seq
    tiles_per_seq = seq // tm
    assert seq % tm == 0 and win_a % tm == 0 and win_a <= seq and win_b <= tm
    first_win_tile = tiles_per_seq - win_a // tm
    cos, slo, shi = rope_tabs
    tab_blocks = cos.shape[0] // tm
    tab_map = lambda i, j: (i % tab_blocks, 0)
    win_map = lambda i, j: (i // tiles_per_seq, 0, jnp.maximum(i % tiles_per_seq - first_win_tile, 0))
    return pl.pallas_call(
        functools.partial(_qkv_kernel, tiles_per_seq, first_win_tile, win_b),
        out_shape=(jax.ShapeDtypeStruct((QKV4_COLS // LANES, t, LANES), F32),
                   jax.ShapeDtypeStruct((2, t, LANES), F32),
                   jax.ShapeDtypeStruct((n_seq, WIDTH, win_a), F32),
                   jax.ShapeDtypeStruct((n_seq, WIDTH, win_a), F32),
                   jax.ShapeDtypeStruct((n_seq, 2 * KV_WIDTH_B, win_b), F32)),
        grid=(t // tm, 4),
        in_specs=[
            pl.BlockSpec((tm, D_MODEL), lambda i, j: (i, 0)),
            pl.BlockSpec((1, D_MODEL), lambda i, j: (0, 0)),
            pl.BlockSpec((D_MODEL, WIDTH), lambda i, j: (0, j)),
            pl.BlockSpec((D_MODEL, 2 * KV_WIDTH_B), lambda i, j: (0, QKV4_COLS // (2 * KV_WIDTH_B))),
            pl.BlockSpec((1, WIDTH), lambda i, j: (0, j)),
            pl.BlockSpec((1, 2 * KV_WIDTH_B), lambda i, j: (0, QKV4_COLS // (2 * KV_WIDTH_B))),
            pl.BlockSpec((tm, LANES), tab_map),
            pl.BlockSpec((tm, LANES), tab_map),
            pl.BlockSpec((tm, LANES), tab_map),
        ],
        out_specs=(pl.BlockSpec((WIDTH // LANES, tm, LANES), lambda i, j: (j, i, 0)),
                   pl.BlockSpec((2, tm, LANES), lambda i, j: (0, i, 0)),
                   pl.BlockSpec((1, WIDTH, tm), win_map),
                   pl.BlockSpec((1, WIDTH, tm), win_map),
                   pl.BlockSpec((1, 2 * KV_WIDTH_B, win_b), lambda i, j: (i // tiles_per_seq, 0, 0))),
        scratch_shapes=[pltpu.VMEM((tm, D_MODEL), BF16)],
        compiler_params=_cparams(("arbitrary", "arbitrary"), 48),
        name="qkv_proj",
    )(x, g_attn, w_in_bf, w_in_bf, b_in, b_in, cos, slo, shi)


def _rope_tables(pos):
    half = HEAD_DIM // 2
    inv = ROPE_THETA ** (-jnp.arange(half, dtype=F32) / half)
    ang = pos.astype(F32)[:, None] * inv[None, :]
    cos = jnp.tile(jnp.cos(ang), (1, LANES // half))
    sin = jnp.tile(jnp.sin(ang), (1, LANES // half))
    first = (jnp.arange(LANES) % HEAD_DIM) < half
    return cos, jnp.where(first, -sin, 0.0), jnp.where(first, 0.0, sin)


def _two_head_softmax_pv(q0, q1, k2, v2, valid2):
    qm = jnp.concatenate([q0, q1], axis=0).astype(BF16)
    s = lax.dot_general(qm, k2, (((1,), (1,)), ((), ())), preferred_element_type=F32)
    s = jnp.where(valid2, s, MASK_VALUE)
    m = jnp.max(s, axis=1, keepdims=True)
    p = jnp.exp(s - m)
    l = jnp.sum(p, axis=1, keepdims=True)
    o = jnp.dot(p.astype(BF16), v2, preferred_element_type=F32) / l
    lse = m + jnp.log(l)
    return o[:BLOCK], lse[:BLOCK], o[BLOCK:], lse[BLOCK:]


ATT_SB = BLOCK * max(d for _, d in DILATIONS)
ATT_UNITS = ATT_SB // BLOCK
ATT_UNROLL = 8


def _attn_prompt_kernel(qa_ref, ka_ref, kap_ref, va_ref, vap_ref, qb_ref, kb_ref, kbp_ref, vb_ref, vbp_ref,
                        sink_ref, oa_ref, ob_ref, kcat, vcat, kbcat, vbcat, o_scr, lse_scr):
    has_prev = pl.program_id(1) > 0
    pair = pl.program_id(2)
    sb = ATT_SB

    kcat[0:sb, :] = kap_ref[0]
    kcat[sb:2 * sb, :] = ka_ref[0]
    vcat[0:sb, :] = vap_ref[0]
    vcat[sb:2 * sb, :] = va_ref[0]
    kv_first = pair < (N_HEADS // N_KV_B) // 2
    lane_row = lax.broadcasted_iota(I32, (1, LANES), 1)
    kv_lanes = jnp.where(lane_row >= HEAD_DIM, 1, 0) == jnp.where(kv_first, 0, 1)
    both_halves = lambda x: jnp.where(kv_lanes, x, pltpu.roll(x, HEAD_DIM, 1))
    kbcat[0:BLOCK, :] = both_halves(kbp_ref[0])
    kbcat[BLOCK:BLOCK + sb, :] = both_halves(kb_ref[0])
    vbcat[0:BLOCK, :] = both_halves(vbp_ref[0])
    vbcat[BLOCK:BLOCK + sb, :] = both_halves(vb_ref[0])

    lane = lax.broadcasted_iota(I32, (BLOCK, LANES), 1)
    first = lane < HEAD_DIM
    qi = lax.broadcasted_iota(I32, (2 * BLOCK, 2 * BLOCK), 0) & (BLOCK - 1)
    si = lax.broadcasted_iota(I32, (2 * BLOCK, 2 * BLOCK), 1)
    dist = qi - si + BLOCK
    band = (dist >= 0) & (dist <= BLOCK)
    own_block = si >= BLOCK

    def rows(start, size, d):
        if d == 1:
            return pl.ds(pl.multiple_of(start, BLOCK), size)
        return pl.ds(start, size, stride=d)

    for p, (_, d) in enumerate(DILATIONS):
        def unit(u, carry, p=p, d=d):
            blk = u // d
            qs = blk * (BLOCK * d) + u % d
            valid = band & (own_block | (blk > 0) | has_prev)
            q = qa_ref[0, rows(qs, BLOCK, d), :] * SCALE
            k2 = kcat[rows(sb + qs - BLOCK * d, 2 * BLOCK, d), :].astype(BF16)
            v2 = vcat[rows(sb + qs - BLOCK * d, 2 * BLOCK, d), :].astype(BF16)
            o0, l0, o1, l1 = _two_head_softmax_pv(jnp.where(first, q, 0.0), jnp.where(first, 0.0, q), k2, v2, valid)
            o_scr[p, rows(qs, BLOCK, d), :] = jnp.where(first, o0, o1)
            lse_scr[p, rows(qs, BLOCK, d), :] = jnp.where(first, l0, l1)
            return carry
        lax.fori_loop(0, ATT_UNITS, unit, 0, unroll=ATT_UNROLL)

    def mix(bk, carry):
        r = pl.ds(pl.multiple_of(bk * BLOCK, BLOCK), BLOCK)
        lses = [lse_scr[p, r, :] for p in range(len(DILATIONS))]
        m = functools.reduce(jnp.maximum, lses)
        ws = [jnp.exp(l - m) for l in lses]
        oa_ref[0, r, :] = sum(w * o_scr[p, r, :] for p, w in enumerate(ws)) / sum(ws)
        return carry
    lax.fori_loop(0, ATT_UNITS, mix, 0)

    sink = sink_ref[...]

    def unit_b(blk, carry):
        r = pl.ds(pl.multiple_of(blk * BLOCK, BLOCK), BLOCK)
        r2 = pl.ds(pl.multiple_of(blk * BLOCK, BLOCK), 2 * BLOCK)
        valid = band & (own_block | (blk > 0) | has_prev)
        q = qb_ref[0, r, :] * SCALE
        k2 = kbcat[r2, :].astype(BF16)
        v2 = vbcat[r2, :].astype(BF16)
        o0, l0, o1, l1 = _two_head_softmax_pv(jnp.where(first, q, 0.0), jnp.where(first, 0.0, q), k2, v2, valid)
        lse = jnp.where(first, l0, l1)
        ob_ref[0, r, :] = jnp.where(first, o0, o1) * jax.nn.sigmoid(lse - sink)
        return carry
    lax.fori_loop(0, ATT_UNITS, unit_b, 0, unroll=ATT_UNROLL)


def _attn_prompt(qkvc, kvbc, sinks_lanes, batch, seq):
    t = batch * seq
    assert seq % ATT_SB == 0
    n_sb = seq // ATT_SB
    n_pairs = WIDTH // LANES
    cur = lambda comp: (lambda b, s, p: (comp * n_pairs + p, b * n_sb + s, 0))
    prev = lambda comp: (lambda b, s, p: (comp * n_pairs + p, b * n_sb + jnp.maximum(s - 1, 0), 0))
    blk = lambda f: pl.BlockSpec((1, ATT_SB, LANES), f)
    kvb_cur = lambda c: pl.BlockSpec((1, ATT_SB, LANES), lambda b, s, p: (c, b * n_sb + s, 0))
    kvb_prev = lambda c: pl.BlockSpec(
        (1, BLOCK, LANES), lambda b, s, p: (c, jnp.maximum((b * n_sb + s) * ATT_UNITS - 1, b * n_sb * ATT_UNITS), 0))
    out_spec = pl.BlockSpec((1, ATT_SB, LANES), lambda b, s, p: (p, b * n_sb + s, 0))
    return pl.pallas_call(
        _attn_prompt_kernel,
        out_shape=(jax.ShapeDtypeStruct((n_pairs, t, LANES), F32),
                   jax.ShapeDtypeStruct((n_pairs, t, LANES), F32)),
        grid=(batch, n_sb, n_pairs),
        in_specs=[blk(cur(0)), blk(cur(1)), blk(prev(1)), blk(cur(2)), blk(prev(2)), blk(cur(3)),
                  kvb_cur(0), kvb_prev(0), kvb_cur(1), kvb_prev(1),
                  pl.BlockSpec((1, LANES), lambda b, s, p: (0, p))],
        out_specs=(out_spec, out_spec),
        scratch_shapes=[
            pltpu.VMEM((2 * ATT_SB, LANES), F32), pltpu.VMEM((2 * ATT_SB, LANES), F32),
            pltpu.VMEM((BLOCK + ATT_SB, LANES), F32), pltpu.VMEM((BLOCK + ATT_SB, LANES), F32),
            pltpu.VMEM((len(DILATIONS), ATT_SB, LANES), F32), pltpu.VMEM((len(DILATIONS), ATT_SB, LANES), F32)],
        compiler_params=_cparams(("parallel", "parallel", "arbitrary"), 48),
        name="attn_prompt",
    )(qkvc, qkvc, qkvc, qkvc, qkvc, qkvc, kvbc, kvbc, kvbc, kvbc, sinks_lanes)


def _hi_dot(a, b, dims=(((1,), (0,)), ((), ()))):
    return lax.dot_general(a, b, dims, preferred_element_type=F32, precision=lax.Precision.HIGHEST)


def _attn_sample_kernel(cnt_ref, qa_ref, kn_ref, vn_ref, qb_ref, kbn_ref, vbn_ref, sink_ref,
                        kt_ref, vt_ref, kbt_ref, vbt_ref, oa_ref, ob_ref):
    hps = N_HEADS // N_KV_B
    nt = (((1,), (1,)), ((), ()))
    head = lax.broadcasted_iota(I32, (hps, 1), 0)
    cnt = cnt_ref[...]
    n_pat = float(len(DILATIONS))

    qa = qa_ref[0] * SCALE
    qa_bf = qa.astype(BF16)
    s = jnp.zeros((hps, cnt.shape[1]), F32)
    for hl in range(hps):
        s_hl = jnp.dot(qa_bf, kt_ref[0, hl].astype(BF16), preferred_element_type=F32)
        s = jnp.where(head == hl, s_hl, s)
    s = jnp.where(cnt > 0.0, s, MASK_VALUE)
    s_new = jnp.sum(qa * kn_ref[0], axis=1, keepdims=True)
    m = jnp.maximum(jnp.max(s, axis=1, keepdims=True), s_new)
    e = cnt * jnp.exp(s - m)
    e_new = n_pat * jnp.exp(s_new - m)
    den = jnp.sum(e, axis=1, keepdims=True) + e_new
    e_bf = e.astype(BF16)
    pv = jnp.zeros((hps, HEAD_DIM), F32)
    for hl in range(hps):
        o_hl = lax.dot_general(e_bf, vt_ref[0, hl].astype(BF16), nt, preferred_element_type=F32)
        pv = jnp.where(head == hl, o_hl, pv)
    oa_ref[0] = (pv + e_new * vn_ref[0]) / den

    qb = qb_ref[0] * SCALE
    sb = _hi_dot(qb, kbt_ref[0, 0])
    sn = jnp.sum(qb * kbn_ref[0, 0], axis=1, keepdims=True)
    mm = jnp.maximum(jnp.max(sb, axis=1, keepdims=True), sn)
    eb = jnp.exp(sb - mm)
    en = jnp.exp(sn - mm)
    l = jnp.sum(eb, axis=1, keepdims=True) + en
    ob = (_hi_dot(eb, vbt_ref[0, 0], nt) + en * vbn_ref[0, 0]) / l
    lse = mm + jnp.log(l)
    ob_ref[0] = ob * jax.nn.sigmoid(lse - sink_ref[0])


def _attn_sample(qkvc_s, kvbc_s, cache_a_k, cache_a_v, cache_b_k, cache_b_v, sinks):
    nb = qkvc_s.shape[1]
    wa = cache_a_k.shape[1]
    hps = N_HEADS // N_KV_B
    assert cache_b_k.shape[1] == WINDOW_B and all(wa >= w and wa % d == 0 for w, d in DILATIONS)
    dist = wa - np.arange(wa)
    cnt = sum(((dist % d == 0) & (dist <= w)).astype(np.float32) for w, d in DILATIONS).reshape(1, wa)
    heads_rows = lambda c: c.reshape(c.shape[0], nb, LANES // HEAD_DIM, HEAD_DIM).transpose(1, 0, 2, 3)
    q4 = heads_rows(qkvc_s).reshape(nb, 4 * N_HEADS, HEAD_DIM)
    kvn = heads_rows(kvbc_s).reshape(nb, 2 * N_KV_B, 1, HEAD_DIM)
    to_row_minor = lambda c: jnp.transpose(c, (0, 2, 3, 1))
    q_spec = lambda comp: pl.BlockSpec((1, hps, HEAD_DIM), lambda b, hh: (b, comp * N_KV_B + hh, 0))
    n_spec = lambda comp: pl.BlockSpec((1, 1, 1, HEAD_DIM), lambda b, hh: (b, comp * N_KV_B + hh, 0, 0))
    a_spec = pl.BlockSpec((1, hps, HEAD_DIM, wa), lambda b, hh: (b, hh, 0, 0))
    b_spec = pl.BlockSpec((1, 1, HEAD_DIM, WINDOW_B), lambda b, hh: (b, hh, 0, 0))
    o_spec = pl.BlockSpec((1, hps, HEAD_DIM), lambda b, hh: (b, hh, 0))
    oa, ob = pl.pallas_call(
        _attn_sample_kernel,
        out_shape=(jax.ShapeDtypeStruct((nb, N_HEADS, HEAD_DIM), F32),
                   jax.ShapeDtypeStruct((nb, N_HEADS, HEAD_DIM), F32)),
        grid=(nb, N_KV_B),
        in_specs=[
            pl.BlockSpec((1, wa), lambda b, hh: (0, 0)),
            q_spec(0), q_spec(1), q_spec(2), q_spec(3), n_spec(0), n_spec(1),
            pl.BlockSpec((1, hps, 1), lambda b, hh: (hh, 0, 0)),
            a_spec, a_spec, b_spec, b_spec,
        ],
        out_specs=(o_spec, o_spec),
        compiler_params=_cparams(("parallel", "parallel"), 40),
        name="attn_sample",
    )(jnp.asarray(cnt), q4, q4, q4, q4, kvn, kvn, sinks.reshape(N_KV_B, hps, 1),
      to_row_minor(cache_a_k), to_row_minor(cache_a_v), to_row_minor(cache_b_k), to_row_minor(cache_b_v))
    chunked = lambda o: o.reshape(nb, WIDTH // LANES, LANES).transpose(1, 0, 2)
    return chunked(oa), chunked(ob)


def _split_bf16(x):
    hi = x.astype(BF16)
    lo = (x - hi.astype(F32)).astype(BF16)
    return hi, lo


def _post_first_kernel(n_valid, *refs):
    i = pl.program_id(0)

    @pl.when(i < n_valid)
    def _():
        _post_body(*refs)

    @pl.when(i >= n_valid)
    def _():
        for o_ref in refs[-4:]:
            o_ref[...] = jnp.zeros(o_ref.shape, o_ref.dtype)


def _post_append_kernel(x1_all, h2_all, ridx_all, rgate_all, *refs):
    del x1_all, h2_all, ridx_all, rgate_all
    _post_body(*refs)


def _post_body(oa_ref, ob_ref, x_ref, ga_ref, gb_ref, w_ref, b_ref, gf_ref, wr_ref, br_ref,
               x1_ref, h2_ref, ridx_ref, rgate_ref):
    unchunk = lambda ref: jnp.concatenate([ref[c] for c in range(ref.shape[0])], axis=1)
    na = _rmsnorm(unchunk(oa_ref), ga_ref[...]).astype(BF16)
    nb = _rmsnorm(unchunk(ob_ref), gb_ref[...]).astype(BF16)
    c = jnp.concatenate([na, nb], axis=1)
    x1 = x_ref[...] + jnp.dot(c, w_ref[...], preferred_element_type=F32) + b_ref[...]
    x1_ref[...] = x1
    h2 = _rmsnorm(x1, gf_ref[...])
    h2_ref[...] = h2

    h_hi, h_lo = _split_bf16(h2)
    w_hi, w_lo = _split_bf16(wr_ref[...])
    dot = lambda a, b: jnp.dot(a, b, preferred_element_type=F32)
    logits = dot(h_hi, w_hi) + (dot(h_hi, w_lo) + dot(h_lo, w_hi)) + br_ref[...]

    tm = logits.shape[0]
    eidx = lax.broadcasted_iota(I32, (tm, N_EXPERTS), 1)
    lane = lax.broadcasted_iota(I32, (tm, LANES), 1)
    work = logits
    vals, idxs = [], []
    for _ in range(TOP_K):
        v = jnp.max(work, axis=1, keepdims=True)
        i = jnp.min(jnp.where(work == v, eidx, N_EXPERTS), axis=1, keepdims=True)
        vals.append(v)
        idxs.append(i)
        work = jnp.where(eidx == i, -jnp.inf, work)
    es = [jnp.exp(v - vals[0]) for v in vals]
    den = sum(es)
    ridx = jnp.zeros((tm, LANES), I32)
    rgate = jnp.zeros((tm, LANES), F32)
    for k in range(TOP_K):
        ridx = jnp.where(lane == k, idxs[k], ridx)
        rgate = jnp.where(lane == k, es[k] / den, rgate)
    ridx_ref[...] = ridx
    rgate_ref[...] = rgate


def _post_attention(oa, ob, x, g_out_a, g_out_b, w_out_bf, b_out, g_ffn, w_router, b_router, tm,
                    out_rows=None, append_to=None, row0=0):
    t = x.shape[0]
    n_valid = t // tm
    n_pairs = WIDTH // LANES
    clamp = lambda i: jnp.minimum(i, n_valid - 1)
    chunked = pl.BlockSpec((n_pairs, tm, LANES), lambda i: (0, clamp(i), 0))
    full = lambda r, c: pl.BlockSpec((r, c), lambda i: (0, 0))
    in_specs = [chunked, chunked, pl.BlockSpec((tm, D_MODEL), lambda i: (clamp(i), 0)),
                full(1, WIDTH), full(1, WIDTH), full(D_MODEL, D_MODEL), full(1, D_MODEL), full(1, D_MODEL),
                full(D_MODEL, N_EXPERTS), full(1, N_EXPERTS)]
    args = (oa, ob, x, g_out_a, g_out_b, w_out_bf, b_out, g_ffn, w_router, b_router)
    if append_to is None:
        rows = t if out_rows is None else out_rows
        assert rows % tm == 0 and t % tm == 0
        kernel_fn, aliases, blk0 = functools.partial(_post_first_kernel, n_valid), {}, 0
    else:
        rows = append_to[0].shape[0]
        assert row0 % tm == 0 and t % tm == 0
        kernel_fn, aliases, blk0 = _post_append_kernel, {k: k for k in range(4)}, row0 // tm
        in_specs = [pl.BlockSpec(memory_space=pl.ANY)] * 4 + in_specs
        args = tuple(append_to) + args
    n_steps = rows // tm if append_to is None else n_valid
    outblk = lambda w: pl.BlockSpec((tm, w), lambda i: (blk0 + i, 0))
    return pl.pallas_call(
        kernel_fn,
        out_shape=(jax.ShapeDtypeStruct((rows, D_MODEL), F32), jax.ShapeDtypeStruct((rows, D_MODEL), F32),
                   jax.ShapeDtypeStruct((rows, LANES), I32), jax.ShapeDtypeStruct((rows, LANES), F32)),
        grid=(n_steps,),
        in_specs=in_specs,
        out_specs=(outblk(D_MODEL), outblk(D_MODEL), outblk(LANES), outblk(LANES)),
        input_output_aliases=aliases,
        compiler_params=_cparams(("arbitrary",), 48),
        name="post_attention",
    )(*args)


def _dispatch_kernel(dest_ref, h_hbm, o_ref, tok_ref, buf, sem):
    i = pl.program_id(0)

    @pl.when(i == 0)
    def _():
        def clear(p, carry):
            tok_ref[p] = 0
            return carry
        lax.fori_loop(0, tok_ref.shape[0], clear, 0, unroll=8)

        def place(t, carry):
            for k in range(TOP_K):
                tok_ref[dest_ref[t * TOP_K + k]] = t
            return carry
        lax.fori_loop(0, dest_ref.shape[0] // TOP_K, place, 0, unroll=4)

    def issue(step, slot):
        base = step * DISPATCH_ROWS

        def body(r, carry):
            pltpu.make_async_copy(h_hbm.at[pl.ds(tok_ref[base + r], 1)],
                                  buf.at[slot, pl.ds(r, 1)], sem.at[slot]).start()
            return carry
        lax.fori_loop(0, DISPATCH_ROWS, body, 0, unroll=8)

    @pl.when(i == 0)
    def _():
        issue(0, 0)

    @pl.when(i + 1 < pl.num_programs(0))
    def _():
        issue(i + 1, (i + 1) % 2)

    slot = i % 2
    pltpu.make_async_copy(h_hbm.at[pl.ds(0, DISPATCH_ROWS)], buf.at[slot], sem.at[slot]).wait()
    o_ref[...] = buf[slot].astype(BF16)


def _dispatch(dest, h2, p):
    return pl.pallas_call(
        _dispatch_kernel,
        out_shape=jax.ShapeDtypeStruct((p, D_MODEL), BF16),
        grid_spec=pltpu.PrefetchScalarGridSpec(
            num_scalar_prefetch=1,
            grid=(p // DISPATCH_ROWS,),
            in_specs=[pl.BlockSpec(memory_space=pl.ANY)],
            out_specs=pl.BlockSpec((DISPATCH_ROWS, D_MODEL), lambda i, dst: (i, 0)),
            scratch_shapes=[pltpu.SMEM((p,), I32),
                            pltpu.VMEM((2, DISPATCH_ROWS, D_MODEL), F32), pltpu.SemaphoreType.DMA((2,))],
        ),
        compiler_params=_cparams(("arbitrary",), 32),
        name="moe_dispatch",
    )(dest, h2)


def _moe_kernel(ie_ref, iblk_ref, insub_ref, iact_ref, used_ref,
                x_hbm, wg_ref, wu_ref, wd_ref, bg_ref, bu_ref, bd_ref, y_hbm,
                xbuf, act, ystage, xsem, ysem):
    it = pl.program_id(0)
    s = pl.program_id(1)
    nsub = insub_ref[it]
    row0 = iblk_ref[it] * MOE_SUB

    xslot = it % 2

    def x_copies(item, op):
        first = iblk_ref[item] * MOE_SUB

        def body(j, c):
            cp = pltpu.make_async_copy(x_hbm.at[pl.ds(first + j * MOE_SUB, MOE_SUB)],
                                       xbuf.at[item % 2, pl.ds(j * MOE_SUB, MOE_SUB)], xsem.at[item % 2])
            cp.start() if op == "start" else cp.wait()
            return c
        lax.fori_loop(0, insub_ref[item], body, 0)

    def for_row_chunks(fn):
        big = nsub // 4

        def body(j, c):
            fn(pl.multiple_of(j * (4 * MOE_SUB), 4 * MOE_SUB), 4 * MOE_SUB)
            return c
        lax.fori_loop(0, big, body, 0)
        rem = nsub - 4 * big
        has2 = rem >= 2

        @pl.when(has2)
        def _():
            fn(pl.multiple_of(big * (4 * MOE_SUB), MOE_SUB), 2 * MOE_SUB)

        @pl.when((rem & 1) == 1)
        def _():
            fn(pl.multiple_of((4 * big + jnp.where(has2, 2, 0)) * MOE_SUB, MOE_SUB), MOE_SUB)

    @pl.when((s == 0) & (it == 0))
    def _():
        x_copies(0, "start")

    @pl.when(s == 0)
    def _():
        x_copies(it, "wait")

    @pl.when((s == 1) & (it + 1 < pl.num_programs(0)))
    def _():
        x_copies(it + 1, "start")

    @pl.when((s < MOE_F) & (nsub > 0))
    def _():
        bg, bu = bg_ref[...], bu_ref[...]

        def gate_up(r0, rows):
            x = xbuf[xslot, pl.ds(r0, rows), :]
            g = jnp.dot(x, wg_ref[...].astype(BF16), preferred_element_type=F32) + bg
            u = jnp.dot(x, wu_ref[...].astype(BF16), preferred_element_type=F32) + bu
            g = jnp.minimum(g, SWIGLU_LIMIT)
            u = jnp.clip(u, -SWIGLU_LIMIT, SWIGLU_LIMIT)
            a = g * jax.nn.sigmoid(SWIGLU_ALPHA * g) * (u + 1.0)
            act[s, pl.ds(r0, rows), :] = a.astype(BF16)
        for_row_chunks(gate_up)

    @pl.when((s >= MOE_F) & (nsub > 0))
    def _():
        n = s - MOE_F
        slot = n % 2
        bd = bd_ref[...]

        def y_copy(step, r0, rows):
            c0 = pl.multiple_of(step * MOE_TF, MOE_TF)
            return pltpu.make_async_copy(
                ystage.at[step % 2, pl.ds(r0, rows), :],
                y_hbm.at[pl.ds(row0 + r0, rows), pl.ds(c0, MOE_TF)], ysem.at[step % 2])

        def down(r0, rows):
            a = jnp.concatenate([act[f, pl.ds(r0, rows), :] for f in range(MOE_F)], axis=1)
            ystage[slot, pl.ds(r0, rows), :] = (
                jnp.dot(a, wd_ref[...].astype(BF16), preferred_element_type=F32) + bd)
            y_copy(n, r0, rows).start()
        for_row_chunks(down)

        @pl.when(n > 0)
        def _():
            for_row_chunks(lambda r0, rows: y_copy(n - 1, r0, rows).wait())

        @pl.when(n == MOE_F - 1)
        def _():
            for_row_chunks(lambda r0, rows: y_copy(n, r0, rows).wait())

    @pl.when((it == pl.num_programs(0) - 1) & (s == pl.num_programs(1) - 1))
    def _():
        n_blocks = y_hbm.shape[0] // MOE_SUB
        ystage[0, pl.ds(0, MOE_SUB), :] = jnp.zeros((MOE_SUB, MOE_TF), F32)

        def z_copy(b, n):
            return pltpu.make_async_copy(
                ystage.at[0, pl.ds(0, MOE_SUB), :],
                y_hbm.at[pl.ds(b * MOE_SUB, MOE_SUB), pl.ds(n * MOE_TF, MOE_TF)], ysem.at[0])

        def fill(b, c):
            for n in range(MOE_F):
                z_copy(b, n).start()
            for n in range(MOE_F):
                z_copy(b, n).wait()
            return c
        lax.fori_loop(used_ref[0], n_blocks, fill, 0)


def _moe_experts(items, x_sorted, w_gate_up, b_gate_up, w_down, b_down):
    item_e, item_blk, item_nsub, item_act, used_blocks = items
    n_items = item_e.shape[0]
    p = x_sorted.shape[0]
    rmax = MOE_MAX_SUB * MOE_SUB
    last = MOE_F - 1
    f_of = lambda s, act: jnp.where(act == 1, jnp.minimum(s, last), last)
    n_of = lambda s, act: jnp.where(act == 1, jnp.maximum(s - MOE_F, 0), last)
    wg_map = lambda i, s, ie, ib, ins, ia, iu: (ie[i], 0, f_of(s, ia[i]))
    wu_map = lambda i, s, ie, ib, ins, ia, iu: (ie[i], 0, MOE_F + f_of(s, ia[i]))
    wd_map = lambda i, s, ie, ib, ins, ia, iu: (ie[i], 0, n_of(s, ia[i]))
    bgu = b_gate_up.reshape(N_EXPERTS, 1, 2 * D_FF)
    bd = b_down.reshape(N_EXPERTS, 1, D_MODEL)
    return pl.pallas_call(
        _moe_kernel,
        out_shape=jax.ShapeDtypeStruct((p, D_MODEL), F32),
        grid_spec=pltpu.PrefetchScalarGridSpec(
            num_scalar_prefetch=5,
            grid=(n_items, 2 * MOE_F),
            in_specs=[
                pl.BlockSpec(memory_space=pl.ANY),
                pl.BlockSpec((None, D_MODEL, MOE_TF), wg_map),
                pl.BlockSpec((None, D_MODEL, MOE_TF), wu_map),
                pl.BlockSpec((None, D_FF, MOE_TF), wd_map),
                pl.BlockSpec((None, 1, MOE_TF), wg_map),
                pl.BlockSpec((None, 1, MOE_TF), wu_map),
                pl.BlockSpec((None, 1, MOE_TF), wd_map),
            ],
            out_specs=pl.BlockSpec(memory_space=pl.ANY),
            scratch_shapes=[
                pltpu.VMEM((2, rmax, D_MODEL), BF16),
                pltpu.VMEM((MOE_F, rmax, MOE_TF), BF16),
                pltpu.VMEM((2, rmax, MOE_TF), F32),
                pltpu.SemaphoreType.DMA((2,)),
                pltpu.SemaphoreType.DMA((2,)),
            ],
        ),
        compiler_params=_cparams(("arbitrary", "arbitrary"), 56),
        name="moe_experts",
    )(item_e, item_blk, item_nsub, item_act, used_blocks, x_sorted, w_gate_up, w_gate_up, w_down, bgu, bgu, bd)


def _tail_kernel(n_first, dest_ref, y_hbm, gate_ref, x1_ref, p_ref, wg_ref, wp_ref, gp_ref, gf_ref,
                 o1_ref, o2_ref, ybuf, sem):
    i = pl.program_id(0)

    def issue(step, slot):
        base = step * (TAIL_ROWS * TOP_K)

        def body(r, carry):
            for k in range(TOP_K):
                pltpu.make_async_copy(y_hbm.at[pl.ds(dest_ref[base + r * TOP_K + k], 1)],
                                      ybuf.at[slot, k, pl.ds(r, 1)], sem.at[slot]).start()
            return carry
        lax.fori_loop(0, TAIL_ROWS, body, 0, unroll=2)

    @pl.when(i == 0)
    def _():
        issue(0, 0)

    @pl.when(i + 1 < pl.num_programs(0))
    def _():
        issue(i + 1, (i + 1) % 2)

    slot = i % 2
    for k in range(TOP_K):
        pltpu.make_async_copy(y_hbm.at[pl.ds(0, TAIL_ROWS)], ybuf.at[slot, k], sem.at[slot]).wait()

    gate = gate_ref[...]
    moe = ybuf[slot, 0] * gate[:, 0:1]
    for k in range(1, TOP_K):
        moe = moe + ybuf[slot, k] * gate[:, k:k + 1]
    x2 = x1_ref[...] + moe
    h3 = _rmsnorm(x2, gp_ref[...]).astype(BF16)
    ple_gate = jax.nn.sigmoid(jnp.dot(h3, wg_ref[...], preferred_element_type=F32))
    ple = jnp.dot(p_ref[...].astype(BF16), wp_ref[...], preferred_element_type=F32)
    x3 = x2 + ple_gate * ple
    y = _rmsnorm(x3, gf_ref[...])

    @pl.when(i < n_first)
    def _():
        o1_ref[...] = y

    @pl.when(i >= n_first)
    def _():
        o2_ref[...] = y


def _tail(dest, y_sorted, rgate, x1, p_all, w_ple_gate_bf, w_ple_proj_bf, g_ple, g_final, t_first):
    t = dest.shape[0] // TOP_K
    assert t_first % TAIL_ROWS == 0 and t % TAIL_ROWS == 0
    n_first = t_first // TAIL_ROWS
    rowblk = lambda w: pl.BlockSpec((TAIL_ROWS, w), lambda i, d: (i, 0))
    full = lambda r, c: pl.BlockSpec((r, c), lambda i, d: (0, 0))
    return pl.pallas_call(
        functools.partial(_tail_kernel, n_first),
        out_shape=(jax.ShapeDtypeStruct((t_first, D_MODEL), F32),
                   jax.ShapeDtypeStruct((t - t_first, D_MODEL), F32)),
        grid_spec=pltpu.PrefetchScalarGridSpec(
            num_scalar_prefetch=1,
            grid=(t // TAIL_ROWS,),
            in_specs=[
                pl.BlockSpec(memory_space=pl.ANY),
                rowblk(LANES), rowblk(D_MODEL), rowblk(PLE_DIM),
                full(D_MODEL, D_MODEL), full(PLE_DIM, D_MODEL), full(1, D_MODEL), full(1, D_MODEL)],
            out_specs=(
                pl.BlockSpec((TAIL_ROWS, D_MODEL), lambda i, d: (jnp.minimum(i, n_first - 1), 0)),
                pl.BlockSpec((TAIL_ROWS, D_MODEL), lambda i, d: (jnp.maximum(i - n_first, 0), 0))),
            scratch_shapes=[pltpu.VMEM((2, TOP_K, TAIL_ROWS, D_MODEL), F32), pltpu.SemaphoreType.DMA((2,))],
        ),
        compiler_params=_cparams(("arbitrary",), 48),
        name="moe_tail",
    )(dest, y_sorted, rgate, x1, p_all, w_ple_gate_bf, w_ple_proj_bf, g_ple, g_final)


def _routing(ridx, rgate_unused, t):
    del rgate_unused
    m = t * TOP_K
    flat_e = ridx[:, :TOP_K].reshape(m)
    onehot = (flat_e[:, None] == jnp.arange(N_EXPERTS, dtype=I32)[None, :]).astype(I32)
    csum = jnp.cumsum(onehot, axis=0)
    rank = jnp.sum((csum - onehot) * onehot, axis=1)
    counts = csum[-1]
    nblk = (counts + MOE_SUB - 1) // MOE_SUB
    blk_end = jnp.cumsum(nblk)
    blk_start = blk_end - nblk
    dest = blk_start[flat_e] * MOE_SUB + rank
    n_blocks = -(-m // MOE_SUB) + N_EXPERTS
    n_blocks = -(-n_blocks // (DISPATCH_ROWS // MOE_SUB)) * (DISPATCH_ROWS // MOE_SUB)
    p = n_blocks * MOE_SUB

    n_items = N_EXPERTS + -(-n_blocks // MOE_MAX_SUB)
    items_per_e = (nblk + MOE_MAX_SUB - 1) // MOE_MAX_SUB
    item_end = jnp.cumsum(items_per_e)
    total = item_end[-1]
    ids = jnp.arange(n_items, dtype=I32)
    active = ids < total
    ids_c = jnp.minimum(ids, total - 1)
    item_e = jnp.searchsorted(item_end, ids_c, side="right").astype(I32)
    local = ids_c - (item_end - items_per_e)[item_e]
    item_blk = blk_start[item_e] + local * MOE_MAX_SUB
    item_nsub = jnp.where(active, jnp.minimum(MOE_MAX_SUB, nblk[item_e] - local * MOE_MAX_SUB), 0)
    items = (item_e, item_blk.astype(I32), item_nsub.astype(I32), active.astype(I32),
             blk_end[-1:].astype(I32))
    return p, dest.astype(I32), items


def kernel(x_prompt, x_sample, cache_a_k, cache_a_v, cache_b_k, cache_b_v, p_prompt, p_sample, g_attn, w_in, b_in, sinks, g_out_a, g_out_b, w_out, b_out, g_ffn, w_router, b_router, w_gate_up, b_gate_up, w_down, b_down, g_ple, w_ple_gate, w_ple_proj, g_final):
    batch, seq, _ = x_prompt.shape
    nb, nseq, _ = x_sample.shape
    assert nseq == 1 and g_attn.shape[0] == 1
    tp, ts = batch * seq, nb * nseq
    row = lambda v: v.reshape(1, -1)

    w_in_bf = _cast_bf16(w_in[0])
    w_out_bf = _cast_bf16(w_out[0])
    w_pg_bf = _cast_bf16(w_ple_gate[0])
    w_pp_bf = _cast_bf16(w_ple_proj[0])

    tabs_p = _rope_tables(jnp.arange(seq, dtype=I32))
    tabs_s = _rope_tables(jnp.full((ts,), PAST_LEN, I32))
    sinks_lanes = jnp.repeat(sinks[0], HEAD_DIM).reshape(1, WIDTH)

    tm_p = 512
    xp = x_prompt.reshape(tp, D_MODEL)
    xs = x_sample.reshape(ts, D_MODEL)
    win_a, win_b = min(DILATIONS[-1][0], seq), min(WINDOW_B, seq)
    qkvc_p, kvbc_p, kt_p, vt_p, kvbt_p = _qkv_proj(xp, row(g_attn), w_in_bf, row(b_in), tabs_p, tm_p,
                                                    batch, win_a, win_b)
    qkvc_s, kvbc_s, kt_s, vt_s, kvbt_s = _qkv_proj(xs, row(g_attn), w_in_bf, row(b_in), tabs_s, ts, 1, ts, ts)

    oa_p, ob_p = _attn_prompt(qkvc_p, kvbc_p, sinks_lanes, batch, seq)
    oa_s, ob_s = _attn_sample(qkvc_s, kvbc_s, cache_a_k[0], cache_a_v[0], cache_b_k[0], cache_b_v[0], sinks[0])

    post = functools.partial(_post_attention, g_out_a=row(g_out_a), g_out_b=row(g_out_b), w_out_bf=w_out_bf,
                             b_out=row(b_out), g_ffn=row(g_ffn), w_router=w_router[0], b_router=row(b_router))
    t = tp + ts
    tm_post = 256
    merged = post(oa_p, ob_p, xp, tm=tm_post, out_rows=-(-t // tm_post) * tm_post)
    x1, h2, ridx, rgate = post(oa_s, ob_s, xs, tm=ts, append_to=merged, row0=tp)

    cat = lambda a, b: jnp.concatenate([a, b], axis=0)
    n_slots, dest, items = _routing(ridx[:t], rgate, t)
    x_sorted = _dispatch(dest, h2, n_slots)
    y_sorted = _moe_experts(items, x_sorted, w_gate_up[0], b_gate_up[0], w_down[0], b_down[0])
    p_all = cat(p_prompt[0].reshape(tp, PLE_DIM), p_sample[0].reshape(ts, PLE_DIM))
    y_p, y_s = _tail(dest, y_sorted, rgate, x1, p_all, w_pg_bf, w_pp_bf, row(g_ple), row(g_final), tp)

    y_prompt = y_p.reshape(batch, seq, D_MODEL)
    y_sample = y_s.reshape(nb, nseq, D_MODEL)

    def rows_out(t3, heads):
        n, _, rows = t3.shape
        return t3.reshape(n, heads, HEAD_DIM, rows).transpose(0, 3, 1, 2)[None]

    def sample_out(t3, heads):
        return t3.reshape(heads, HEAD_DIM, nb).transpose(2, 0, 1)[None, :, None]

    return (y_prompt, y_sample,
            rows_out(kt_p, N_HEADS), rows_out(vt_p, N_HEADS),
            rows_out(kvbt_p[:, :KV_WIDTH_B], N_KV_B), rows_out(kvbt_p[:, KV_WIDTH_B:], N_KV_B),
            sample_out(kt_s, N_HEADS), sample_out(vt_s, N_HEADS),
            sample_out(kvbt_s[:, :KV_WIDTH_B], N_KV_B), sample_out(kvbt_s[:, KV_WIDTH_B:], N_KV_B))
```

```python
import functools

import jax
import jax.numpy as jnp
import numpy as np
from jax import lax
from jax.experimental import pallas as pl
from jax.experimental.pallas import tpu as pltpu

F32 = jnp.float32
BF16 = jnp.bfloat16
I32 = jnp.int32

D_MODEL = 2048
HEAD_DIM = 64
N_HEADS = 16
WIDTH = N_HEADS * HEAD_DIM
N_KV_B = 2
KV_WIDTH_B = N_KV_B * HEAD_DIM
DILATIONS = ((128, 1), (512, 4), (2048, 16))
WINDOW_B = 128
BLOCK = 128
N_EXPERTS = 32
TOP_K = 4
D_FF = 2048
SWIGLU_ALPHA = 1.702
SWIGLU_LIMIT = 7.0
PLE_DIM = 256
ROPE_THETA = 10000.0
NORM_EPS = 1e-5
MASK_VALUE = -1e30
SCALE = HEAD_DIM ** -0.5
PAST_LEN = 16384
QKV4_COLS = 4 * WIDTH
IN_COLS = QKV4_COLS + 2 * KV_WIDTH_B

LANES = 128
VMEM_LIMIT_CAP = 60 * 1024 * 1024

MOE_TF = 512
MOE_F = D_FF // MOE_TF
MOE_SUB = 128
MOE_MAX_SUB = 12
DISPATCH_ROWS = 256
TAIL_ROWS = 128


def _cparams(semantics, vmem_mb):
    return pltpu.CompilerParams(
        dimension_semantics=semantics,
        vmem_limit_bytes=min(vmem_mb * 1024 * 1024, VMEM_LIMIT_CAP))


def _rmsnorm(x, g):
    ms = jnp.mean(x * x, axis=-1, keepdims=True)
    return x * lax.rsqrt(ms + NORM_EPS) * g


def _cast_kernel(x_ref, o_ref):
    o_ref[...] = x_ref[...].astype(o_ref.dtype)


def _cast_bf16(w, rows_per_step=256):
    r, c = w.shape
    return pl.pallas_call(
        _cast_kernel,
        out_shape=jax.ShapeDtypeStruct((r, c), BF16),
        grid=(r // rows_per_step,),
        in_specs=[pl.BlockSpec((rows_per_step, c), lambda i: (i, 0))],
        out_specs=pl.BlockSpec((rows_per_step, c), lambda i: (i, 0)),
        compiler_params=_cparams(("parallel",), 32),
        name="cast_bf16",
    )(w)


def _rope(yc, cos, sin_lo, sin_hi):
    return yc * cos + pltpu.roll(yc, LANES - 32, 1) * sin_lo + pltpu.roll(yc, 32, 1) * sin_hi


def _qkv_kernel(tiles_per_seq, first_win_tile, win_b,
                x_ref, g_ref, w_ref, wkv_ref, b_ref, bkv_ref, cos_ref, slo_ref, shi_ref,
                qkv_ref, kvb_ref, kt_ref, vt_ref, kvbt_ref, h_scr):
    i = pl.program_id(0)
    j = pl.program_id(1)
    tile_in_seq = i % tiles_per_seq
    in_window = tile_in_seq >= first_win_tile
    n_chunks = WIDTH // LANES
    chunk = lambda v, c: v[:, c * LANES:(c + 1) * LANES]

    @pl.when(j == 0)
    def _():
        h_scr[...] = _rmsnorm(x_ref[...], g_ref[...]).astype(BF16)

    h = h_scr[...]
    cos, slo, shi = cos_ref[...], slo_ref[...], shi_ref[...]
    is_v = j == 2
    for c2 in range(n_chunks // 2):
        cols = slice(2 * c2 * LANES, 2 * (c2 + 1) * LANES)
        y2 = jnp.dot(h, w_ref[:, cols], preferred_element_type=F32) + b_ref[:, cols]
        for k in range(2):
            yc = chunk(y2, k)
            qkv_ref[2 * c2 + k] = jnp.where(is_v, yc, _rope(yc, cos, slo, shi))

    @pl.when((j == 1) & in_window)
    def _():
        for c in range(n_chunks):
            kt_ref[0, c * LANES:(c + 1) * LANES, :] = qkv_ref[c].T

    @pl.when(is_v & in_window)
    def _():
        for c in range(n_chunks):
            vt_ref[0, c * LANES:(c + 1) * LANES, :] = qkv_ref[c].T

    @pl.when(j == 3)
    def _():
        ykv = jnp.dot(h, wkv_ref[...], preferred_element_type=F32) + bkv_ref[...]
        kvb_ref[0] = _rope(chunk(ykv, 0), cos, slo, shi)
        kvb_ref[1] = chunk(ykv, 1)

    @pl.when((j == 3) & (tile_in_seq == tiles_per_seq - 1))
    def _():
        tm = kvb_ref.shape[1]
        kvbt_ref[0, :KV_WIDTH_B, :] = kvb_ref[0, tm - win_b:, :].T
        kvbt_ref[0, KV_WIDTH_B:, :] = kvb_ref[1, tm - win_b:, :].T


def _qkv_proj(x, g_attn, w_in_bf, b_in, rope_tabs, tm, n_seq, win_a, win_b):
    t = x.shape[0]
    seq = t // n_seq
    tiles_per_seq = seq // tm
    assert seq % tm == 0 and win_a % tm == 0 and win_a <= seq and win_b <= tm
    first_win_tile = tiles_per_seq - win_a // tm
    cos, slo, shi = rope_tabs
    tab_blocks = cos.shape[0] // tm
    tab_map = lambda i, j: (i % tab_blocks, 0)
    win_map = lambda i, j: (i // tiles_per_seq, 0, jnp.maximum(i % tiles_per_seq - first_win_tile, 0))
    return pl.pallas_call(
        functools.partial(_qkv_kernel, tiles_per_seq, first_win_tile, win_b),
        out_shape=(jax.ShapeDtypeStruct((QKV4_COLS // LANES, t, LANES), F32),
                   jax.ShapeDtypeStruct((2, t, LANES), F32),
                   jax.ShapeDtypeStruct((n_seq, WIDTH, win_a), F32),
                   jax.ShapeDtypeStruct((n_seq, WIDTH, win_a), F32),
                   jax.ShapeDtypeStruct((n_seq, 2 * KV_WIDTH_B, win_b), F32)),
        grid=(t // tm, 4),
        in_specs=[
            pl.BlockSpec((tm, D_MODEL), lambda i, j: (i, 0)),
            pl.BlockSpec((1, D_MODEL), lambda i, j: (0, 0)),
            pl.BlockSpec((D_MODEL, WIDTH), lambda i, j: (0, j)),
            pl.BlockSpec((D_MODEL, 2 * KV_WIDTH_B), lambda i, j: (0, QKV4_COLS // (2 * KV_WIDTH_B))),
            pl.BlockSpec((1, WIDTH), lambda i, j: (0, j)),
            pl.BlockSpec((1, 2 * KV_WIDTH_B), lambda i, j: (0, QKV4_COLS // (2 * KV_WIDTH_B))),
            pl.BlockSpec((tm, LANES), tab_map),
            pl.BlockSpec((tm, LANES), tab_map),
            pl.BlockSpec((tm, LANES), tab_map),
        ],
        out_specs=(pl.BlockSpec((WIDTH // LANES, tm, LANES), lambda i, j: (j, i, 0)),
                   pl.BlockSpec((2, tm, LANES), lambda i, j: (0, i, 0)),
                   pl.BlockSpec((1, WIDTH, tm), win_map),
                   pl.BlockSpec((1, WIDTH, tm), win_map),
                   pl.BlockSpec((1, 2 * KV_WIDTH_B, win_b), lambda i, j: (i // tiles_per_seq, 0, 0))),
        scratch_shapes=[pltpu.VMEM((tm, D_MODEL), BF16)],
        compiler_params=_cparams(("arbitrary", "arbitrary"), 48),
        name="qkv_proj",
    )(x, g_attn, w_in_bf, w_in_bf, b_in, b_in, cos, slo, shi)


def _rope_tables(pos):
    half = HEAD_DIM // 2
    inv = ROPE_THETA ** (-jnp.arange(half, dtype=F32) / half)
    ang = pos.astype(F32)[:, None] * inv[None, :]
    cos = jnp.tile(jnp.cos(ang), (1, LANES // half))
    sin = jnp.tile(jnp.sin(ang), (1, LANES // half))
    first = (jnp.arange(LANES) % HEAD_DIM) < half
    return cos, jnp.where(first, -sin, 0.0), jnp.where(first, 0.0, sin)


def _two_head_softmax_pv(q0, q1, k2, v2, valid2):
    qm = jnp.concatenate([q0, q1], axis=0).astype(BF16)
    s = lax.dot_general(qm, k2, (((1,), (1,)), ((), ())), preferred_element_type=F32)
    s = jnp.where(valid2, s, MASK_VALUE)
    m = jnp.max(s, axis=1, keepdims=True)
    p = jnp.exp(s - m)
    l = jnp.sum(p, axis=1, keepdims=True)
    o = jnp.dot(p.astype(BF16), v2, preferred_element_type=F32) / l
    lse = m + jnp.log(l)
    return o[:BLOCK], lse[:BLOCK], o[BLOCK:], lse[BLOCK:]


ATT_SB = BLOCK * max(d for _, d in DILATIONS)
ATT_UNITS = ATT_SB // BLOCK
ATT_UNROLL = 8


ATT_SUBSTEPS = (len(DILATIONS) + 1) * (ATT_UNITS // ATT_UNROLL)
N_SAMPLE_REFS = 12


def _attn_kernel(units_per_step, *refs):
    sample_in, rest = refs[:N_SAMPLE_REFS], refs[N_SAMPLE_REFS:]
    prompt_in, (oas_ref, obs_ref, oa_ref, ob_ref), scratch = rest[:11], rest[11:15], rest[15:]
    for j in range(units_per_step):
        _sample_unit(j, *sample_in, oas_ref, obs_ref)
    _attn_prompt_substep(*prompt_in, oa_ref, ob_ref, *scratch)


def _attn_prompt_substep(qa_ref, ka_ref, kap_ref, va_ref, vap_ref, qb_ref, kb_ref, kbp_ref, vb_ref, vbp_ref,
                         sink_ref, oa_ref, ob_ref, kcat, vcat, kbcat, vbcat, o_scr, lse_scr):
    has_prev = pl.program_id(1) > 0
    pair = pl.program_id(2)
    sub = pl.program_id(3)
    sb = ATT_SB

    @pl.when(sub == 0)
    def _():
        kcat[0:sb, :] = kap_ref[0]
        kcat[sb:2 * sb, :] = ka_ref[0]
        vcat[0:sb, :] = vap_ref[0]
        vcat[sb:2 * sb, :] = va_ref[0]
        kv_first = pair < (N_HEADS // N_KV_B) // 2
        lane_row = lax.broadcasted_iota(I32, (1, LANES), 1)
        kv_lanes = jnp.where(lane_row >= HEAD_DIM, 1, 0) == jnp.where(kv_first, 0, 1)
        both_halves = lambda x: jnp.where(kv_lanes, x, pltpu.roll(x, HEAD_DIM, 1))
        kbcat[0:BLOCK, :] = both_halves(kbp_ref[0])
        kbcat[BLOCK:BLOCK + sb, :] = both_halves(kb_ref[0])
        vbcat[0:BLOCK, :] = both_halves(vbp_ref[0])
        vbcat[BLOCK:BLOCK + sb, :] = both_halves(vb_ref[0])

    lane = lax.broadcasted_iota(I32, (BLOCK, LANES), 1)
    first = lane < HEAD_DIM
    qi = lax.broadcasted_iota(I32, (2 * BLOCK, 2 * BLOCK), 0) & (BLOCK - 1)
    si = lax.broadcasted_iota(I32, (2 * BLOCK, 2 * BLOCK), 1)
    dist = qi - si + BLOCK
    band = (dist >= 0) & (dist <= BLOCK)
    own_block = si >= BLOCK

    def rows(start, size, d):
        return pl.ds(start, size) if d == 1 else pl.ds(start, size, stride=d)

    def key_mask(blk):
        return band if blk > 0 else band & (own_block | has_prev)

    halves = ATT_UNITS // ATT_UNROLL
    for p, (_, d) in enumerate(DILATIONS):
        for half in range(halves):
            @pl.when(sub == p * halves + half)
            def _(p=p, d=d, half=half):
                for u in range(half * ATT_UNROLL, (half + 1) * ATT_UNROLL):
                    blk = u // d
                    qs = blk * (BLOCK * d) + u % d
                    q = qa_ref[0, rows(qs, BLOCK, d), :] * SCALE
                    k2 = kcat[rows(sb + qs - BLOCK * d, 2 * BLOCK, d), :].astype(BF16)
                    v2 = vcat[rows(sb + qs - BLOCK * d, 2 * BLOCK, d), :].astype(BF16)
                    o0, l0, o1, l1 = _two_head_softmax_pv(jnp.where(first, q, 0.0), jnp.where(first, 0.0, q),
                                                          k2, v2, key_mask(blk))
                    o_scr[p, rows(qs, BLOCK, d), :] = jnp.where(first, o0, o1)
                    lse_scr[p, rows(qs, BLOCK, d), :] = jnp.where(first, l0, l1)

    @pl.when(sub == len(DILATIONS) * halves - 1)
    def _():
        def mix(bk, carry):
            r = pl.ds(pl.multiple_of(bk * BLOCK, BLOCK), BLOCK)
            lses = [lse_scr[p, r, :] for p in range(len(DILATIONS))]
            m = functools.reduce(jnp.maximum, lses)
            ws = [jnp.exp(l - m) for l in lses]
            oa_ref[0, r, :] = sum(w * o_scr[p, r, :] for p, w in enumerate(ws)) / sum(ws)
            return carry
        lax.fori_loop(0, ATT_UNITS, mix, 0)

    for half in range(halves):
        @pl.when(sub == len(DILATIONS) * halves + half)
        def _(half=half):
            sink = sink_ref[...]
            for blk in range(half * ATT_UNROLL, (half + 1) * ATT_UNROLL):
                r = pl.ds(blk * BLOCK, BLOCK)
                r2 = pl.ds(blk * BLOCK, 2 * BLOCK)
                q = qb_ref[0, r, :] * SCALE
                k2 = kbcat[r2, :].astype(BF16)
                v2 = vbcat[r2, :].astype(BF16)
                o0, l0, o1, l1 = _two_head_softmax_pv(jnp.where(first, q, 0.0), jnp.where(first, 0.0, q),
                                                      k2, v2, key_mask(blk))
                lse = jnp.where(first, l0, l1)
                ob_ref[0, r, :] = jnp.where(first, o0, o1) * jax.nn.sigmoid(lse - sink)


def _attention(qkvc, kvbc, sinks_lanes, batch, seq, qkvc_s, kvbc_s, cache_a_k, cache_a_v, cache_b_k, cache_b_v, sinks):
    t = batch * seq
    assert seq % ATT_SB == 0
    n_sb = seq // ATT_SB
    n_pairs = WIDTH // LANES
    cur = lambda comp: (lambda b, s, p, q: (comp * n_pairs + p, b * n_sb + s, 0))
    prev = lambda comp: (lambda b, s, p, q: (comp * n_pairs + p, b * n_sb + jnp.maximum(s - 1, 0), 0))
    blk = lambda f: pl.BlockSpec((1, ATT_SB, LANES), f)
    kvb_cur = lambda c: pl.BlockSpec((1, ATT_SB, LANES), lambda b, s, p, q: (c, b * n_sb + s, 0))
    kvb_prev = lambda c: pl.BlockSpec(
        (1, BLOCK, LANES),
        lambda b, s, p, q: (c, jnp.maximum((b * n_sb + s) * ATT_UNITS - 1, b * n_sb * ATT_UNITS), 0))
    out_spec = pl.BlockSpec((1, ATT_SB, LANES), lambda b, s, p, q: (p, b * n_sb + s, 0))
    prompt_specs = [blk(cur(0)), blk(cur(1)), blk(prev(1)), blk(cur(2)), blk(prev(2)), blk(cur(3)),
                    kvb_cur(0), kvb_prev(0), kvb_cur(1), kvb_prev(1),
                    pl.BlockSpec((1, LANES), lambda b, s, p, q: (0, p))]

    nb = qkvc_s.shape[1]
    wa = cache_a_k.shape[1]
    hps = N_HEADS // N_KV_B
    n_units = nb * N_KV_B
    n_steps = batch * n_sb * n_pairs * ATT_SUBSTEPS
    assert n_units % n_steps == 0
    ups = n_units // n_steps
    assert cache_b_k.shape[1] == WINDOW_B and all(wa >= w and wa % d == 0 for w, d in DILATIONS)
    dist = wa - np.arange(wa)
    cnt = sum(((dist % d == 0) & (dist <= w)).astype(np.float32) for w, d in DILATIONS).reshape(1, wa)
    q4u = (qkvc_s.reshape(4, N_KV_B, n_pairs // N_KV_B, nb, LANES // HEAD_DIM, HEAD_DIM)
           .transpose(3, 1, 0, 2, 4, 5).reshape(n_units, 4, hps, HEAD_DIM))
    kvnu = (kvbc_s.reshape(2, nb, N_KV_B, HEAD_DIM).transpose(1, 2, 0, 3).reshape(n_units, 2, 1, HEAD_DIM))
    sink_u = jnp.tile(sinks.reshape(N_KV_B, hps, 1), (nb, 1, 1))
    to_units = lambda c: jnp.transpose(c, (0, 2, 3, 1)).reshape(n_units, -1, HEAD_DIM, c.shape[1])
    step = lambda b, s, p, q: ((b * n_sb + s) * n_pairs + p) * ATT_SUBSTEPS + q
    unit_blk = lambda shape: pl.BlockSpec((ups,) + shape, lambda b, s, p, q: (step(b, s, p, q),) + (0,) * len(shape))
    comp_blk = lambda c, shape: pl.BlockSpec((ups, 1) + shape,
                                             lambda b, s, p, q: (step(b, s, p, q), c) + (0,) * len(shape))
    sample_specs = [pl.BlockSpec((1, wa), lambda b, s, p, q: (0, 0)),
                    comp_blk(0, (hps, HEAD_DIM)), comp_blk(1, (hps, HEAD_DIM)), comp_blk(2, (hps, HEAD_DIM)),
                    comp_blk(3, (hps, HEAD_DIM)), comp_blk(0, (1, HEAD_DIM)), comp_blk(1, (1, HEAD_DIM)),
                    unit_blk((hps, 1)),
                    unit_blk((hps, HEAD_DIM, wa)), unit_blk((hps, HEAD_DIM, wa)),
                    unit_blk((1, HEAD_DIM, WINDOW_B)), unit_blk((1, HEAD_DIM, WINDOW_B))]
    assert len(sample_specs) == N_SAMPLE_REFS
    oas, obs, oa, ob = pl.pallas_call(
        functools.partial(_attn_kernel, ups),
        out_shape=(jax.ShapeDtypeStruct((n_units, hps, HEAD_DIM), F32),
                   jax.ShapeDtypeStruct((n_units, hps, HEAD_DIM), F32),
                   jax.ShapeDtypeStruct((n_pairs, t, LANES), F32),
                   jax.ShapeDtypeStruct((n_pairs, t, LANES), F32)),
        grid=(batch, n_sb, n_pairs, ATT_SUBSTEPS),
        in_specs=sample_specs + prompt_specs,
        out_specs=(unit_blk((hps, HEAD_DIM)), unit_blk((hps, HEAD_DIM)), out_spec, out_spec),
        scratch_shapes=[
            pltpu.VMEM((2 * ATT_SB, LANES), F32), pltpu.VMEM((2 * ATT_SB, LANES), F32),
            pltpu.VMEM((BLOCK + ATT_SB, LANES), F32), pltpu.VMEM((BLOCK + ATT_SB, LANES), F32),
            pltpu.VMEM((len(DILATIONS), ATT_SB, LANES), F32), pltpu.VMEM((len(DILATIONS), ATT_SB, LANES), F32)],
        compiler_params=_cparams(("parallel", "parallel", "arbitrary", "arbitrary"), 56),
        name="attention",
    )(jnp.asarray(cnt), q4u, q4u, q4u, q4u, kvnu, kvnu, sink_u,
      to_units(cache_a_k), to_units(cache_a_v), to_units(cache_b_k), to_units(cache_b_v),
      qkvc, qkvc, qkvc, qkvc, qkvc, qkvc, kvbc, kvbc, kvbc, kvbc, sinks_lanes)
    chunked = lambda o: o.reshape(nb, n_pairs, LANES).transpose(1, 0, 2)
    return oa, ob, chunked(oas), chunked(obs)


def _hi_dot(a, b, dims=(((1,), (0,)), ((), ()))):
    return lax.dot_general(a, b, dims, preferred_element_type=F32, precision=lax.Precision.HIGHEST)


def _sample_unit(j, cnt_ref, qa_ref, kn_ref, vn_ref, qb_ref, kbn_ref, vbn_ref, sink_ref,
                 kt_ref, vt_ref, kbt_ref, vbt_ref, oa_ref, ob_ref):
    hps = N_HEADS // N_KV_B
    nt = (((1,), (1,)), ((), ()))
    head = lax.broadcasted_iota(I32, (hps, 1), 0)
    cnt = cnt_ref[...]
    n_pat = float(len(DILATIONS))

    qa = qa_ref[j, 0] * SCALE
    qa_bf = qa.astype(BF16)
    s = jnp.zeros((hps, cnt.shape[1]), F32)
    for hl in range(hps):
        s_hl = jnp.dot(qa_bf, kt_ref[j, hl].astype(BF16), preferred_element_type=F32)
        s = jnp.where(head == hl, s_hl, s)
    s = jnp.where(cnt > 0.0, s, MASK_VALUE)
    s_new = jnp.sum(qa * kn_ref[j, 0], axis=1, keepdims=True)
    m = jnp.maximum(jnp.max(s, axis=1, keepdims=True), s_new)
    e = cnt * jnp.exp(s - m)
    e_new = n_pat * jnp.exp(s_new - m)
    den = jnp.sum(e, axis=1, keepdims=True) + e_new
    e_bf = e.astype(BF16)
    pv = jnp.zeros((hps, HEAD_DIM), F32)
    for hl in range(hps):
        o_hl = lax.dot_general(e_bf, vt_ref[j, hl].astype(BF16), nt, preferred_element_type=F32)
        pv = jnp.where(head == hl, o_hl, pv)
    oa_ref[j] = (pv + e_new * vn_ref[j, 0]) / den

    qb = qb_ref[j, 0] * SCALE
    sb = _hi_dot(qb, kbt_ref[j, 0])
    sn = jnp.sum(qb * kbn_ref[j, 0], axis=1, keepdims=True)
    mm = jnp.maximum(jnp.max(sb, axis=1, keepdims=True), sn)
    eb = jnp.exp(sb - mm)
    en = jnp.exp(sn - mm)
    l = jnp.sum(eb, axis=1, keepdims=True) + en
    ob = (_hi_dot(eb, vbt_ref[j, 0], nt) + en * vbn_ref[j, 0]) / l
    lse = mm + jnp.log(l)
    ob_ref[j] = ob * jax.nn.sigmoid(lse - sink_ref[j])


def _split_bf16(x):
    hi = x.astype(BF16)
    lo = (x - hi.astype(F32)).astype(BF16)
    return hi, lo


def _post_first_kernel(n_valid, *refs):
    i = pl.program_id(0)

    @pl.when(i < n_valid)
    def _():
        _post_body(*refs)

    @pl.when(i >= n_valid)
    def _():
        for o_ref in refs[-4:]:
            o_ref[...] = jnp.zeros(o_ref.shape, o_ref.dtype)


def _post_append_kernel(x1_all, h2_all, ridx_all, rgate_all, *refs):
    del x1_all, h2_all, ridx_all, rgate_all
    _post_body(*refs)


def _post_body(oa_ref, ob_ref, x_ref, ga_ref, gb_ref, w_ref, b_ref, gf_ref, wr_ref, br_ref,
               x1_ref, h2_ref, ridx_ref, rgate_ref):
    unchunk = lambda ref: jnp.concatenate([ref[c] for c in range(ref.shape[0])], axis=1)
    na = _rmsnorm(unchunk(oa_ref), ga_ref[...]).astype(BF16)
    nb = _rmsnorm(unchunk(ob_ref), gb_ref[...]).astype(BF16)
    c = jnp.concatenate([na, nb], axis=1)
    x1 = x_ref[...] + jnp.dot(c, w_ref[...], preferred_element_type=F32) + b_ref[...]
    x1_ref[...] = x1
    h2 = _rmsnorm(x1, gf_ref[...])
    h2_ref[...] = h2

    h_hi, h_lo = _split_bf16(h2)
    w_hi, w_lo = _split_bf16(wr_ref[...])
    dot = lambda a, b: jnp.dot(a, b, preferred_element_type=F32)
    logits = dot(h_hi, w_hi) + (dot(h_hi, w_lo) + dot(h_lo, w_hi)) + br_ref[...]

    tm = logits.shape[0]
    eidx = lax.broadcasted_iota(I32, (tm, N_EXPERTS), 1)
    lane = lax.broadcasted_iota(I32, (tm, LANES), 1)
    work = logits
    vals, idxs = [], []
    for _ in range(TOP_K):
        v = jnp.max(work, axis=1, keepdims=True)
        i = jnp.min(jnp.where(work == v, eidx, N_EXPERTS), axis=1, keepdims=True)
        vals.append(v)
        idxs.append(i)
        work = jnp.where(eidx == i, -jnp.inf, work)
    es = [jnp.exp(v - vals[0]) for v in vals]
    den = sum(es)
    ridx = jnp.zeros((tm, LANES), I32)
    rgate = jnp.zeros((tm, LANES), F32)
    for k in range(TOP_K):
        ridx = jnp.where(lane == k, idxs[k], ridx)
        rgate = jnp.where(lane == k, es[k] / den, rgate)
    ridx_ref[...] = ridx
    rgate_ref[...] = rgate


def _post_attention(oa, ob, x, g_out_a, g_out_b, w_out_bf, b_out, g_ffn, w_router, b_router, tm,
                    out_rows=None, append_to=None, row0=0):
    t = x.shape[0]
    n_valid = t // tm
    n_pairs = WIDTH // LANES
    clamp = lambda i: jnp.minimum(i, n_valid - 1)
    chunked = pl.BlockSpec((n_pairs, tm, LANES), lambda i: (0, clamp(i), 0))
    full = lambda r, c: pl.BlockSpec((r, c), lambda i: (0, 0))
    in_specs = [chunked, chunked, pl.BlockSpec((tm, D_MODEL), lambda i: (clamp(i), 0)),
                full(1, WIDTH), full(1, WIDTH), full(D_MODEL, D_MODEL), full(1, D_MODEL), full(1, D_MODEL),
                full(D_MODEL, N_EXPERTS), full(1, N_EXPERTS)]
    args = (oa, ob, x, g_out_a, g_out_b, w_out_bf, b_out, g_ffn, w_router, b_router)
    if append_to is None:
        rows = t if out_rows is None else out_rows
        assert rows % tm == 0 and t % tm == 0
        kernel_fn, aliases, blk0 = functools.partial(_post_first_kernel, n_valid), {}, 0
    else:
        rows = append_to[0].shape[0]
        assert row0 % tm == 0 and t % tm == 0
        kernel_fn, aliases, blk0 = _post_append_kernel, {k: k for k in range(4)}, row0 // tm
        in_specs = [pl.BlockSpec(memory_space=pl.ANY)] * 4 + in_specs
        args = tuple(append_to) + args
    n_steps = rows // tm if append_to is None else n_valid
    outblk = lambda w: pl.BlockSpec((tm, w), lambda i: (blk0 + i, 0))
    return pl.pallas_call(
        kernel_fn,
        out_shape=(jax.ShapeDtypeStruct((rows, D_MODEL), F32), jax.ShapeDtypeStruct((rows, D_MODEL), F32),
                   jax.ShapeDtypeStruct((rows, LANES), I32), jax.ShapeDtypeStruct((rows, LANES), F32)),
        grid=(n_steps,),
        in_specs=in_specs,
        out_specs=(outblk(D_MODEL), outblk(D_MODEL), outblk(LANES), outblk(LANES)),
        input_output_aliases=aliases,
        compiler_params=_cparams(("arbitrary",), 48),
        name="post_attention",
    )(*args)


def _dispatch_kernel(dest_ref, h_hbm, o_ref, tok_ref, buf, sem):
    i = pl.program_id(0)

    @pl.when(i == 0)
    def _():
        def clear(p, carry):
            tok_ref[p] = 0
            return carry
        lax.fori_loop(0, tok_ref.shape[0], clear, 0, unroll=8)

        def place(t, carry):
            for k in range(TOP_K):
                tok_ref[dest_ref[t * TOP_K + k]] = t
            return carry
        lax.fori_loop(0, dest_ref.shape[0] // TOP_K, place, 0, unroll=4)

    def issue(step, slot):
        base = step * DISPATCH_ROWS

        def body(r, carry):
            pltpu.make_async_copy(h_hbm.at[pl.ds(tok_ref[base + r], 1)],
                                  buf.at[slot, pl.ds(r, 1)], sem.at[slot]).start()
            return carry
        lax.fori_loop(0, DISPATCH_ROWS, body, 0, unroll=8)

    @pl.when(i == 0)
    def _():
        issue(0, 0)

    @pl.when(i + 1 < pl.num_programs(0))
    def _():
        issue(i + 1, (i + 1) % 2)

    slot = i % 2
    pltpu.make_async_copy(h_hbm.at[pl.ds(0, DISPATCH_ROWS)], buf.at[slot], sem.at[slot]).wait()
    o_ref[...] = buf[slot].astype(BF16)


def _dispatch(dest, h2, p):
    return pl.pallas_call(
        _dispatch_kernel,
        out_shape=jax.ShapeDtypeStruct((p, D_MODEL), BF16),
        grid_spec=pltpu.PrefetchScalarGridSpec(
            num_scalar_prefetch=1,
            grid=(p // DISPATCH_ROWS,),
            in_specs=[pl.BlockSpec(memory_space=pl.ANY)],
            out_specs=pl.BlockSpec((DISPATCH_ROWS, D_MODEL), lambda i, dst: (i, 0)),
            scratch_shapes=[pltpu.SMEM((p,), I32),
                            pltpu.VMEM((2, DISPATCH_ROWS, D_MODEL), F32), pltpu.SemaphoreType.DMA((2,))],
        ),
        compiler_params=_cparams(("arbitrary",), 32),
        name="moe_dispatch",
    )(dest, h2)


def _moe_kernel(ie_ref, iblk_ref, insub_ref, iact_ref, used_ref,
                x_hbm, wg_ref, wu_ref, wd_ref, bg_ref, bu_ref, bd_ref, y_hbm,
                xbuf, act, ystage, xsem, ysem):
    it = pl.program_id(0)
    s = pl.program_id(1)
    nsub = insub_ref[it]
    row0 = iblk_ref[it] * MOE_SUB

    xslot = it % 2

    def x_copies(item, op):
        first = iblk_ref[item] * MOE_SUB

        def body(j, c):
            cp = pltpu.make_async_copy(x_hbm.at[pl.ds(first + j * MOE_SUB, MOE_SUB)],
                                       xbuf.at[item % 2, pl.ds(j * MOE_SUB, MOE_SUB)], xsem.at[item % 2])
            cp.start() if op == "start" else cp.wait()
            return c
        lax.fori_loop(0, insub_ref[item], body, 0)

    def for_row_chunks(fn):
        big = nsub // 4

        def body(j, c):
            fn(pl.multiple_of(j * (4 * MOE_SUB), 4 * MOE_SUB), 4 * MOE_SUB)
            return c
        lax.fori_loop(0, big, body, 0)
        rem = nsub - 4 * big
        has2 = rem >= 2

        @pl.when(has2)
        def _():
            fn(pl.multiple_of(big * (4 * MOE_SUB), MOE_SUB), 2 * MOE_SUB)

        @pl.when((rem & 1) == 1)
        def _():
            fn(pl.multiple_of((4 * big + jnp.where(has2, 2, 0)) * MOE_SUB, MOE_SUB), MOE_SUB)

    @pl.when((s == 0) & (it == 0))
    def _():
        x_copies(0, "start")

    @pl.when(s == 0)
    def _():
        x_copies(it, "wait")

    @pl.when((s == 1) & (it + 1 < pl.num_programs(0)))
    def _():
        x_copies(it + 1, "start")

    @pl.when((s < MOE_F) & (nsub > 0))
    def _():
        bg, bu = bg_ref[...], bu_ref[...]

        def gate_up(r0, rows):
            x = xbuf[xslot, pl.ds(r0, rows), :]
            g = jnp.dot(x, wg_ref[...].astype(BF16), preferred_element_type=F32) + bg
            u = jnp.dot(x, wu_ref[...].astype(BF16), preferred_element_type=F32) + bu
            g = jnp.minimum(g, SWIGLU_LIMIT)
            u = jnp.clip(u, -SWIGLU_LIMIT, SWIGLU_LIMIT)
            a = g * jax.nn.sigmoid(SWIGLU_ALPHA * g) * (u + 1.0)
            act[s, pl.ds(r0, rows), :] = a.astype(BF16)
        for_row_chunks(gate_up)

    @pl.when((s >= MOE_F) & (nsub > 0))
    def _():
        n = s - MOE_F
        slot = n % 2
        bd = bd_ref[...]

        def y_copy(step, r0, rows):
            c0 = pl.multiple_of(step * MOE_TF, MOE_TF)
            return pltpu.make_async_copy(
                ystage.at[step % 2, pl.ds(r0, rows), :],
                y_hbm.at[pl.ds(row0 + r0, rows), pl.ds(c0, MOE_TF)], ysem.at[step % 2])

        def down(r0, rows):
            a = jnp.concatenate([act[f, pl.ds(r0, rows), :] for f in range(MOE_F)], axis=1)
            ystage[slot, pl.ds(r0, rows), :] = (
                jnp.dot(a, wd_ref[...].astype(BF16), preferred_element_type=F32) + bd)
            y_copy(n, r0, rows).start()
        for_row_chunks(down)

        @pl.when(n > 0)
        def _():
            for_row_chunks(lambda r0, rows: y_copy(n - 1, r0, rows).wait())

        @pl.when(n == MOE_F - 1)
        def _():
            for_row_chunks(lambda r0, rows: y_copy(n, r0, rows).wait())

    @pl.when((it == pl.num_programs(0) - 1) & (s == pl.num_programs(1) - 1))
    def _():
        n_blocks = y_hbm.shape[0] // MOE_SUB
        ystage[0, pl.ds(0, MOE_SUB), :] = jnp.zeros((MOE_SUB, MOE_TF), F32)

        def z_copy(b, n):
            return pltpu.make_async_copy(
                ystage.at[0, pl.ds(0, MOE_SUB), :],
                y_hbm.at[pl.ds(b * MOE_SUB, MOE_SUB), pl.ds(n * MOE_TF, MOE_TF)], ysem.at[0])

        def fill(b, c):
            for n in range(MOE_F):
                z_copy(b, n).start()
            for n in range(MOE_F):
                z_copy(b, n).wait()
            return c
        lax.fori_loop(used_ref[0], n_blocks, fill, 0)


def _moe_experts(items, x_sorted, w_gate_up, b_gate_up, w_down, b_down):
    item_e, item_blk, item_nsub, item_act, used_blocks = items
    n_items = item_e.shape[0]
    p = x_sorted.shape[0]
    rmax = MOE_MAX_SUB * MOE_SUB
    last = MOE_F - 1
    f_of = lambda s, act: jnp.where(act == 1, jnp.minimum(s, last), last)
    n_of = lambda s, act: jnp.where(act == 1, jnp.maximum(s - MOE_F, 0), last)
    wg_map = lambda i, s, ie, ib, ins, ia, iu: (ie[i], 0, f_of(s, ia[i]))
    wu_map = lambda i, s, ie, ib, ins, ia, iu: (ie[i], 0, MOE_F + f_of(s, ia[i]))
    wd_map = lambda i, s, ie, ib, ins, ia, iu: (ie[i], 0, n_of(s, ia[i]))
    bgu = b_gate_up.reshape(N_EXPERTS, 1, 2 * D_FF)
    bd = b_down.reshape(N_EXPERTS, 1, D_MODEL)
    return pl.pallas_call(
        _moe_kernel,
        out_shape=jax.ShapeDtypeStruct((p, D_MODEL), F32),
        grid_spec=pltpu.PrefetchScalarGridSpec(
            num_scalar_prefetch=5,
            grid=(n_items, 2 * MOE_F),
            in_specs=[
                pl.BlockSpec(memory_space=pl.ANY),
                pl.BlockSpec((None, D_MODEL, MOE_TF), wg_map),
                pl.BlockSpec((None, D_MODEL, MOE_TF), wu_map),
                pl.BlockSpec((None, D_FF, MOE_TF), wd_map),
                pl.BlockSpec((None, 1, MOE_TF), wg_map),
                pl.BlockSpec((None, 1, MOE_TF), wu_map),
                pl.BlockSpec((None, 1, MOE_TF), wd_map),
            ],
            out_specs=pl.BlockSpec(memory_space=pl.ANY),
            scratch_shapes=[
                pltpu.VMEM((2, rmax, D_MODEL), BF16),
                pltpu.VMEM((MOE_F, rmax, MOE_TF), BF16),
                pltpu.VMEM((2, rmax, MOE_TF), F32),
                pltpu.SemaphoreType.DMA((2,)),
                pltpu.SemaphoreType.DMA((2,)),
            ],
        ),
        compiler_params=_cparams(("arbitrary", "arbitrary"), 56),
        name="moe_experts",
    )(item_e, item_blk, item_nsub, item_act, used_blocks, x_sorted, w_gate_up, w_gate_up, w_down, bgu, bgu, bd)


def _tail_kernel(n_first, dest_ref, y_hbm, gate_ref, x1_ref, p_ref, wg_ref, wp_ref, gp_ref, gf_ref,
                 o1_ref, o2_ref, ybuf, sem):
    i = pl.program_id(0)

    def issue(step, slot):
        base = step * (TAIL_ROWS * TOP_K)

        def body(r, carry):
            for k in range(TOP_K):
                pltpu.make_async_copy(y_hbm.at[pl.ds(dest_ref[base + r * TOP_K + k], 1)],
                                      ybuf.at[slot, k, pl.ds(r, 1)], sem.at[slot]).start()
            return carry
        lax.fori_loop(0, TAIL_ROWS, body, 0, unroll=2)

    @pl.when(i == 0)
    def _():
        issue(0, 0)

    @pl.when(i + 1 < pl.num_programs(0))
    def _():
        issue(i + 1, (i + 1) % 2)

    slot = i % 2
    for k in range(TOP_K):
        pltpu.make_async_copy(y_hbm.at[pl.ds(0, TAIL_ROWS)], ybuf.at[slot, k], sem.at[slot]).wait()

    gate = gate_ref[...]
    moe = ybuf[slot, 0] * gate[:, 0:1]
    for k in range(1, TOP_K):
        moe = moe + ybuf[slot, k] * gate[:, k:k + 1]
    x2 = x1_ref[...] + moe
    h3 = _rmsnorm(x2, gp_ref[...]).astype(BF16)
    ple_gate = jax.nn.sigmoid(jnp.dot(h3, wg_ref[...], preferred_element_type=F32))
    ple = jnp.dot(p_ref[...].astype(BF16), wp_ref[...], preferred_element_type=F32)
    x3 = x2 + ple_gate * ple
    y = _rmsnorm(x3, gf_ref[...])

    @pl.when(i < n_first)
    def _():
        o1_ref[...] = y

    @pl.when(i >= n_first)
    def _():
        o2_ref[...] = y


def _tail(dest, y_sorted, rgate, x1, p_all, w_ple_gate_bf, w_ple_proj_bf, g_ple, g_final, t_first):
    t = dest.shape[0] // TOP_K
    assert t_first % TAIL_ROWS == 0 and t % TAIL_ROWS == 0
    n_first = t_first // TAIL_ROWS
    rowblk = lambda w: pl.BlockSpec((TAIL_ROWS, w), lambda i, d: (i, 0))
    full = lambda r, c: pl.BlockSpec((r, c), lambda i, d: (0, 0))
    return pl.pallas_call(
        functools.partial(_tail_kernel, n_first),
        out_shape=(jax.ShapeDtypeStruct((t_first, D_MODEL), F32),
                   jax.ShapeDtypeStruct((t - t_first, D_MODEL), F32)),
        grid_spec=pltpu.PrefetchScalarGridSpec(
            num_scalar_prefetch=1,
            grid=(t // TAIL_ROWS,),
            in_specs=[
                pl.BlockSpec(memory_space=pl.ANY),
                rowblk(LANES), rowblk(D_MODEL), rowblk(PLE_DIM),
                full(D_MODEL, D_MODEL), full(PLE_DIM, D_MODEL), full(1, D_MODEL), full(1, D_MODEL)],
            out_specs=(
                pl.BlockSpec((TAIL_ROWS, D_MODEL), lambda i, d: (jnp.minimum(i, n_first - 1), 0)),
                pl.BlockSpec((TAIL_ROWS, D_MODEL), lambda i, d: (jnp.maximum(i - n_first, 0), 0))),
            scratch_shapes=[pltpu.VMEM((2, TOP_K, TAIL_ROWS, D_MODEL), F32), pltpu.SemaphoreType.DMA((2,))],
        ),
        compiler_params=_cparams(("arbitrary",), 48),
        name="moe_tail",
    )(dest, y_sorted, rgate, x1, p_all, w_ple_gate_bf, w_ple_proj_bf, g_ple, g_final)


def _routing(ridx, rgate_unused, t):
    del rgate_unused
    m = t * TOP_K
    flat_e = ridx[:, :TOP_K].reshape(m)
    onehot = (flat_e[:, None] == jnp.arange(N_EXPERTS, dtype=I32)[None, :]).astype(I32)
    csum = jnp.cumsum(onehot, axis=0)
    rank = jnp.sum((csum - onehot) * onehot, axis=1)
    counts = csum[-1]
    nblk = (counts + MOE_SUB - 1) // MOE_SUB
    blk_end = jnp.cumsum(nblk)
    blk_start = blk_end - nblk
    dest = blk_start[flat_e] * MOE_SUB + rank
    n_blocks = -(-m // MOE_SUB) + N_EXPERTS
    n_blocks = -(-n_blocks // (DISPATCH_ROWS // MOE_SUB)) * (DISPATCH_ROWS // MOE_SUB)
    p = n_blocks * MOE_SUB

    n_items = N_EXPERTS + -(-n_blocks // MOE_MAX_SUB)
    items_per_e = (nblk + MOE_MAX_SUB - 1) // MOE_MAX_SUB
    item_end = jnp.cumsum(items_per_e)
    total = item_end[-1]
    ids = jnp.arange(n_items, dtype=I32)
    active = ids < total
    ids_c = jnp.minimum(ids, total - 1)
    item_e = jnp.searchsorted(item_end, ids_c, side="right").astype(I32)
    local = ids_c - (item_end - items_per_e)[item_e]
    item_blk = blk_start[item_e] + local * MOE_MAX_SUB
    item_nsub = jnp.where(active, jnp.minimum(MOE_MAX_SUB, nblk[item_e] - local * MOE_MAX_SUB), 0)
    items = (item_e, item_blk.astype(I32), item_nsub.astype(I32), active.astype(I32),
             blk_end[-1:].astype(I32))
    return p, dest.astype(I32), items


def kernel(x_prompt, x_sample, cache_a_k, cache_a_v, cache_b_k, cache_b_v, p_prompt, p_sample, g_attn, w_in, b_in, sinks, g_out_a, g_out_b, w_out, b_out, g_ffn, w_router, b_router, w_gate_up, b_gate_up, w_down, b_down, g_ple, w_ple_gate, w_ple_proj, g_final):
    batch, seq, _ = x_prompt.shape
    nb, nseq, _ = x_sample.shape
    assert nseq == 1 and g_attn.shape[0] == 1
    tp, ts = batch * seq, nb * nseq
    row = lambda v: v.reshape(1, -1)

    w_in_bf = _cast_bf16(w_in[0])
    w_out_bf = _cast_bf16(w_out[0])
    w_pg_bf = _cast_bf16(w_ple_gate[0])
    w_pp_bf = _cast_bf16(w_ple_proj[0])

    tabs_p = _rope_tables(jnp.arange(seq, dtype=I32))
    tabs_s = _rope_tables(jnp.full((ts,), PAST_LEN, I32))
    sinks_lanes = jnp.repeat(sinks[0], HEAD_DIM).reshape(1, WIDTH)

    tm_p = 512
    xp = x_prompt.reshape(tp, D_MODEL)
    xs = x_sample.reshape(ts, D_MODEL)
    win_a, win_b = min(DILATIONS[-1][0], seq), min(WINDOW_B, seq)
    qkvc_p, kvbc_p, kt_p, vt_p, kvbt_p = _qkv_proj(xp, row(g_attn), w_in_bf, row(b_in), tabs_p, tm_p,
                                                    batch, win_a, win_b)
    qkvc_s, kvbc_s, kt_s, vt_s, kvbt_s = _qkv_proj(xs, row(g_attn), w_in_bf, row(b_in), tabs_s, ts, 1, ts, ts)

    oa_p, ob_p, oa_s, ob_s = _attention(qkvc_p, kvbc_p, sinks_lanes, batch, seq, qkvc_s, kvbc_s,
                                        cache_a_k[0], cache_a_v[0], cache_b_k[0], cache_b_v[0], sinks[0])

    post = functools.partial(_post_attention, g_out_a=row(g_out_a), g_out_b=row(g_out_b), w_out_bf=w_out_bf,
                             b_out=row(b_out), g_ffn=row(g_ffn), w_router=w_router[0], b_router=row(b_router))
    t = tp + ts
    tm_post = 256
    merged = post(oa_p, ob_p, xp, tm=tm_post, out_rows=-(-t // tm_post) * tm_post)
    x1, h2, ridx, rgate = post(oa_s, ob_s, xs, tm=ts, append_to=merged, row0=tp)

    cat = lambda a, b: jnp.concatenate([a, b], axis=0)
    n_slots, dest, items = _routing(ridx[:t], rgate, t)
    x_sorted = _dispatch(dest, h2, n_slots)
    y_sorted = _moe_experts(items, x_sorted, w_gate_up[0], b_gate_up[0], w_down[0], b_down[0])
    p_all = cat(p_prompt[0].reshape(tp, PLE_DIM), p_sample[0].reshape(ts, PLE_DIM))
    y_p, y_s = _tail(dest, y_sorted, rgate, x1, p_all, w_pg_bf, w_pp_bf, row(g_ple), row(g_final), tp)

    y_prompt = y_p.reshape(batch, seq, D_MODEL)
    y_sample = y_s.reshape(nb, nseq, D_MODEL)

    def rows_out(t3, heads):
        n, _, rows = t3.shape
        return t3.reshape(n, heads, HEAD_DIM, rows).transpose(0, 3, 1, 2)[None]

    def sample_out(t3, heads):
        return t3.reshape(heads, HEAD_DIM, nb).transpose(2, 0, 1)[None, :, None]

    return (y_prompt, y_sample,
            rows_out(kt_p, N_HEADS), rows_out(vt_p, N_HEADS),
            rows_out(kvbt_p[:, :KV_WIDTH_B], N_KV_B), rows_out(kvbt_p[:, KV_WIDTH_B:], N_KV_B),
            sample_out(kt_s, N_HEADS), sample_out(vt_s, N_HEADS),
            sample_out(kvbt_s[:, :KV_WIDTH_B], N_KV_B), sample_out(kvbt_s[:, KV_WIDTH_B:], N_KV_B))
```

```python
import functools

import jax
import jax.numpy as jnp
import numpy as np
from jax import lax
from jax.experimental import pallas as pl
from jax.experimental.pallas import tpu as pltpu

F32 = jnp.float32
BF16 = jnp.bfloat16
I32 = jnp.int32

D_MODEL = 2048
HEAD_DIM = 64
N_HEADS = 16
WIDTH = N_HEADS * HEAD_DIM
N_KV_B = 2
KV_WIDTH_B = N_KV_B * HEAD_DIM
DILATIONS = ((128, 1), (512, 4), (2048, 16))
WINDOW_B = 128
BLOCK = 128
N_EXPERTS = 32
TOP_K = 4
D_FF = 2048
SWIGLU_ALPHA = 1.702
SWIGLU_LIMIT = 7.0
PLE_DIM = 256
ROPE_THETA = 10000.0
NORM_EPS = 1e-5
MASK_VALUE = -1e30
SCALE = HEAD_DIM ** -0.5
PAST_LEN = 16384
QKV4_COLS = 4 * WIDTH
IN_COLS = QKV4_COLS + 2 * KV_WIDTH_B

LANES = 128
VMEM_LIMIT_CAP = 60 * 1024 * 1024

MOE_TF = 512
MOE_F = D_FF // MOE_TF
MOE_SUB = 128
MOE_MAX_SUB = 12
DISPATCH_ROWS = 256
TAIL_ROWS = 128


def _cparams(semantics, vmem_mb):
    return pltpu.CompilerParams(
        dimension_semantics=semantics,
        vmem_limit_bytes=min(vmem_mb * 1024 * 1024, VMEM_LIMIT_CAP))


def _rmsnorm(x, g):
    ms = jnp.mean(x * x, axis=-1, keepdims=True)
    return x * lax.rsqrt(ms + NORM_EPS) * g


def _cast_kernel(x_ref, o_ref):
    o_ref[...] = x_ref[...].astype(o_ref.dtype)


def _cast_bf16(w, rows_per_step=256):
    r, c = w.shape
    return pl.pallas_call(
        _cast_kernel,
        out_shape=jax.ShapeDtypeStruct((r, c), BF16),
        grid=(r // rows_per_step,),
        in_specs=[pl.BlockSpec((rows_per_step, c), lambda i: (i, 0))],
        out_specs=pl.BlockSpec((rows_per_step, c), lambda i: (i, 0)),
        compiler_params=_cparams(("parallel",), 32),
        name="cast_bf16",
    )(w)


def _rope(yc, cos, sin_lo, sin_hi):
    return yc * cos + pltpu.roll(yc, LANES - 32, 1) * sin_lo + pltpu.roll(yc, 32, 1) * sin_hi


def _qkv_kernel(tiles_per_seq, first_win_tile, win_b,
                x_ref, g_ref, w_ref, wkv_ref, b_ref, bkv_ref, cos_ref, slo_ref, shi_ref,
                qkv_ref, kvb_ref, kt_ref, vt_ref, kvbt_ref, h_scr):
    i = pl.program_id(0)
    j = pl.program_id(1)
    tile_in_seq = i % tiles_per_seq
    in_window = tile_in_seq >= first_win_tile
    n_chunks = WIDTH // LANES
    chunk = lambda v, c: v[:, c * LANES:(c + 1) * LANES]

    @pl.when(j == 0)
    def _():
        h_scr[...] = _rmsnorm(x_ref[...], g_ref[...]).astype(BF16)

    h = h_scr[...]
    cos, slo, shi = cos_ref[...], slo_ref[...], shi_ref[...]
    is_v = j == 2
    for c2 in range(n_chunks // 2):
        cols = slice(2 * c2 * LANES, 2 * (c2 + 1) * LANES)
        y2 = jnp.dot(h, w_ref[:, cols], preferred_element_type=F32) + b_ref[:, cols]
        for k in range(2):
            yc = chunk(y2, k)
            qkv_ref[2 * c2 + k] = jnp.where(is_v, yc, _rope(yc, cos, slo, shi))

    @pl.when((j == 1) & in_window)
    def _():
        for c in range(n_chunks):
            kt_ref[0, c * LANES:(c + 1) * LANES, :] = qkv_ref[c].T

    @pl.when(is_v & in_window)
    def _():
        for c in range(n_chunks):
            vt_ref[0, c * LANES:(c + 1) * LANES, :] = qkv_ref[c].T

    @pl.when(j == 3)
    def _():
        ykv = jnp.dot(h, wkv_ref[...], preferred_element_type=F32) + bkv_ref[...]
        kvb_ref[0] = _rope(chunk(ykv, 0), cos, slo, shi)
        kvb_ref[1] = chunk(ykv, 1)

    @pl.when((j == 3) & (tile_in_seq == tiles_per_seq - 1))
    def _():
        tm = kvb_ref.shape[1]
        kvbt_ref[0, :KV_WIDTH_B, :] = kvb_ref[0, tm - win_b:, :].T
        kvbt_ref[0, KV_WIDTH_B:, :] = kvb_ref[1, tm - win_b:, :].T


def _qkv_proj(x, g_attn, w_in_bf, b_in, rope_tabs, tm, n_seq, win_a, win_b):
    t = x.shape[0]
    seq = t // n_seq
    tiles_per_seq = seq // tm
    assert seq % tm == 0 and win_a % tm == 0 and win_a <= seq and win_b <= tm
    first_win_tile = tiles_per_seq - win_a // tm
    cos, slo, shi = rope_tabs
    tab_blocks = cos.shape[0] // tm
    tab_map = lambda i, j: (i % tab_blocks, 0)
    win_map = lambda i, j: (i // tiles_per_seq, 0, jnp.maximum(i % tiles_per_seq - first_win_tile, 0))
    return pl.pallas_call(
        functools.partial(_qkv_kernel, tiles_per_seq, first_win_tile, win_b),
        out_shape=(jax.ShapeDtypeStruct((QKV4_COLS // LANES, t, LANES), F32),
                   jax.ShapeDtypeStruct((2, t, LANES), F32),
                   jax.ShapeDtypeStruct((n_seq, WIDTH, win_a), F32),
                   jax.ShapeDtypeStruct((n_seq, WIDTH, win_a), F32),
                   jax.ShapeDtypeStruct((n_seq, 2 * KV_WIDTH_B, win_b), F32)),
        grid=(t // tm, 4),
        in_specs=[
            pl.BlockSpec((tm, D_MODEL), lambda i, j: (i, 0)),
            pl.BlockSpec((1, D_MODEL), lambda i, j: (0, 0)),
            pl.BlockSpec((D_MODEL, WIDTH), lambda i, j: (0, j)),
            pl.BlockSpec((D_MODEL, 2 * KV_WIDTH_B), lambda i, j: (0, QKV4_COLS // (2 * KV_WIDTH_B))),
            pl.BlockSpec((1, WIDTH), lambda i, j: (0, j)),
            pl.BlockSpec((1, 2 * KV_WIDTH_B), lambda i, j: (0, QKV4_COLS // (2 * KV_WIDTH_B))),
            pl.BlockSpec((tm, LANES), tab_map),
            pl.BlockSpec((tm, LANES), tab_map),
            pl.BlockSpec((tm, LANES), tab_map),
        ],
        out_specs=(pl.BlockSpec((WIDTH // LANES, tm, LANES), lambda i, j: (j, i, 0)),
                   pl.BlockSpec((2, tm, LANES), lambda i, j: (0, i, 0)),
                   pl.BlockSpec((1, WIDTH, tm), win_map),
                   pl.BlockSpec((1, WIDTH, tm), win_map),
                   pl.BlockSpec((1, 2 * KV_WIDTH_B, win_b), lambda i, j: (i // tiles_per_seq, 0, 0))),
        scratch_shapes=[pltpu.VMEM((tm, D_MODEL), BF16)],
        compiler_params=_cparams(("arbitrary", "arbitrary"), 48),
        name="qkv_proj",
    )(x, g_attn, w_in_bf, w_in_bf, b_in, b_in, cos, slo, shi)


def _rope_tables(pos):
    half = HEAD_DIM // 2
    inv = ROPE_THETA ** (-jnp.arange(half, dtype=F32) / half)
    ang = pos.astype(F32)[:, None] * inv[None, :]
    cos = jnp.tile(jnp.cos(ang), (1, LANES // half))
    sin = jnp.tile(jnp.sin(ang), (1, LANES // half))
    first = (jnp.arange(LANES) % HEAD_DIM) < half
    return cos, jnp.where(first, -sin, 0.0), jnp.where(first, 0.0, sin)


def _two_head_softmax_pv(q0, q1, k2, v2, valid2):
    qm = jnp.concatenate([q0, q1], axis=0).astype(BF16)
    s = lax.dot_general(qm, k2, (((1,), (1,)), ((), ())), preferred_element_type=F32)
    s = jnp.where(valid2, s, MASK_VALUE)
    m = jnp.max(s, axis=1, keepdims=True)
    p = jnp.exp(s - m)
    l = jnp.sum(p, axis=1, keepdims=True)
    o = jnp.dot(p.astype(BF16), v2, preferred_element_type=F32) / l
    lse = m + jnp.log(l)
    return o[:BLOCK], lse[:BLOCK], o[BLOCK:], lse[BLOCK:]


ATT_SB = BLOCK * max(d for _, d in DILATIONS)
ATT_UNITS = ATT_SB // BLOCK
ATT_UNROLL = 8


ATT_SUBSTEPS = (len(DILATIONS) + 1) * (ATT_UNITS // ATT_UNROLL)
N_SAMPLE_REFS = 12


def _attn_kernel(units_per_step, *refs):
    sample_in, rest = refs[:N_SAMPLE_REFS], refs[N_SAMPLE_REFS:]
    prompt_in, (oas_ref, obs_ref, oa_ref, ob_ref), scratch = rest[:11], rest[11:15], rest[15:]
    for j in range(units_per_step):
        _sample_unit(j, *sample_in, oas_ref, obs_ref)
    _attn_prompt_substep(*prompt_in, oa_ref, ob_ref, *scratch)


def _attn_prompt_substep(qa_ref, ka_ref, kap_ref, va_ref, vap_ref, qb_ref, kb_ref, kbp_ref, vb_ref, vbp_ref,
                         sink_ref, oa_ref, ob_ref, kcat, vcat, kbcat, vbcat, o_scr, lse_scr):
    has_prev = pl.program_id(1) > 0
    pair = pl.program_id(2)
    sub = pl.program_id(3)
    sb = ATT_SB

    @pl.when(sub == 0)
    def _():
        kcat[0:sb, :] = kap_ref[0]
        kcat[sb:2 * sb, :] = ka_ref[0]
        vcat[0:sb, :] = vap_ref[0]
        vcat[sb:2 * sb, :] = va_ref[0]
        kv_first = pair < (N_HEADS // N_KV_B) // 2
        lane_row = lax.broadcasted_iota(I32, (1, LANES), 1)
        kv_lanes = jnp.where(lane_row >= HEAD_DIM, 1, 0) == jnp.where(kv_first, 0, 1)
        both_halves = lambda x: jnp.where(kv_lanes, x, pltpu.roll(x, HEAD_DIM, 1))
        kbcat[0:BLOCK, :] = both_halves(kbp_ref[0])
        kbcat[BLOCK:BLOCK + sb, :] = both_halves(kb_ref[0])
        vbcat[0:BLOCK, :] = both_halves(vbp_ref[0])
        vbcat[BLOCK:BLOCK + sb, :] = both_halves(vb_ref[0])

    lane = lax.broadcasted_iota(I32, (BLOCK, LANES), 1)
    first = lane < HEAD_DIM
    qi = lax.broadcasted_iota(I32, (2 * BLOCK, 2 * BLOCK), 0) & (BLOCK - 1)
    si = lax.broadcasted_iota(I32, (2 * BLOCK, 2 * BLOCK), 1)
    dist = qi - si + BLOCK
    band = (dist >= 0) & (dist <= BLOCK)
    own_block = si >= BLOCK

    def rows(start, size, d):
        return pl.ds(start, size) if d == 1 else pl.ds(start, size, stride=d)

    def key_mask(blk):
        return band if blk > 0 else band & (own_block | has_prev)

    halves = ATT_UNITS // ATT_UNROLL
    for p, (_, d) in enumerate(DILATIONS):
        for half in range(halves):
            @pl.when(sub == p * halves + half)
            def _(p=p, d=d, half=half):
                for u in range(half * ATT_UNROLL, (half + 1) * ATT_UNROLL):
                    blk = u // d
                    qs = blk * (BLOCK * d) + u % d
                    q = qa_ref[0, rows(qs, BLOCK, d), :] * SCALE
                    k2 = kcat[rows(sb + qs - BLOCK * d, 2 * BLOCK, d), :].astype(BF16)
                    v2 = vcat[rows(sb + qs - BLOCK * d, 2 * BLOCK, d), :].astype(BF16)
                    o0, l0, o1, l1 = _two_head_softmax_pv(jnp.where(first, q, 0.0), jnp.where(first, 0.0, q),
                                                          k2, v2, key_mask(blk))
                    o_scr[p, rows(qs, BLOCK, d), :] = jnp.where(first, o0, o1)
                    lse_scr[p, rows(qs, BLOCK, d), :] = jnp.where(first, l0, l1)

    @pl.when(sub == len(DILATIONS) * halves - 1)
    def _():
        def mix(bk, carry):
            r = pl.ds(pl.multiple_of(bk * BLOCK, BLOCK), BLOCK)
            lses = [lse_scr[p, r, :] for p in range(len(DILATIONS))]
            m = functools.reduce(jnp.maximum, lses)
            ws = [jnp.exp(l - m) for l in lses]
            oa_ref[0, r, :] = sum(w * o_scr[p, r, :] for p, w in enumerate(ws)) / sum(ws)
            return carry
        lax.fori_loop(0, ATT_UNITS, mix, 0)

    for half in range(halves):
        @pl.when(sub == len(DILATIONS) * halves + half)
        def _(half=half):
            sink = sink_ref[...]
            for blk in range(half * ATT_UNROLL, (half + 1) * ATT_UNROLL):
                r = pl.ds(blk * BLOCK, BLOCK)
                r2 = pl.ds(blk * BLOCK, 2 * BLOCK)
                q = qb_ref[0, r, :] * SCALE
                k2 = kbcat[r2, :].astype(BF16)
                v2 = vbcat[r2, :].astype(BF16)
                o0, l0, o1, l1 = _two_head_softmax_pv(jnp.where(first, q, 0.0), jnp.where(first, 0.0, q),
                                                      k2, v2, key_mask(blk))
                lse = jnp.where(first, l0, l1)
                ob_ref[0, r, :] = jnp.where(first, o0, o1) * jax.nn.sigmoid(lse - sink)


def _attention(qkvc, kvbc, sinks_lanes, batch, seq, qkvc_s, kvbc_s, cache_a_k, cache_a_v, cache_b_k, cache_b_v, sinks):
    t = batch * seq
    assert seq % ATT_SB == 0
    n_sb = seq // ATT_SB
    n_pairs = WIDTH // LANES
    cur = lambda comp: (lambda b, s, p, q: (comp * n_pairs + p, b * n_sb + s, 0))
    prev = lambda comp: (lambda b, s, p, q: (comp * n_pairs + p, b * n_sb + jnp.maximum(s - 1, 0), 0))
    blk = lambda f: pl.BlockSpec((1, ATT_SB, LANES), f)
    kvb_cur = lambda c: pl.BlockSpec((1, ATT_SB, LANES), lambda b, s, p, q: (c, b * n_sb + s, 0))
    kvb_prev = lambda c: pl.BlockSpec(
        (1, BLOCK, LANES),
        lambda b, s, p, q: (c, jnp.maximum((b * n_sb + s) * ATT_UNITS - 1, b * n_sb * ATT_UNITS), 0))
    out_spec = pl.BlockSpec((1, ATT_SB, LANES), lambda b, s, p, q: (p, b * n_sb + s, 0))
    prompt_specs = [blk(cur(0)), blk(cur(1)), blk(prev(1)), blk(cur(2)), blk(prev(2)), blk(cur(3)),
                    kvb_cur(0), kvb_prev(0), kvb_cur(1), kvb_prev(1),
                    pl.BlockSpec((1, LANES), lambda b, s, p, q: (0, p))]

    nb = qkvc_s.shape[1]
    wa = cache_a_k.shape[1]
    hps = N_HEADS // N_KV_B
    n_units = nb * N_KV_B
    n_steps = batch * n_sb * n_pairs * ATT_SUBSTEPS
    assert n_units % n_steps == 0
    ups = n_units // n_steps
    assert cache_b_k.shape[1] == WINDOW_B and all(wa >= w and wa % d == 0 for w, d in DILATIONS)
    dist = wa - np.arange(wa)
    cnt = sum(((dist % d == 0) & (dist <= w)).astype(np.float32) for w, d in DILATIONS).reshape(1, wa)
    q4u = (qkvc_s.reshape(4, N_KV_B, n_pairs // N_KV_B, nb, LANES // HEAD_DIM, HEAD_DIM)
           .transpose(3, 1, 0, 2, 4, 5).reshape(n_units, 4, hps, HEAD_DIM))
    kvnu = (kvbc_s.reshape(2, nb, N_KV_B, HEAD_DIM).transpose(1, 2, 0, 3).reshape(n_units, 2, 1, HEAD_DIM))
    sink_u = jnp.tile(sinks.reshape(N_KV_B, hps, 1), (nb, 1, 1))
    to_units = lambda c: jnp.transpose(c, (0, 2, 3, 1)).reshape(n_units, -1, HEAD_DIM, c.shape[1])
    step = lambda b, s, p, q: ((b * n_sb + s) * n_pairs + p) * ATT_SUBSTEPS + q
    unit_blk = lambda shape: pl.BlockSpec((ups,) + shape, lambda b, s, p, q: (step(b, s, p, q),) + (0,) * len(shape))
    comp_blk = lambda c, shape: pl.BlockSpec((ups, 1) + shape,
                                             lambda b, s, p, q: (step(b, s, p, q), c) + (0,) * len(shape))
    sample_specs = [pl.BlockSpec((1, wa), lambda b, s, p, q: (0, 0)),
                    comp_blk(0, (hps, HEAD_DIM)), comp_blk(1, (hps, HEAD_DIM)), comp_blk(2, (hps, HEAD_DIM)),
                    comp_blk(3, (hps, HEAD_DIM)), comp_blk(0, (1, HEAD_DIM)), comp_blk(1, (1, HEAD_DIM)),
                    unit_blk((hps, 1)),
                    unit_blk((hps, HEAD_DIM, wa)), unit_blk((hps, HEAD_DIM, wa)),
                    unit_blk((1, HEAD_DIM, WINDOW_B)), unit_blk((1, HEAD_DIM, WINDOW_B))]
    assert len(sample_specs) == N_SAMPLE_REFS
    oas, obs, oa, ob = pl.pallas_call(
        functools.partial(_attn_kernel, ups),
        out_shape=(jax.ShapeDtypeStruct((n_units, hps, HEAD_DIM), F32),
                   jax.ShapeDtypeStruct((n_units, hps, HEAD_DIM), F32),
                   jax.ShapeDtypeStruct((n_pairs, t, LANES), F32),
                   jax.ShapeDtypeStruct((n_pairs, t, LANES), F32)),
        grid=(batch, n_sb, n_pairs, ATT_SUBSTEPS),
        in_specs=sample_specs + prompt_specs,
        out_specs=(unit_blk((hps, HEAD_DIM)), unit_blk((hps, HEAD_DIM)), out_spec, out_spec),
        scratch_shapes=[
            pltpu.VMEM((2 * ATT_SB, LANES), F32), pltpu.VMEM((2 * ATT_SB, LANES), F32),
            pltpu.VMEM((BLOCK + ATT_SB, LANES), F32), pltpu.VMEM((BLOCK + ATT_SB, LANES), F32),
            pltpu.VMEM((len(DILATIONS), ATT_SB, LANES), F32), pltpu.VMEM((len(DILATIONS), ATT_SB, LANES), F32)],
        compiler_params=_cparams(("parallel", "parallel", "arbitrary", "arbitrary"), 56),
        name="attention",
    )(jnp.asarray(cnt), q4u, q4u, q4u, q4u, kvnu, kvnu, sink_u,
      to_units(cache_a_k), to_units(cache_a_v), to_units(cache_b_k), to_units(cache_b_v),
      qkvc, qkvc, qkvc, qkvc, qkvc, qkvc, kvbc, kvbc, kvbc, kvbc, sinks_lanes)
    chunked = lambda o: o.reshape(nb, n_pairs, LANES).transpose(1, 0, 2)
    return oa, ob, chunked(oas), chunked(obs)


def _hi_dot(a, b, dims=(((1,), (0,)), ((), ()))):
    return lax.dot_general(a, b, dims, preferred_element_type=F32, precision=lax.Precision.HIGHEST)


def _sample_unit(j, cnt_ref, qa_ref, kn_ref, vn_ref, qb_ref, kbn_ref, vbn_ref, sink_ref,
                 kt_ref, vt_ref, kbt_ref, vbt_ref, oa_ref, ob_ref):
    hps = N_HEADS // N_KV_B
    nt = (((1,), (1,)), ((), ()))
    head = lax.broadcasted_iota(I32, (hps, 1), 0)
    cnt = cnt_ref[...]
    n_pat = float(len(DILATIONS))

    qa = qa_ref[j, 0] * SCALE
    qa_bf = qa.astype(BF16)
    s = jnp.zeros((hps, cnt.shape[1]), F32)
    for hl in range(hps):
        s_hl = jnp.dot(qa_bf, kt_ref[j, hl].astype(BF16), preferred_element_type=F32)
        s = jnp.where(head == hl, s_hl, s)
    s = jnp.where(cnt > 0.0, s, MASK_VALUE)
    s_new = jnp.sum(qa * kn_ref[j, 0], axis=1, keepdims=True)
    m = jnp.maximum(jnp.max(s, axis=1, keepdims=True), s_new)
    e = cnt * jnp.exp(s - m)
    e_new = n_pat * jnp.exp(s_new - m)
    den = jnp.sum(e, axis=1, keepdims=True) + e_new
    e_bf = e.astype(BF16)
    pv = jnp.zeros((hps, HEAD_DIM), F32)
    for hl in range(hps):
        o_hl = lax.dot_general(e_bf, vt_ref[j, hl].astype(BF16), nt, preferred_element_type=F32)
        pv = jnp.where(head == hl, o_hl, pv)
    oa_ref[j] = (pv + e_new * vn_ref[j, 0]) / den

    qb = qb_ref[j, 0] * SCALE
    sb = _hi_dot(qb, kbt_ref[j, 0])
    sn = jnp.sum(qb * kbn_ref[j, 0], axis=1, keepdims=True)
    mm = jnp.maximum(jnp.max(sb, axis=1, keepdims=True), sn)
    eb = jnp.exp(sb - mm)
    en = jnp.exp(sn - mm)
    l = jnp.sum(eb, axis=1, keepdims=True) + en
    ob = (_hi_dot(eb, vbt_ref[j, 0], nt) + en * vbn_ref[j, 0]) / l
    lse = mm + jnp.log(l)
    ob_ref[j] = ob * jax.nn.sigmoid(lse - sink_ref[j])


def _split_bf16(x):
    hi = x.astype(BF16)
    lo = (x - hi.astype(F32)).astype(BF16)
    return hi, lo


def _post_first_kernel(n_valid, *refs):
    i = pl.program_id(0)

    @pl.when(i < n_valid)
    def _():
        _post_body(*refs)

    @pl.when(i >= n_valid)
    def _():
        for o_ref in refs[-4:]:
            o_ref[...] = jnp.zeros(o_ref.shape, o_ref.dtype)


def _post_append_kernel(x1_all, h2_all, ridx_all, rgate_all, *refs):
    del x1_all, h2_all, ridx_all, rgate_all
    _post_body(*refs)


def _post_body(oa_ref, ob_ref, x_ref, ga_ref, gb_ref, w_ref, b_ref, gf_ref, wr_ref, br_ref,
               x1_ref, h2_ref, ridx_ref, rgate_ref):
    unchunk = lambda ref: jnp.concatenate([ref[c] for c in range(ref.shape[0])], axis=1)
    na = _rmsnorm(unchunk(oa_ref), ga_ref[...]).astype(BF16)
    nb = _rmsnorm(unchunk(ob_ref), gb_ref[...]).astype(BF16)
    c = jnp.concatenate([na, nb], axis=1)
    x1 = x_ref[...] + jnp.dot(c, w_ref[...], preferred_element_type=F32) + b_ref[...]
    x1_ref[...] = x1
    h2 = _rmsnorm(x1, gf_ref[...])
    h2_ref[...] = h2

    h_hi, h_lo = _split_bf16(h2)
    w_hi, w_lo = _split_bf16(wr_ref[...])
    dot = lambda a, b: jnp.dot(a, b, preferred_element_type=F32)
    logits = dot(h_hi, w_hi) + (dot(h_hi, w_lo) + dot(h_lo, w_hi)) + br_ref[...]

    tm = logits.shape[0]
    eidx = lax.broadcasted_iota(I32, (tm, N_EXPERTS), 1)
    lane = lax.broadcasted_iota(I32, (tm, LANES), 1)
    work = logits
    vals, idxs = [], []
    for _ in range(TOP_K):
        v = jnp.max(work, axis=1, keepdims=True)
        i = jnp.min(jnp.where(work == v, eidx, N_EXPERTS), axis=1, keepdims=True)
        vals.append(v)
        idxs.append(i)
        work = jnp.where(eidx == i, -jnp.inf, work)
    es = [jnp.exp(v - vals[0]) for v in vals]
    den = sum(es)
    ridx = jnp.zeros((tm, LANES), I32)
    rgate = jnp.zeros((tm, LANES), F32)
    for k in range(TOP_K):
        ridx = jnp.where(lane == k, idxs[k], ridx)
        rgate = jnp.where(lane == k, es[k] / den, rgate)
    ridx_ref[...] = ridx
    rgate_ref[...] = rgate


def _post_attention(oa, ob, x, g_out_a, g_out_b, w_out_bf, b_out, g_ffn, w_router, b_router, tm,
                    out_rows=None, append_to=None, row0=0):
    t = x.shape[0]
    n_valid = t // tm
    n_pairs = WIDTH // LANES
    clamp = lambda i: jnp.minimum(i, n_valid - 1)
    chunked = pl.BlockSpec((n_pairs, tm, LANES), lambda i: (0, clamp(i), 0))
    full = lambda r, c: pl.BlockSpec((r, c), lambda i: (0, 0))
    in_specs = [chunked, chunked, pl.BlockSpec((tm, D_MODEL), lambda i: (clamp(i), 0)),
                full(1, WIDTH), full(1, WIDTH), full(D_MODEL, D_MODEL), full(1, D_MODEL), full(1, D_MODEL),
                full(D_MODEL, N_EXPERTS), full(1, N_EXPERTS)]
    args = (oa, ob, x, g_out_a, g_out_b, w_out_bf, b_out, g_ffn, w_router, b_router)
    if append_to is None:
        rows = t if out_rows is None else out_rows
        assert rows % tm == 0 and t % tm == 0
        kernel_fn, aliases, blk0 = functools.partial(_post_first_kernel, n_valid), {}, 0
    else:
        rows = append_to[0].shape[0]
        assert row0 % tm == 0 and t % tm == 0
        kernel_fn, aliases, blk0 = _post_append_kernel, {k: k for k in range(4)}, row0 // tm
        in_specs = [pl.BlockSpec(memory_space=pl.ANY)] * 4 + in_specs
        args = tuple(append_to) + args
    n_steps = rows // tm if append_to is None else n_valid
    outblk = lambda w: pl.BlockSpec((tm, w), lambda i: (blk0 + i, 0))
    return pl.pallas_call(
        kernel_fn,
        out_shape=(jax.ShapeDtypeStruct((rows, D_MODEL), F32), jax.ShapeDtypeStruct((rows, D_MODEL), F32),
                   jax.ShapeDtypeStruct((rows, LANES), I32), jax.ShapeDtypeStruct((rows, LANES), F32)),
        grid=(n_steps,),
        in_specs=in_specs,
        out_specs=(outblk(D_MODEL), outblk(D_MODEL), outblk(LANES), outblk(LANES)),
        input_output_aliases=aliases,
        compiler_params=_cparams(("arbitrary",), 48),
        name="post_attention",
    )(*args)


def _dispatch_kernel(dest_ref, h_hbm, o_ref, tok_ref, buf, sem):
    i = pl.program_id(0)

    @pl.when(i == 0)
    def _():
        def clear(p, carry):
            tok_ref[p] = 0
            return carry
        lax.fori_loop(0, tok_ref.shape[0], clear, 0, unroll=8)

        def place(t, carry):
            for k in range(TOP_K):
                tok_ref[dest_ref[t * TOP_K + k]] = t
            return carry
        lax.fori_loop(0, dest_ref.shape[0] // TOP_K, place, 0, unroll=4)

    def issue(step, slot):
        base = step * DISPATCH_ROWS

        def body(r, carry):
            pltpu.make_async_copy(h_hbm.at[pl.ds(tok_ref[base + r], 1)],
                                  buf.at[slot, pl.ds(r, 1)], sem.at[slot]).start()
            return carry
        lax.fori_loop(0, DISPATCH_ROWS, body, 0, unroll=8)

    @pl.when(i == 0)
    def _():
        issue(0, 0)

    @pl.when(i + 1 < pl.num_programs(0))
    def _():
        issue(i + 1, (i + 1) % 2)

    slot = i % 2
    pltpu.make_async_copy(h_hbm.at[pl.ds(0, DISPATCH_ROWS)], buf.at[slot], sem.at[slot]).wait()
    o_ref[...] = buf[slot].astype(BF16)


def _dispatch(dest, h2, p):
    return pl.pallas_call(
        _dispatch_kernel,
        out_shape=jax.ShapeDtypeStruct((p, D_MODEL), BF16),
        grid_spec=pltpu.PrefetchScalarGridSpec(
            num_scalar_prefetch=1,
            grid=(p // DISPATCH_ROWS,),
            in_specs=[pl.BlockSpec(memory_space=pl.ANY)],
            out_specs=pl.BlockSpec((DISPATCH_ROWS, D_MODEL), lambda i, dst: (i, 0)),
            scratch_shapes=[pltpu.SMEM((p,), I32),
                            pltpu.VMEM((2, DISPATCH_ROWS, D_MODEL), F32), pltpu.SemaphoreType.DMA((2,))],
        ),
        compiler_params=_cparams(("arbitrary",), 32),
        name="moe_dispatch",
    )(dest, h2)


def _moe_kernel(ie_ref, iblk_ref, insub_ref, iact_ref, used_ref,
                x_hbm, wg_ref, wu_ref, wd_ref, bg_ref, bu_ref, bd_ref, y_hbm,
                xbuf, act, ystage, xsem, ysem):
    it = pl.program_id(0)
    s = pl.program_id(1)
    nsub = insub_ref[it]
    row0 = iblk_ref[it] * MOE_SUB

    xslot = it % 2

    def x_copies(item, op):
        first = iblk_ref[item] * MOE_SUB

        def body(j, c):
            cp = pltpu.make_async_copy(x_hbm.at[pl.ds(first + j * MOE_SUB, MOE_SUB)],
                                       xbuf.at[item % 2, pl.ds(j * MOE_SUB, MOE_SUB)], xsem.at[item % 2])
            cp.start() if op == "start" else cp.wait()
            return c
        lax.fori_loop(0, insub_ref[item], body, 0)

    def for_row_chunks(fn):
        big = nsub // 4

        def body(j, c):
            fn(pl.multiple_of(j * (4 * MOE_SUB), 4 * MOE_SUB), 4 * MOE_SUB)
            return c
        lax.fori_loop(0, big, body, 0)
        rem = nsub - 4 * big
        has2 = rem >= 2

        @pl.when(has2)
        def _():
            fn(pl.multiple_of(big * (4 * MOE_SUB), MOE_SUB), 2 * MOE_SUB)

        @pl.when((rem & 1) == 1)
        def _():
            fn(pl.multiple_of((4 * big + jnp.where(has2, 2, 0)) * MOE_SUB, MOE_SUB), MOE_SUB)

    @pl.when((s == 0) & (it == 0))
    def _():
        x_copies(0, "start")

    @pl.when(s == 0)
    def _():
        x_copies(it, "wait")

    @pl.when((s == 1) & (it + 1 < pl.num_programs(0)))
    def _():
        x_copies(it + 1, "start")

    @pl.when((s < MOE_F) & (nsub > 0))
    def _():
        bg, bu = bg_ref[...], bu_ref[...]

        def gate_up(r0, rows):
            x = xbuf[xslot, pl.ds(r0, rows), :]
            g = jnp.dot(x, wg_ref[...].astype(BF16), preferred_element_type=F32) + bg
            u = jnp.dot(x, wu_ref[...].astype(BF16), preferred_element_type=F32) + bu
            g = jnp.minimum(g, SWIGLU_LIMIT)
            u = jnp.clip(u, -SWIGLU_LIMIT, SWIGLU_LIMIT)
            a = g * jax.nn.sigmoid(SWIGLU_ALPHA * g) * (u + 1.0)
            act[s, pl.ds(r0, rows), :] = a.astype(BF16)
        for_row_chunks(gate_up)

    @pl.when((s >= MOE_F) & (nsub > 0))
    def _():
        n = s - MOE_F
        slot = n % 2
        bd = bd_ref[...]

        def y_copy(step, r0, rows):
            c0 = pl.multiple_of(step * MOE_TF, MOE_TF)
            return pltpu.make_async_copy(
                ystage.at[step % 2, pl.ds(r0, rows), :],
                y_hbm.at[pl.ds(row0 + r0, rows), pl.ds(c0, MOE_TF)], ysem.at[step % 2])

        def down(r0, rows):
            a = jnp.concatenate([act[f, pl.ds(r0, rows), :] for f in range(MOE_F)], axis=1)
            ystage[slot, pl.ds(r0, rows), :] = (
                jnp.dot(a, wd_ref[...].astype(BF16), preferred_element_type=F32) + bd)
            y_copy(n, r0, rows).start()
        for_row_chunks(down)

        @pl.when(n > 0)
        def _():
            for_row_chunks(lambda r0, rows: y_copy(n - 1, r0, rows).wait())

        @pl.when(n == MOE_F - 1)
        def _():
            for_row_chunks(lambda r0, rows: y_copy(n, r0, rows).wait())

    @pl.when((it == pl.num_programs(0) - 1) & (s == pl.num_programs(1) - 1))
    def _():
        n_blocks = y_hbm.shape[0] // MOE_SUB
        ystage[0, pl.ds(0, MOE_SUB), :] = jnp.zeros((MOE_SUB, MOE_TF), F32)

        def z_copy(b, n):
            return pltpu.make_async_copy(
                ystage.at[0, pl.ds(0, MOE_SUB), :],
                y_hbm.at[pl.ds(b * MOE_SUB, MOE_SUB), pl.ds(n * MOE_TF, MOE_TF)], ysem.at[0])

        def fill(b, c):
            for n in range(MOE_F):
                z_copy(b, n).start()
            for n in range(MOE_F):
                z_copy(b, n).wait()
            return c
        lax.fori_loop(used_ref[0], n_blocks, fill, 0)


def _moe_experts(items, x_sorted, w_gate_up, b_gate_up, w_down, b_down):
    item_e, item_blk, item_nsub, item_act, used_blocks = items
    n_items = item_e.shape[0]
    p = x_sorted.shape[0]
    rmax = MOE_MAX_SUB * MOE_SUB
    last = MOE_F - 1
    f_of = lambda s, act: jnp.where(act == 1, jnp.minimum(s, last), last)
    n_of = lambda s, act: jnp.where(act == 1, jnp.maximum(s - MOE_F, 0), last)
    wg_map = lambda i, s, ie, ib, ins, ia, iu: (ie[i], 0, f_of(s, ia[i]))
    wu_map = lambda i, s, ie, ib, ins, ia, iu: (ie[i], 0, MOE_F + f_of(s, ia[i]))
    wd_map = lambda i, s, ie, ib, ins, ia, iu: (ie[i], 0, n_of(s, ia[i]))
    bgu = b_gate_up.reshape(N_EXPERTS, 1, 2 * D_FF)
    bd = b_down.reshape(N_EXPERTS, 1, D_MODEL)
    return pl.pallas_call(
        _moe_kernel,
        out_shape=jax.ShapeDtypeStruct((p, D_MODEL), F32),
        grid_spec=pltpu.PrefetchScalarGridSpec(
            num_scalar_prefetch=5,
            grid=(n_items, 2 * MOE_F),
            in_specs=[
                pl.BlockSpec(memory_space=pl.ANY),
                pl.BlockSpec((None, D_MODEL, MOE_TF), wg_map),
                pl.BlockSpec((None, D_MODEL, MOE_TF), wu_map),
                pl.BlockSpec((None, D_FF, MOE_TF), wd_map),
                pl.BlockSpec((None, 1, MOE_TF), wg_map),
                pl.BlockSpec((None, 1, MOE_TF), wu_map),
                pl.BlockSpec((None, 1, MOE_TF), wd_map),
            ],
            out_specs=pl.BlockSpec(memory_space=pl.ANY),
            scratch_shapes=[
                pltpu.VMEM((2, rmax, D_MODEL), BF16),
                pltpu.VMEM((MOE_F, rmax, MOE_TF), BF16),
                pltpu.VMEM((2, rmax, MOE_TF), F32),
                pltpu.SemaphoreType.DMA((2,)),
                pltpu.SemaphoreType.DMA((2,)),
            ],
        ),
        compiler_params=_cparams(("arbitrary", "arbitrary"), 56),
        name="moe_experts",
    )(item_e, item_blk, item_nsub, item_act, used_blocks, x_sorted, w_gate_up, w_gate_up, w_down, bgu, bgu, bd)


def _tail_kernel(n_first, dest_ref, y_hbm, gate_ref, x1_ref, p_ref, wg_ref, wp_ref, gp_ref, gf_ref,
                 o1_ref, o2_ref, ybuf, sem):
    i = pl.program_id(0)

    def issue(step, slot):
        base = step * (TAIL_ROWS * TOP_K)

        def body(r, carry):
            for k in range(TOP_K):
                pltpu.make_async_copy(y_hbm.at[pl.ds(dest_ref[base + r * TOP_K + k], 1)],
                                      ybuf.at[slot, k, pl.ds(r, 1)], sem.at[slot]).start()
            return carry
        lax.fori_loop(0, TAIL_ROWS, body, 0, unroll=2)

    @pl.when(i == 0)
    def _():
        issue(0, 0)

    @pl.when(i + 1 < pl.num_programs(0))
    def _():
        issue(i + 1, (i + 1) % 2)

    slot = i % 2
    for k in range(TOP_K):
        pltpu.make_async_copy(y_hbm.at[pl.ds(0, TAIL_ROWS)], ybuf.at[slot, k], sem.at[slot]).wait()

    gate = gate_ref[...]
    moe = ybuf[slot, 0] * gate[:, 0:1]
    for k in range(1, TOP_K):
        moe = moe + ybuf[slot, k] * gate[:, k:k + 1]
    x2 = x1_ref[...] + moe
    h3 = _rmsnorm(x2, gp_ref[...]).astype(BF16)
    ple_gate = jax.nn.sigmoid(jnp.dot(h3, wg_ref[...], preferred_element_type=F32))
    ple = jnp.dot(p_ref[...].astype(BF16), wp_ref[...], preferred_element_type=F32)
    x3 = x2 + ple_gate * ple
    y = _rmsnorm(x3, gf_ref[...])

    @pl.when(i < n_first)
    def _():
        o1_ref[...] = y

    @pl.when(i >= n_first)
    def _():
        o2_ref[...] = y


def _tail(dest, y_sorted, rgate, x1, p_all, w_ple_gate_bf, w_ple_proj_bf, g_ple, g_final, t_first):
    t = dest.shape[0] // TOP_K
    assert t_first % TAIL_ROWS == 0 and t % TAIL_ROWS == 0
    n_first = t_first // TAIL_ROWS
    rowblk = lambda w: pl.BlockSpec((TAIL_ROWS, w), lambda i, d: (i, 0))
    full = lambda r, c: pl.BlockSpec((r, c), lambda i, d: (0, 0))
    return pl.pallas_call(
        functools.partial(_tail_kernel, n_first),
        out_shape=(jax.ShapeDtypeStruct((t_first, D_MODEL), F32),
                   jax.ShapeDtypeStruct((t - t_first, D_MODEL), F32)),
        grid_spec=pltpu.PrefetchScalarGridSpec(
            num_scalar_prefetch=1,
            grid=(t // TAIL_ROWS,),
            in_specs=[
                pl.BlockSpec(memory_space=pl.ANY),
                rowblk(LANES), rowblk(D_MODEL), rowblk(PLE_DIM),
                full(D_MODEL, D_MODEL), full(PLE_DIM, D_MODEL), full(1, D_MODEL), full(1, D_MODEL)],
            out_specs=(
                pl.BlockSpec((TAIL_ROWS, D_MODEL), lambda i, d: (jnp.minimum(i, n_first - 1), 0)),
                pl.BlockSpec((TAIL_ROWS, D_MODEL), lambda i, d: (jnp.maximum(i - n_first, 0), 0))),
            scratch_shapes=[pltpu.VMEM((2, TOP_K, TAIL_ROWS, D_MODEL), F32), pltpu.SemaphoreType.DMA((2,))],
        ),
        compiler_params=_cparams(("arbitrary",), 48),
        name="moe_tail",
    )(dest, y_sorted, rgate, x1, p_all, w_ple_gate_bf, w_ple_proj_bf, g_ple, g_final)


def _routing(ridx, rgate_unused, t):
    del rgate_unused
    m = t * TOP_K
    flat_e = ridx[:, :TOP_K].reshape(m)
    onehot = (flat_e[:, None] == jnp.arange(N_EXPERTS, dtype=I32)[None, :]).astype(I32)
    csum = jnp.cumsum(onehot, axis=0)
    rank = jnp.sum((csum - onehot) * onehot, axis=1)
    counts = csum[-1]
    nblk = (counts + MOE_SUB - 1) // MOE_SUB
    blk_end = jnp.cumsum(nblk)
    blk_start = blk_end - nblk
    dest = blk_start[flat_e] * MOE_SUB + rank
    n_blocks = -(-m // MOE_SUB) + N_EXPERTS
    n_blocks = -(-n_blocks // (DISPATCH_ROWS // MOE_SUB)) * (DISPATCH_ROWS // MOE_SUB)
    p = n_blocks * MOE_SUB

    n_items = (n_blocks + (MOE_MAX_SUB - 1) * N_EXPERTS) // MOE_MAX_SUB
    items_per_e = (nblk + MOE_MAX_SUB - 1) // MOE_MAX_SUB
    item_end = jnp.cumsum(items_per_e)
    total = item_end[-1]
    ids = jnp.arange(n_items, dtype=I32)
    active = ids < total
    ids_c = jnp.minimum(ids, total - 1)
    item_e = jnp.searchsorted(item_end, ids_c, side="right").astype(I32)
    local = ids_c - (item_end - items_per_e)[item_e]
    item_blk = blk_start[item_e] + local * MOE_MAX_SUB
    item_nsub = jnp.where(active, jnp.minimum(MOE_MAX_SUB, nblk[item_e] - local * MOE_MAX_SUB), 0)
    items = (item_e, item_blk.astype(I32), item_nsub.astype(I32), active.astype(I32),
             blk_end[-1:].astype(I32))
    return p, dest.astype(I32), items


def kernel(x_prompt, x_sample, cache_a_k, cache_a_v, cache_b_k, cache_b_v, p_prompt, p_sample, g_attn, w_in, b_in, sinks, g_out_a, g_out_b, w_out, b_out, g_ffn, w_router, b_router, w_gate_up, b_gate_up, w_down, b_down, g_ple, w_ple_gate, w_ple_proj, g_final):
    batch, seq, _ = x_prompt.shape
    nb, nseq, _ = x_sample.shape
    assert nseq == 1 and g_attn.shape[0] == 1
    tp, ts = batch * seq, nb * nseq
    row = lambda v: v.reshape(1, -1)

    w_in_bf = _cast_bf16(w_in[0])
    w_out_bf = _cast_bf16(w_out[0])
    w_pg_bf = _cast_bf16(w_ple_gate[0])
    w_pp_bf = _cast_bf16(w_ple_proj[0])

    tabs_p = _rope_tables(jnp.arange(seq, dtype=I32))
    tabs_s = _rope_tables(jnp.full((ts,), PAST_LEN, I32))
    sinks_lanes = jnp.repeat(sinks[0], HEAD_DIM).reshape(1, WIDTH)

    tm_p = 512
    xp = x_prompt.reshape(tp, D_MODEL)
    xs = x_sample.reshape(ts, D_MODEL)
    win_a, win_b = min(DILATIONS[-1][0], seq), min(WINDOW_B, seq)
    qkvc_p, kvbc_p, kt_p, vt_p, kvbt_p = _qkv_proj(xp, row(g_attn), w_in_bf, row(b_in), tabs_p, tm_p,
                                                    batch, win_a, win_b)
    qkvc_s, kvbc_s, kt_s, vt_s, kvbt_s = _qkv_proj(xs, row(g_attn), w_in_bf, row(b_in), tabs_s, ts, 1, ts, ts)

    oa_p, ob_p, oa_s, ob_s = _attention(qkvc_p, kvbc_p, sinks_lanes, batch, seq, qkvc_s, kvbc_s,
                                        cache_a_k[0], cache_a_v[0], cache_b_k[0], cache_b_v[0], sinks[0])

    post = functools.partial(_post_attention, g_out_a=row(g_out_a), g_out_b=row(g_out_b), w_out_bf=w_out_bf,
                             b_out=row(b_out), g_ffn=row(g_ffn), w_router=w_router[0], b_router=row(b_router))
    t = tp + ts
    tm_post = 256
    merged = post(oa_p, ob_p, xp, tm=tm_post, out_rows=-(-t // tm_post) * tm_post)
    x1, h2, ridx, rgate = post(oa_s, ob_s, xs, tm=ts, append_to=merged, row0=tp)

    cat = lambda a, b: jnp.concatenate([a, b], axis=0)
    n_slots, dest, items = _routing(ridx[:t], rgate, t)
    x_sorted = _dispatch(dest, h2, n_slots)
    y_sorted = _moe_experts(items, x_sorted, w_gate_up[0], b_gate_up[0], w_down[0], b_down[0])
    p_all = cat(p_prompt[0].reshape(tp, PLE_DIM), p_sample[0].reshape(ts, PLE_DIM))
    y_p, y_s = _tail(dest, y_sorted, rgate, x1, p_all, w_pg_bf, w_pp_bf, row(g_ple), row(g_final), tp)

    y_prompt = y_p.reshape(batch, seq, D_MODEL)
    y_sample = y_s.reshape(nb, nseq, D_MODEL)

    def rows_out(t3, heads):
        n, _, rows = t3.shape
        return t3.reshape(n, heads, HEAD_DIM, rows).transpose(0, 3, 1, 2)[None]

    def sample_out(t3, heads):
        return t3.reshape(heads, HEAD_DIM, nb).transpose(2, 0, 1)[None, :, None]

    return (y_prompt, y_sample,
            rows_out(kt_p, N_HEADS), rows_out(vt_p, N_HEADS),
            rows_out(kvbt_p[:, :KV_WIDTH_B], N_KV_B), rows_out(kvbt_p[:, KV_WIDTH_B:], N_KV_B),
            sample_out(kt_s, N_HEADS), sample_out(vt_s, N_HEADS),
            sample_out(kvbt_s[:, :KV_WIDTH_B], N_KV_B), sample_out(kvbt_s[:, KV_WIDTH_B:], N_KV_B))
```

```python
import functools

import jax
import jax.numpy as jnp
import numpy as np
from jax import lax
from jax.experimental import pallas as pl
from jax.experimental.pallas import tpu as pltpu

F32 = jnp.float32
BF16 = jnp.bfloat16
I32 = jnp.int32

D_MODEL = 2048
HEAD_DIM = 64
N_HEADS = 16
WIDTH = N_HEADS * HEAD_DIM
N_KV_B = 2
KV_WIDTH_B = N_KV_B * HEAD_DIM
DILATIONS = ((128, 1), (512, 4), (2048, 16))
WINDOW_B = 128
BLOCK = 128
N_EXPERTS = 32
TOP_K = 4
D_FF = 2048
SWIGLU_ALPHA = 1.702
SWIGLU_LIMIT = 7.0
PLE_DIM = 256
ROPE_THETA = 10000.0
NORM_EPS = 1e-5
MASK_VALUE = -1e30
SCALE = HEAD_DIM ** -0.5
PAST_LEN = 16384
QKV4_COLS = 4 * WIDTH
IN_COLS = QKV4_COLS + 2 * KV_WIDTH_B

LANES = 128
VMEM_LIMIT_CAP = 60 * 1024 * 1024

MOE_TF = 512
MOE_F = D_FF // MOE_TF
MOE_SUB = 128
MOE_MAX_SUB = 12
DISPATCH_ROWS = 512
TAIL_ROWS = 128


def _cparams(semantics, vmem_mb):
    return pltpu.CompilerParams(
        dimension_semantics=semantics,
        vmem_limit_bytes=min(vmem_mb * 1024 * 1024, VMEM_LIMIT_CAP))


def _rmsnorm(x, g):
    ms = jnp.mean(x * x, axis=-1, keepdims=True)
    return x * lax.rsqrt(ms + NORM_EPS) * g


def _cast_kernel(x_ref, o_ref):
    o_ref[...] = x_ref[...].astype(o_ref.dtype)


def _cast_bf16(w, rows_per_step=256):
    r, c = w.shape
    return pl.pallas_call(
        _cast_kernel,
        out_shape=jax.ShapeDtypeStruct((r, c), BF16),
        grid=(r // rows_per_step,),
        in_specs=[pl.BlockSpec((rows_per_step, c), lambda i: (i, 0))],
        out_specs=pl.BlockSpec((rows_per_step, c), lambda i: (i, 0)),
        compiler_params=_cparams(("parallel",), 32),
        name="cast_bf16",
    )(w)


def _rope(yc, cos, sin_lo, sin_hi):
    return yc * cos + pltpu.roll(yc, LANES - 32, 1) * sin_lo + pltpu.roll(yc, 32, 1) * sin_hi


def _qkv_kernel(tiles_per_seq, first_win_tile, win_b,
                x_ref, g_ref, w_ref, wkv_ref, b_ref, bkv_ref, cos_ref, slo_ref, shi_ref,
                qkv_ref, kvb_ref, kt_ref, vt_ref, kvbt_ref, h_scr):
    i = pl.program_id(0)
    j = pl.program_id(1)
    tile_in_seq = i % tiles_per_seq
    in_window = tile_in_seq >= first_win_tile
    n_chunks = WIDTH // LANES
    chunk = lambda v, c: v[:, c * LANES:(c + 1) * LANES]

    @pl.when(j == 0)
    def _():
        h_scr[...] = _rmsnorm(x_ref[...], g_ref[...]).astype(BF16)

    h = h_scr[...]
    cos, slo, shi = cos_ref[...], slo_ref[...], shi_ref[...]
    is_v = j == 2
    for c2 in range(n_chunks // 2):
        cols = slice(2 * c2 * LANES, 2 * (c2 + 1) * LANES)
        y2 = jnp.dot(h, w_ref[:, cols], preferred_element_type=F32) + b_ref[:, cols]
        for k in range(2):
            yc = chunk(y2, k)
            qkv_ref[2 * c2 + k] = jnp.where(is_v, yc, _rope(yc, cos, slo, shi))

    @pl.when((j == 1) & in_window)
    def _():
        for c in range(n_chunks):
            kt_ref[0, c * LANES:(c + 1) * LANES, :] = qkv_ref[c].T

    @pl.when(is_v & in_window)
    def _():
        for c in range(n_chunks):
            vt_ref[0, c * LANES:(c + 1) * LANES, :] = qkv_ref[c].T

    @pl.when(j == 3)
    def _():
        ykv = jnp.dot(h, wkv_ref[...], preferred_element_type=F32) + bkv_ref[...]
        kvb_ref[0] = _rope(chunk(ykv, 0), cos, slo, shi)
        kvb_ref[1] = chunk(ykv, 1)

    @pl.when((j == 3) & (tile_in_seq == tiles_per_seq - 1))
    def _():
        tm = kvb_ref.shape[1]
        kvbt_ref[0, :KV_WIDTH_B, :] = kvb_ref[0, tm - win_b:, :].T
        kvbt_ref[0, KV_WIDTH_B:, :] = kvb_ref[1, tm - win_b:, :].T


def _qkv_proj(x, g_attn, w_in_bf, b_in, rope_tabs, tm, n_seq, win_a, win_b):
    t = x.shape[0]
    seq = t // n_seq
    tiles_per_seq = seq // tm
    assert seq % tm == 0 and win_a % tm == 0 and win_a <= seq and win_b <= tm
    first_win_tile = tiles_per_seq - win_a // tm
    cos, slo, shi = rope_tabs
    tab_blocks = cos.shape[0] // tm
    tab_map = lambda i, j: (i % tab_blocks, 0)
    win_map = lambda i, j: (i // tiles_per_seq, 0, jnp.maximum(i % tiles_per_seq - first_win_tile, 0))
    return pl.pallas_call(
        functools.partial(_qkv_kernel, tiles_per_seq, first_win_tile, win_b),
        out_shape=(jax.ShapeDtypeStruct((QKV4_COLS // LANES, t, LANES), F32),
                   jax.ShapeDtypeStruct((2, t, LANES), F32),
                   jax.ShapeDtypeStruct((n_seq, WIDTH, win_a), F32),
                   jax.ShapeDtypeStruct((n_seq, WIDTH, win_a), F32),
                   jax.ShapeDtypeStruct((n_seq, 2 * KV_WIDTH_B, win_b), F32)),
        grid=(t // tm, 4),
        in_specs=[
            pl.BlockSpec((tm, D_MODEL), lambda i, j: (i, 0)),
            pl.BlockSpec((1, D_MODEL), lambda i, j: (0, 0)),
            pl.BlockSpec((D_MODEL, WIDTH), lambda i, j: (0, j)),
            pl.BlockSpec((D_MODEL, 2 * KV_WIDTH_B), lambda i, j: (0, QKV4_COLS // (2 * KV_WIDTH_B))),
            pl.BlockSpec((1, WIDTH), lambda i, j: (0, j)),
            pl.BlockSpec((1, 2 * KV_WIDTH_B), lambda i, j: (0, QKV4_COLS // (2 * KV_WIDTH_B))),
            pl.BlockSpec((tm, LANES), tab_map),
            pl.BlockSpec((tm, LANES), tab_map),
            pl.BlockSpec((tm, LANES), tab_map),
        ],
        out_specs=(pl.BlockSpec((WIDTH // LANES, tm, LANES), lambda i, j: (j, i, 0)),
                   pl.BlockSpec((2, tm, LANES), lambda i, j: (0, i, 0)),
                   pl.BlockSpec((1, WIDTH, tm), win_map),
                   pl.BlockSpec((1, WIDTH, tm), win_map),
                   pl.BlockSpec((1, 2 * KV_WIDTH_B, win_b), lambda i, j: (i // tiles_per_seq, 0, 0))),
        scratch_shapes=[pltpu.VMEM((tm, D_MODEL), BF16)],
        compiler_params=_cparams(("arbitrary", "arbitrary"), 48),
        name="qkv_proj",
    )(x, g_attn, w_in_bf, w_in_bf, b_in, b_in, cos, slo, shi)


def _rope_tables(pos):
    half = HEAD_DIM // 2
    inv = ROPE_THETA ** (-jnp.arange(half, dtype=F32) / half)
    ang = pos.astype(F32)[:, None] * inv[None, :]
    cos = jnp.tile(jnp.cos(ang), (1, LANES // half))
    sin = jnp.tile(jnp.sin(ang), (1, LANES // half))
    first = (jnp.arange(LANES) % HEAD_DIM) < half
    return cos, jnp.where(first, -sin, 0.0), jnp.where(first, 0.0, sin)


def _two_head_softmax_pv(q0, q1, k2, v2, valid2):
    qm = jnp.concatenate([q0, q1], axis=0).astype(BF16)
    s = lax.dot_general(qm, k2, (((1,), (1,)), ((), ())), preferred_element_type=F32)
    s = jnp.where(valid2, s, MASK_VALUE)
    m = jnp.max(s, axis=1, keepdims=True)
    p = jnp.exp(s - m)
    l = jnp.sum(p, axis=1, keepdims=True)
    o = jnp.dot(p.astype(BF16), v2, preferred_element_type=F32) / l
    lse = m + jnp.log(l)
    return o[:BLOCK], lse[:BLOCK], o[BLOCK:], lse[BLOCK:]


ATT_SB = BLOCK * max(d for _, d in DILATIONS)
ATT_UNITS = ATT_SB // BLOCK
ATT_UNROLL = 8


ATT_SUBSTEPS = (len(DILATIONS) + 1) * (ATT_UNITS // ATT_UNROLL)
N_SAMPLE_REFS = 12


def _attn_kernel(units_per_step, *refs):
    sample_in, rest = refs[:N_SAMPLE_REFS], refs[N_SAMPLE_REFS:]
    prompt_in, (oas_ref, obs_ref, oa_ref, ob_ref), scratch = rest[:11], rest[11:15], rest[15:]
    for j in range(units_per_step):
        _sample_unit(j, *sample_in, oas_ref, obs_ref)
    _attn_prompt_substep(*prompt_in, oa_ref, ob_ref, *scratch)


def _attn_prompt_substep(qa_ref, ka_ref, kap_ref, va_ref, vap_ref, qb_ref, kb_ref, kbp_ref, vb_ref, vbp_ref,
                         sink_ref, oa_ref, ob_ref, kcat, vcat, kbcat, vbcat, o_scr, lse_scr):
    has_prev = pl.program_id(1) > 0
    pair = pl.program_id(2)
    sub = pl.program_id(3)
    sb = ATT_SB

    @pl.when(sub == 0)
    def _():
        kcat[0:sb, :] = kap_ref[0]
        kcat[sb:2 * sb, :] = ka_ref[0]
        vcat[0:sb, :] = vap_ref[0]
        vcat[sb:2 * sb, :] = va_ref[0]
        kv_first = pair < (N_HEADS // N_KV_B) // 2
        lane_row = lax.broadcasted_iota(I32, (1, LANES), 1)
        kv_lanes = jnp.where(lane_row >= HEAD_DIM, 1, 0) == jnp.where(kv_first, 0, 1)
        both_halves = lambda x: jnp.where(kv_lanes, x, pltpu.roll(x, HEAD_DIM, 1))
        kbcat[0:BLOCK, :] = both_halves(kbp_ref[0])
        kbcat[BLOCK:BLOCK + sb, :] = both_halves(kb_ref[0])
        vbcat[0:BLOCK, :] = both_halves(vbp_ref[0])
        vbcat[BLOCK:BLOCK + sb, :] = both_halves(vb_ref[0])

    lane = lax.broadcasted_iota(I32, (BLOCK, LANES), 1)
    first = lane < HEAD_DIM
    qi = lax.broadcasted_iota(I32, (2 * BLOCK, 2 * BLOCK), 0) & (BLOCK - 1)
    si = lax.broadcasted_iota(I32, (2 * BLOCK, 2 * BLOCK), 1)
    dist = qi - si + BLOCK
    band = (dist >= 0) & (dist <= BLOCK)
    own_block = si >= BLOCK

    def rows(start, size, d):
        return pl.ds(start, size) if d == 1 else pl.ds(start, size, stride=d)

    def key_mask(blk):
        return band if blk > 0 else band & (own_block | has_prev)

    halves = ATT_UNITS // ATT_UNROLL
    for p, (_, d) in enumerate(DILATIONS):
        for half in range(halves):
            @pl.when(sub == p * halves + half)
            def _(p=p, d=d, half=half):
                for u in range(half * ATT_UNROLL, (half + 1) * ATT_UNROLL):
                    blk = u // d
                    qs = blk * (BLOCK * d) + u % d
                    q = qa_ref[0, rows(qs, BLOCK, d), :] * SCALE
                    k2 = kcat[rows(sb + qs - BLOCK * d, 2 * BLOCK, d), :].astype(BF16)
                    v2 = vcat[rows(sb + qs - BLOCK * d, 2 * BLOCK, d), :].astype(BF16)
                    o0, l0, o1, l1 = _two_head_softmax_pv(jnp.where(first, q, 0.0), jnp.where(first, 0.0, q),
                                                          k2, v2, key_mask(blk))
                    o_scr[p, rows(qs, BLOCK, d), :] = jnp.where(first, o0, o1)
                    lse_scr[p, rows(qs, BLOCK, d), :] = jnp.where(first, l0, l1)

    @pl.when(sub == len(DILATIONS) * halves - 1)
    def _():
        def mix(bk, carry):
            r = pl.ds(pl.multiple_of(bk * BLOCK, BLOCK), BLOCK)
            lses = [lse_scr[p, r, :] for p in range(len(DILATIONS))]
            m = functools.reduce(jnp.maximum, lses)
            ws = [jnp.exp(l - m) for l in lses]
            oa_ref[0, r, :] = sum(w * o_scr[p, r, :] for p, w in enumerate(ws)) / sum(ws)
            return carry
        lax.fori_loop(0, ATT_UNITS, mix, 0)

    for half in range(halves):
        @pl.when(sub == len(DILATIONS) * halves + half)
        def _(half=half):
            sink = sink_ref[...]
            for blk in range(half * ATT_UNROLL, (half + 1) * ATT_UNROLL):
                r = pl.ds(blk * BLOCK, BLOCK)
                r2 = pl.ds(blk * BLOCK, 2 * BLOCK)
                q = qb_ref[0, r, :] * SCALE
                k2 = kbcat[r2, :].astype(BF16)
                v2 = vbcat[r2, :].astype(BF16)
                o0, l0, o1, l1 = _two_head_softmax_pv(jnp.where(first, q, 0.0), jnp.where(first, 0.0, q),
                                                      k2, v2, key_mask(blk))
                lse = jnp.where(first, l0, l1)
                ob_ref[0, r, :] = jnp.where(first, o0, o1) * jax.nn.sigmoid(lse - sink)


def _attention(qkvc, kvbc, sinks_lanes, batch, seq, qkvc_s, kvbc_s, cache_a_k, cache_a_v, cache_b_k, cache_b_v, sinks):
    t = batch * seq
    assert seq % ATT_SB == 0
    n_sb = seq // ATT_SB
    n_pairs = WIDTH // LANES
    cur = lambda comp: (lambda b, s, p, q: (comp * n_pairs + p, b * n_sb + s, 0))
    prev = lambda comp: (lambda b, s, p, q: (comp * n_pairs + p, b * n_sb + jnp.maximum(s - 1, 0), 0))
    blk = lambda f: pl.BlockSpec((1, ATT_SB, LANES), f)
    kvb_cur = lambda c: pl.BlockSpec((1, ATT_SB, LANES), lambda b, s, p, q: (c, b * n_sb + s, 0))
    kvb_prev = lambda c: pl.BlockSpec(
        (1, BLOCK, LANES),
        lambda b, s, p, q: (c, jnp.maximum((b * n_sb + s) * ATT_UNITS - 1, b * n_sb * ATT_UNITS), 0))
    out_spec = pl.BlockSpec((1, ATT_SB, LANES), lambda b, s, p, q: (p, b * n_sb + s, 0))
    prompt_specs = [blk(cur(0)), blk(cur(1)), blk(prev(1)), blk(cur(2)), blk(prev(2)), blk(cur(3)),
                    kvb_cur(0), kvb_prev(0), kvb_cur(1), kvb_prev(1),
                    pl.BlockSpec((1, LANES), lambda b, s, p, q: (0, p))]

    nb = qkvc_s.shape[1]
    wa = cache_a_k.shape[1]
    hps = N_HEADS // N_KV_B
    n_units = nb * N_KV_B
    n_steps = batch * n_sb * n_pairs * ATT_SUBSTEPS
    assert n_units % n_steps == 0
    ups = n_units // n_steps
    assert cache_b_k.shape[1] == WINDOW_B and all(wa >= w and wa % d == 0 for w, d in DILATIONS)
    dist = wa - np.arange(wa)
    cnt = sum(((dist % d == 0) & (dist <= w)).astype(np.float32) for w, d in DILATIONS).reshape(1, wa)
    q4u = (qkvc_s.reshape(4, N_KV_B, n_pairs // N_KV_B, nb, LANES // HEAD_DIM, HEAD_DIM)
           .transpose(3, 1, 0, 2, 4, 5).reshape(n_units, 4, hps, HEAD_DIM))
    kvnu = (kvbc_s.reshape(2, nb, N_KV_B, HEAD_DIM).transpose(1, 2, 0, 3).reshape(n_units, 2, 1, HEAD_DIM))
    sink_u = jnp.tile(sinks.reshape(N_KV_B, hps, 1), (nb, 1, 1))
    to_units = lambda c: jnp.transpose(c, (0, 2, 3, 1)).reshape(n_units, -1, HEAD_DIM, c.shape[1])
    step = lambda b, s, p, q: ((b * n_sb + s) * n_pairs + p) * ATT_SUBSTEPS + q
    unit_blk = lambda shape: pl.BlockSpec((ups,) + shape, lambda b, s, p, q: (step(b, s, p, q),) + (0,) * len(shape))
    comp_blk = lambda c, shape: pl.BlockSpec((ups, 1) + shape,
                                             lambda b, s, p, q: (step(b, s, p, q), c) + (0,) * len(shape))
    sample_specs = [pl.BlockSpec((1, wa), lambda b, s, p, q: (0, 0)),
                    comp_blk(0, (hps, HEAD_DIM)), comp_blk(1, (hps, HEAD_DIM)), comp_blk(2, (hps, HEAD_DIM)),
                    comp_blk(3, (hps, HEAD_DIM)), comp_blk(0, (1, HEAD_DIM)), comp_blk(1, (1, HEAD_DIM)),
                    unit_blk((hps, 1)),
                    unit_blk((hps, HEAD_DIM, wa)), unit_blk((hps, HEAD_DIM, wa)),
                    unit_blk((1, HEAD_DIM, WINDOW_B)), unit_blk((1, HEAD_DIM, WINDOW_B))]
    assert len(sample_specs) == N_SAMPLE_REFS
    oas, obs, oa, ob = pl.pallas_call(
        functools.partial(_attn_kernel, ups),
        out_shape=(jax.ShapeDtypeStruct((n_units, hps, HEAD_DIM), F32),
                   jax.ShapeDtypeStruct((n_units, hps, HEAD_DIM), F32),
                   jax.ShapeDtypeStruct((n_pairs, t, LANES), F32),
                   jax.ShapeDtypeStruct((n_pairs, t, LANES), F32)),
        grid=(batch, n_sb, n_pairs, ATT_SUBSTEPS),
        in_specs=sample_specs + prompt_specs,
        out_specs=(unit_blk((hps, HEAD_DIM)), unit_blk((hps, HEAD_DIM)), out_spec, out_spec),
        scratch_shapes=[
            pltpu.VMEM((2 * ATT_SB, LANES), F32), pltpu.VMEM((2 * ATT_SB, LANES), F32),
            pltpu.VMEM((BLOCK + ATT_SB, LANES), F32), pltpu.VMEM((BLOCK + ATT_SB, LANES), F32),
            pltpu.VMEM((len(DILATIONS), ATT_SB, LANES), F32), pltpu.VMEM((len(DILATIONS), ATT_SB, LANES), F32)],
        compiler_params=_cparams(("parallel", "parallel", "arbitrary", "arbitrary"), 56),
        name="attention",
    )(jnp.asarray(cnt), q4u, q4u, q4u, q4u, kvnu, kvnu, sink_u,
      to_units(cache_a_k), to_units(cache_a_v), to_units(cache_b_k), to_units(cache_b_v),
      qkvc, qkvc, qkvc, qkvc, qkvc, qkvc, kvbc, kvbc, kvbc, kvbc, sinks_lanes)
    chunked = lambda o: o.reshape(nb, n_pairs, LANES).transpose(1, 0, 2)
    return oa, ob, chunked(oas), chunked(obs)


def _hi_dot(a, b, dims=(((1,), (0,)), ((), ()))):
    return lax.dot_general(a, b, dims, preferred_element_type=F32, precision=lax.Precision.HIGHEST)


def _sample_unit(j, cnt_ref, qa_ref, kn_ref, vn_ref, qb_ref, kbn_ref, vbn_ref, sink_ref,
                 kt_ref, vt_ref, kbt_ref, vbt_ref, oa_ref, ob_ref):
    hps = N_HEADS // N_KV_B
    nt = (((1,), (1,)), ((), ()))
    head = lax.broadcasted_iota(I32, (hps, 1), 0)
    cnt = cnt_ref[...]
    n_pat = float(len(DILATIONS))

    qa = qa_ref[j, 0] * SCALE
    qa_bf = qa.astype(BF16)
    s = jnp.zeros((hps, cnt.shape[1]), F32)
    for hl in range(hps):
        s_hl = jnp.dot(qa_bf, kt_ref[j, hl].astype(BF16), preferred_element_type=F32)
        s = jnp.where(head == hl, s_hl, s)
    s = jnp.where(cnt > 0.0, s, MASK_VALUE)
    s_new = jnp.sum(qa * kn_ref[j, 0], axis=1, keepdims=True)
    m = jnp.maximum(jnp.max(s, axis=1, keepdims=True), s_new)
    e = cnt * jnp.exp(s - m)
    e_new = n_pat * jnp.exp(s_new - m)
    den = jnp.sum(e, axis=1, keepdims=True) + e_new
    e_bf = e.astype(BF16)
    pv = jnp.zeros((hps, HEAD_DIM), F32)
    for hl in range(hps):
        o_hl = lax.dot_general(e_bf, vt_ref[j, hl].astype(BF16), nt, preferred_element_type=F32)
        pv = jnp.where(head == hl, o_hl, pv)
    oa_ref[j] = (pv + e_new * vn_ref[j, 0]) / den

    qb = qb_ref[j, 0] * SCALE
    sb = _hi_dot(qb, kbt_ref[j, 0])
    sn = jnp.sum(qb * kbn_ref[j, 0], axis=1, keepdims=True)
    mm = jnp.maximum(jnp.max(sb, axis=1, keepdims=True), sn)
    eb = jnp.exp(sb - mm)
    en = jnp.exp(sn - mm)
    l = jnp.sum(eb, axis=1, keepdims=True) + en
    ob = (_hi_dot(eb, vbt_ref[j, 0], nt) + en * vbn_ref[j, 0]) / l
    lse = mm + jnp.log(l)
    ob_ref[j] = ob * jax.nn.sigmoid(lse - sink_ref[j])


def _split_bf16(x):
    hi = x.astype(BF16)
    lo = (x - hi.astype(F32)).astype(BF16)
    return hi, lo


def _post_first_kernel(n_valid, *refs):
    i = pl.program_id(0)

    @pl.when(i < n_valid)
    def _():
        _post_body(*refs)

    @pl.when(i >= n_valid)
    def _():
        for o_ref in refs[-4:]:
            o_ref[...] = jnp.zeros(o_ref.shape, o_ref.dtype)


def _post_append_kernel(x1_all, h2_all, ridx_all, rgate_all, *refs):
    del x1_all, h2_all, ridx_all, rgate_all
    _post_body(*refs)


def _post_body(oa_ref, ob_ref, x_ref, ga_ref, gb_ref, w_ref, b_ref, gf_ref, wr_ref, br_ref,
               x1_ref, h2_ref, ridx_ref, rgate_ref):
    unchunk = lambda ref: jnp.concatenate([ref[c] for c in range(ref.shape[0])], axis=1)
    na = _rmsnorm(unchunk(oa_ref), ga_ref[...]).astype(BF16)
    nb = _rmsnorm(unchunk(ob_ref), gb_ref[...]).astype(BF16)
    c = jnp.concatenate([na, nb], axis=1)
    x1 = x_ref[...] + jnp.dot(c, w_ref[...], preferred_element_type=F32) + b_ref[...]
    x1_ref[...] = x1
    h2 = _rmsnorm(x1, gf_ref[...])
    h2_ref[...] = h2

    h_hi, h_lo = _split_bf16(h2)
    w_hi, w_lo = _split_bf16(wr_ref[...])
    dot = lambda a, b: jnp.dot(a, b, preferred_element_type=F32)
    logits = dot(h_hi, w_hi) + (dot(h_hi, w_lo) + dot(h_lo, w_hi)) + br_ref[...]

    tm = logits.shape[0]
    eidx = lax.broadcasted_iota(I32, (tm, N_EXPERTS), 1)
    lane = lax.broadcasted_iota(I32, (tm, LANES), 1)
    work = logits
    vals, idxs = [], []
    for _ in range(TOP_K):
        v = jnp.max(work, axis=1, keepdims=True)
        i = jnp.min(jnp.where(work == v, eidx, N_EXPERTS), axis=1, keepdims=True)
        vals.append(v)
        idxs.append(i)
        work = jnp.where(eidx == i, -jnp.inf, work)
    es = [jnp.exp(v - vals[0]) for v in vals]
    den = sum(es)
    ridx = jnp.zeros((tm, LANES), I32)
    rgate = jnp.zeros((tm, LANES), F32)
    for k in range(TOP_K):
        ridx = jnp.where(lane == k, idxs[k], ridx)
        rgate = jnp.where(lane == k, es[k] / den, rgate)
    ridx_ref[...] = ridx
    rgate_ref[...] = rgate


def _post_attention(oa, ob, x, g_out_a, g_out_b, w_out_bf, b_out, g_ffn, w_router, b_router, tm,
                    out_rows=None, append_to=None, row0=0):
    t = x.shape[0]
    n_valid = t // tm
    n_pairs = WIDTH // LANES
    clamp = lambda i: jnp.minimum(i, n_valid - 1)
    chunked = pl.BlockSpec((n_pairs, tm, LANES), lambda i: (0, clamp(i), 0))
    full = lambda r, c: pl.BlockSpec((r, c), lambda i: (0, 0))
    in_specs = [chunked, chunked, pl.BlockSpec((tm, D_MODEL), lambda i: (clamp(i), 0)),
                full(1, WIDTH), full(1, WIDTH), full(D_MODEL, D_MODEL), full(1, D_MODEL), full(1, D_MODEL),
                full(D_MODEL, N_EXPERTS), full(1, N_EXPERTS)]
    args = (oa, ob, x, g_out_a, g_out_b, w_out_bf, b_out, g_ffn, w_router, b_router)
    if append_to is None:
        rows = t if out_rows is None else out_rows
        assert rows % tm == 0 and t % tm == 0
        kernel_fn, aliases, blk0 = functools.partial(_post_first_kernel, n_valid), {}, 0
    else:
        rows = append_to[0].shape[0]
        assert row0 % tm == 0 and t % tm == 0
        kernel_fn, aliases, blk0 = _post_append_kernel, {k: k for k in range(4)}, row0 // tm
        in_specs = [pl.BlockSpec(memory_space=pl.ANY)] * 4 + in_specs
        args = tuple(append_to) + args
    n_steps = rows // tm if append_to is None else n_valid
    outblk = lambda w: pl.BlockSpec((tm, w), lambda i: (blk0 + i, 0))
    return pl.pallas_call(
        kernel_fn,
        out_shape=(jax.ShapeDtypeStruct((rows, D_MODEL), F32), jax.ShapeDtypeStruct((rows, D_MODEL), F32),
                   jax.ShapeDtypeStruct((rows, LANES), I32), jax.ShapeDtypeStruct((rows, LANES), F32)),
        grid=(n_steps,),
        in_specs=in_specs,
        out_specs=(outblk(D_MODEL), outblk(D_MODEL), outblk(LANES), outblk(LANES)),
        input_output_aliases=aliases,
        compiler_params=_cparams(("arbitrary",), 48),
        name="post_attention",
    )(*args)


def _dispatch_kernel(dest_ref, h_hbm, o_ref, tok_ref, buf, sem):
    i = pl.program_id(0)

    @pl.when(i == 0)
    def _():
        def clear(p, carry):
            tok_ref[p] = 0
            return carry
        lax.fori_loop(0, tok_ref.shape[0], clear, 0, unroll=8)

        def place(t, carry):
            for k in range(TOP_K):
                tok_ref[dest_ref[t * TOP_K + k]] = t
            return carry
        lax.fori_loop(0, dest_ref.shape[0] // TOP_K, place, 0, unroll=4)

    def issue(step, slot):
        base = step * DISPATCH_ROWS

        def body(r, carry):
            pltpu.make_async_copy(h_hbm.at[pl.ds(tok_ref[base + r], 1)],
                                  buf.at[slot, pl.ds(r, 1)], sem.at[slot]).start()
            return carry
        lax.fori_loop(0, DISPATCH_ROWS, body, 0, unroll=8)

    @pl.when(i == 0)
    def _():
        issue(0, 0)

    @pl.when(i + 1 < pl.num_programs(0))
    def _():
        issue(i + 1, (i + 1) % 2)

    slot = i % 2
    pltpu.make_async_copy(h_hbm.at[pl.ds(0, DISPATCH_ROWS)], buf.at[slot], sem.at[slot]).wait()
    o_ref[...] = buf[slot].astype(BF16)


def _dispatch(dest, h2, p):
    return pl.pallas_call(
        _dispatch_kernel,
        out_shape=jax.ShapeDtypeStruct((p, D_MODEL), BF16),
        grid_spec=pltpu.PrefetchScalarGridSpec(
            num_scalar_prefetch=1,
            grid=(p // DISPATCH_ROWS,),
            in_specs=[pl.BlockSpec(memory_space=pl.ANY)],
            out_specs=pl.BlockSpec((DISPATCH_ROWS, D_MODEL), lambda i, dst: (i, 0)),
            scratch_shapes=[pltpu.SMEM((p,), I32),
                            pltpu.VMEM((2, DISPATCH_ROWS, D_MODEL), F32), pltpu.SemaphoreType.DMA((2,))],
        ),
        compiler_params=_cparams(("arbitrary",), 32),
        name="moe_dispatch",
    )(dest, h2)


def _moe_kernel(ie_ref, iblk_ref, insub_ref, iact_ref, used_ref,
                x_hbm, wg_ref, wu_ref, wd_ref, bg_ref, bu_ref, bd_ref, y_hbm,
                xbuf, act, ystage, xsem, ysem):
    it = pl.program_id(0)
    s = pl.program_id(1)
    nsub = insub_ref[it]
    row0 = iblk_ref[it] * MOE_SUB

    xslot = it % 2

    def x_copies(item, op):
        first = iblk_ref[item] * MOE_SUB

        def body(j, c):
            cp = pltpu.make_async_copy(x_hbm.at[pl.ds(first + j * MOE_SUB, MOE_SUB)],
                                       xbuf.at[item % 2, pl.ds(j * MOE_SUB, MOE_SUB)], xsem.at[item % 2])
            cp.start() if op == "start" else cp.wait()
            return c
        lax.fori_loop(0, insub_ref[item], body, 0)

    def for_row_chunks(fn):
        big = nsub // 4

        def body(j, c):
            fn(pl.multiple_of(j * (4 * MOE_SUB), 4 * MOE_SUB), 4 * MOE_SUB)
            return c
        lax.fori_loop(0, big, body, 0)
        rem = nsub - 4 * big
        has2 = rem >= 2

        @pl.when(has2)
        def _():
            fn(pl.multiple_of(big * (4 * MOE_SUB), MOE_SUB), 2 * MOE_SUB)

        @pl.when((rem & 1) == 1)
        def _():
            fn(pl.multiple_of((4 * big + jnp.where(has2, 2, 0)) * MOE_SUB, MOE_SUB), MOE_SUB)

    @pl.when((s == 0) & (it == 0))
    def _():
        x_copies(0, "start")

    @pl.when(s == 0)
    def _():
        x_copies(it, "wait")

    @pl.when((s == 1) & (it + 1 < pl.num_programs(0)))
    def _():
        x_copies(it + 1, "start")

    @pl.when((s < MOE_F) & (nsub > 0))
    def _():
        bg, bu = bg_ref[...], bu_ref[...]

        def gate_up(r0, rows):
            x = xbuf[xslot, pl.ds(r0, rows), :]
            g = jnp.dot(x, wg_ref[...].astype(BF16), preferred_element_type=F32) + bg
            u = jnp.dot(x, wu_ref[...].astype(BF16), preferred_element_type=F32) + bu
            g = jnp.minimum(g, SWIGLU_LIMIT)
            u = jnp.clip(u, -SWIGLU_LIMIT, SWIGLU_LIMIT)
            a = g * jax.nn.sigmoid(SWIGLU_ALPHA * g) * (u + 1.0)
            act[s, pl.ds(r0, rows), :] = a.astype(BF16)
        for_row_chunks(gate_up)

    @pl.when((s >= MOE_F) & (nsub > 0))
    def _():
        n = s - MOE_F
        slot = n % 2
        bd = bd_ref[...]

        def y_copy(step, r0, rows):
            c0 = pl.multiple_of(step * MOE_TF, MOE_TF)
            return pltpu.make_async_copy(
                ystage.at[step % 2, pl.ds(r0, rows), :],
                y_hbm.at[pl.ds(row0 + r0, rows), pl.ds(c0, MOE_TF)], ysem.at[step % 2])

        def down(r0, rows):
            a = jnp.concatenate([act[f, pl.ds(r0, rows), :] for f in range(MOE_F)], axis=1)
            ystage[slot, pl.ds(r0, rows), :] = (
                jnp.dot(a, wd_ref[...].astype(BF16), preferred_element_type=F32) + bd)
            y_copy(n, r0, rows).start()
        for_row_chunks(down)

        @pl.when(n > 0)
        def _():
            for_row_chunks(lambda r0, rows: y_copy(n - 1, r0, rows).wait())

        @pl.when(n == MOE_F - 1)
        def _():
            for_row_chunks(lambda r0, rows: y_copy(n, r0, rows).wait())

    @pl.when((it == pl.num_programs(0) - 1) & (s == pl.num_programs(1) - 1))
    def _():
        n_blocks = y_hbm.shape[0] // MOE_SUB
        ystage[0, pl.ds(0, MOE_SUB), :] = jnp.zeros((MOE_SUB, MOE_TF), F32)

        def z_copy(b, n):
            return pltpu.make_async_copy(
                ystage.at[0, pl.ds(0, MOE_SUB), :],
                y_hbm.at[pl.ds(b * MOE_SUB, MOE_SUB), pl.ds(n * MOE_TF, MOE_TF)], ysem.at[0])

        def fill(b, c):
            for n in range(MOE_F):
                z_copy(b, n).start()
            for n in range(MOE_F):
                z_copy(b, n).wait()
            return c
        lax.fori_loop(used_ref[0], n_blocks, fill, 0)


def _moe_experts(items, x_sorted, w_gate_up, b_gate_up, w_down, b_down):
    item_e, item_blk, item_nsub, item_act, used_blocks = items
    n_items = item_e.shape[0]
    p = x_sorted.shape[0]
    rmax = MOE_MAX_SUB * MOE_SUB
    last = MOE_F - 1
    f_of = lambda s, act: jnp.where(act == 1, jnp.minimum(s, last), last)
    n_of = lambda s, act: jnp.where(act == 1, jnp.maximum(s - MOE_F, 0), last)
    wg_map = lambda i, s, ie, ib, ins, ia, iu: (ie[i], 0, f_of(s, ia[i]))
    wu_map = lambda i, s, ie, ib, ins, ia, iu: (ie[i], 0, MOE_F + f_of(s, ia[i]))
    wd_map = lambda i, s, ie, ib, ins, ia, iu: (ie[i], 0, n_of(s, ia[i]))
    bgu = b_gate_up.reshape(N_EXPERTS, 1, 2 * D_FF)
    bd = b_down.reshape(N_EXPERTS, 1, D_MODEL)
    return pl.pallas_call(
        _moe_kernel,
        out_shape=jax.ShapeDtypeStruct((p, D_MODEL), F32),
        grid_spec=pltpu.PrefetchScalarGridSpec(
            num_scalar_prefetch=5,
            grid=(n_items, 2 * MOE_F),
            in_specs=[
                pl.BlockSpec(memory_space=pl.ANY),
                pl.BlockSpec((None, D_MODEL, MOE_TF), wg_map),
                pl.BlockSpec((None, D_MODEL, MOE_TF), wu_map),
                pl.BlockSpec((None, D_FF, MOE_TF), wd_map),
                pl.BlockSpec((None, 1, MOE_TF), wg_map),
                pl.BlockSpec((None, 1, MOE_TF), wu_map),
                pl.BlockSpec((None, 1, MOE_TF), wd_map),
            ],
            out_specs=pl.BlockSpec(memory_space=pl.ANY),
            scratch_shapes=[
                pltpu.VMEM((2, rmax, D_MODEL), BF16),
                pltpu.VMEM((MOE_F, rmax, MOE_TF), BF16),
                pltpu.VMEM((2, rmax, MOE_TF), F32),
                pltpu.SemaphoreType.DMA((2,)),
                pltpu.SemaphoreType.DMA((2,)),
            ],
        ),
        compiler_params=_cparams(("arbitrary", "arbitrary"), 56),
        name="moe_experts",
    )(item_e, item_blk, item_nsub, item_act, used_blocks, x_sorted, w_gate_up, w_gate_up, w_down, bgu, bgu, bd)


def _tail_kernel(n_first, dest_ref, y_hbm, gate_ref, x1_ref, p_ref, wg_ref, wp_ref, gp_ref, gf_ref,
                 o1_ref, o2_ref, ybuf, sem):
    i = pl.program_id(0)

    def issue(step, slot):
        base = step * (TAIL_ROWS * TOP_K)

        def body(r, carry):
            for k in range(TOP_K):
                pltpu.make_async_copy(y_hbm.at[pl.ds(dest_ref[base + r * TOP_K + k], 1)],
                                      ybuf.at[slot, k, pl.ds(r, 1)], sem.at[slot]).start()
            return carry
        lax.fori_loop(0, TAIL_ROWS, body, 0, unroll=2)

    @pl.when(i == 0)
    def _():
        issue(0, 0)

    @pl.when(i + 1 < pl.num_programs(0))
    def _():
        issue(i + 1, (i + 1) % 2)

    slot = i % 2
    for k in range(TOP_K):
        pltpu.make_async_copy(y_hbm.at[pl.ds(0, TAIL_ROWS)], ybuf.at[slot, k], sem.at[slot]).wait()

    gate = gate_ref[...]
    moe = ybuf[slot, 0] * gate[:, 0:1]
    for k in range(1, TOP_K):
        moe = moe + ybuf[slot, k] * gate[:, k:k + 1]
    x2 = x1_ref[...] + moe
    h3 = _rmsnorm(x2, gp_ref[...]).astype(BF16)
    ple_gate = jax.nn.sigmoid(jnp.dot(h3, wg_ref[...], preferred_element_type=F32))
    ple = jnp.dot(p_ref[...].astype(BF16), wp_ref[...], preferred_element_type=F32)
    x3 = x2 + ple_gate * ple
    y = _rmsnorm(x3, gf_ref[...])

    @pl.when(i < n_first)
    def _():
        o1_ref[...] = y

    @pl.when(i >= n_first)
    def _():
        o2_ref[...] = y


def _tail(dest, y_sorted, rgate, x1, p_all, w_ple_gate_bf, w_ple_proj_bf, g_ple, g_final, t_first):
    t = dest.shape[0] // TOP_K
    assert t_first % TAIL_ROWS == 0 and t % TAIL_ROWS == 0
    n_first = t_first // TAIL_ROWS
    rowblk = lambda w: pl.BlockSpec((TAIL_ROWS, w), lambda i, d: (i, 0))
    full = lambda r, c: pl.BlockSpec((r, c), lambda i, d: (0, 0))
    return pl.pallas_call(
        functools.partial(_tail_kernel, n_first),
        out_shape=(jax.ShapeDtypeStruct((t_first, D_MODEL), F32),
                   jax.ShapeDtypeStruct((t - t_first, D_MODEL), F32)),
        grid_spec=pltpu.PrefetchScalarGridSpec(
            num_scalar_prefetch=1,
            grid=(t // TAIL_ROWS,),
            in_specs=[
                pl.BlockSpec(memory_space=pl.ANY),
                rowblk(LANES), rowblk(D_MODEL), rowblk(PLE_DIM),
                full(D_MODEL, D_MODEL), full(PLE_DIM, D_MODEL), full(1, D_MODEL), full(1, D_MODEL)],
            out_specs=(
                pl.BlockSpec((TAIL_ROWS, D_MODEL), lambda i, d: (jnp.minimum(i, n_first - 1), 0)),
                pl.BlockSpec((TAIL_ROWS, D_MODEL), lambda i, d: (jnp.maximum(i - n_first, 0), 0))),
            scratch_shapes=[pltpu.VMEM((2, TOP_K, TAIL_ROWS, D_MODEL), F32), pltpu.SemaphoreType.DMA((2,))],
        ),
        compiler_params=_cparams(("arbitrary",), 48),
        name="moe_tail",
    )(dest, y_sorted, rgate, x1, p_all, w_ple_gate_bf, w_ple_proj_bf, g_ple, g_final)


def _routing(ridx, rgate_unused, t):
    del rgate_unused
    m = t * TOP_K
    flat_e = ridx[:, :TOP_K].reshape(m)
    onehot = (flat_e[:, None] == jnp.arange(N_EXPERTS, dtype=I32)[None, :]).astype(I32)
    csum = jnp.cumsum(onehot, axis=0)
    rank = jnp.sum((csum - onehot) * onehot, axis=1)
    counts = csum[-1]
    nblk = (counts + MOE_SUB - 1) // MOE_SUB
    blk_end = jnp.cumsum(nblk)
    blk_start = blk_end - nblk
    dest = blk_start[flat_e] * MOE_SUB + rank
    n_blocks = -(-m // MOE_SUB) + N_EXPERTS
    n_blocks = -(-n_blocks // (DISPATCH_ROWS // MOE_SUB)) * (DISPATCH_ROWS // MOE_SUB)
    p = n_blocks * MOE_SUB

    n_items = (n_blocks + (MOE_MAX_SUB - 1) * N_EXPERTS) // MOE_MAX_SUB
    items_per_e = (nblk + MOE_MAX_SUB - 1) // MOE_MAX_SUB
    item_end = jnp.cumsum(items_per_e)
    total = item_end[-1]
    ids = jnp.arange(n_items, dtype=I32)
    active = ids < total
    ids_c = jnp.minimum(ids, total - 1)
    item_e = jnp.searchsorted(item_end, ids_c, side="right").astype(I32)
    local = ids_c - (item_end - items_per_e)[item_e]
    item_blk = blk_start[item_e] + local * MOE_MAX_SUB
    item_nsub = jnp.where(active, jnp.minimum(MOE_MAX_SUB, nblk[item_e] - local * MOE_MAX_SUB), 0)
    items = (item_e, item_blk.astype(I32), item_nsub.astype(I32), active.astype(I32),
             blk_end[-1:].astype(I32))
    return p, dest.astype(I32), items


def kernel(x_prompt, x_sample, cache_a_k, cache_a_v, cache_b_k, cache_b_v, p_prompt, p_sample, g_attn, w_in, b_in, sinks, g_out_a, g_out_b, w_out, b_out, g_ffn, w_router, b_router, w_gate_up, b_gate_up, w_down, b_down, g_ple, w_ple_gate, w_ple_proj, g_final):
    batch, seq, _ = x_prompt.shape
    nb, nseq, _ = x_sample.shape
    assert nseq == 1 and g_attn.shape[0] == 1
    tp, ts = batch * seq, nb * nseq
    row = lambda v: v.reshape(1, -1)

    w_in_bf = _cast_bf16(w_in[0])
    w_out_bf = _cast_bf16(w_out[0])
    w_pg_bf = _cast_bf16(w_ple_gate[0])
    w_pp_bf = _cast_bf16(w_ple_proj[0])

    tabs_p = _rope_tables(jnp.arange(seq, dtype=I32))
    tabs_s = _rope_tables(jnp.full((ts,), PAST_LEN, I32))
    sinks_lanes = jnp.repeat(sinks[0], HEAD_DIM).reshape(1, WIDTH)

    tm_p = 512
    xp = x_prompt.reshape(tp, D_MODEL)
    xs = x_sample.reshape(ts, D_MODEL)
    win_a, win_b = min(DILATIONS[-1][0], seq), min(WINDOW_B, seq)
    qkvc_p, kvbc_p, kt_p, vt_p, kvbt_p = _qkv_proj(xp, row(g_attn), w_in_bf, row(b_in), tabs_p, tm_p,
                                                    batch, win_a, win_b)
    qkvc_s, kvbc_s, kt_s, vt_s, kvbt_s = _qkv_proj(xs, row(g_attn), w_in_bf, row(b_in), tabs_s, ts, 1, ts, ts)

    oa_p, ob_p, oa_s, ob_s = _attention(qkvc_p, kvbc_p, sinks_lanes, batch, seq, qkvc_s, kvbc_s,
                                        cache_a_k[0], cache_a_v[0], cache_b_k[0], cache_b_v[0], sinks[0])

    post = functools.partial(_post_attention, g_out_a=row(g_out_a), g_out_b=row(g_out_b), w_out_bf=w_out_bf,
                             b_out=row(b_out), g_ffn=row(g_ffn), w_router=w_router[0], b_router=row(b_router))
    t = tp + ts
    tm_post = 256
    merged = post(oa_p, ob_p, xp, tm=tm_post, out_rows=-(-t // tm_post) * tm_post)
    x1, h2, ridx, rgate = post(oa_s, ob_s, xs, tm=ts, append_to=merged, row0=tp)

    cat = lambda a, b: jnp.concatenate([a, b], axis=0)
    n_slots, dest, items = _routing(ridx[:t], rgate, t)
    x_sorted = _dispatch(dest, h2, n_slots)
    y_sorted = _moe_experts(items, x_sorted, w_gate_up[0], b_gate_up[0], w_down[0], b_down[0])
    p_all = cat(p_prompt[0].reshape(tp, PLE_DIM), p_sample[0].reshape(ts, PLE_DIM))
    y_p, y_s = _tail(dest, y_sorted, rgate, x1, p_all, w_pg_bf, w_pp_bf, row(g_ple), row(g_final), tp)

    y_prompt = y_p.reshape(batch, seq, D_MODEL)
    y_sample = y_s.reshape(nb, nseq, D_MODEL)

    def rows_out(t3, heads):
        n, _, rows = t3.shape
        return t3.reshape(n, heads, HEAD_DIM, rows).transpose(0, 3, 1, 2)[None]

    def sample_out(t3, heads):
        return t3.reshape(heads, HEAD_DIM, nb).transpose(2, 0, 1)[None, :, None]

    return (y_prompt, y_sample,
            rows_out(kt_p, N_HEADS), rows_out(vt_p, N_HEADS),
            rows_out(kvbt_p[:, :KV_WIDTH_B], N_KV_B), rows_out(kvbt_p[:, KV_WIDTH_B:], N_KV_B),
            sample_out(kt_s, N_HEADS), sample_out(vt_s, N_HEADS),
            sample_out(kvbt_s[:, :KV_WIDTH_B], N_KV_B), sample_out(kvbt_s[:, KV_WIDTH_B:], N_KV_B))
```

```python
import functools

import jax
import jax.numpy as jnp
import numpy as np
from jax import lax
from jax.experimental import pallas as pl
from jax.experimental.pallas import tpu as pltpu

F32 = jnp.float32
BF16 = jnp.bfloat16
I32 = jnp.int32

D_MODEL = 2048
HEAD_DIM = 64
N_HEADS = 16
WIDTH = N_HEADS * HEAD_DIM
N_KV_B = 2
KV_WIDTH_B = N_KV_B * HEAD_DIM
DILATIONS = ((128, 1), (512, 4), (2048, 16))
WINDOW_B = 128
BLOCK = 128
N_EXPERTS = 32
TOP_K = 4
D_FF = 2048
SWIGLU_ALPHA = 1.702
SWIGLU_LIMIT = 7.0
PLE_DIM = 256
ROPE_THETA = 10000.0
NORM_EPS = 1e-5
MASK_VALUE = -1e30
SCALE = HEAD_DIM ** -0.5
PAST_LEN = 16384
QKV4_COLS = 4 * WIDTH
IN_COLS = QKV4_COLS + 2 * KV_WIDTH_B

LANES = 128
VMEM_LIMIT_CAP = 60 * 1024 * 1024

MOE_TF = 512
MOE_F = D_FF // MOE_TF
MOE_SUB = 128
MOE_MAX_SUB = 12
DISPATCH_ROWS = 512
TAIL_ROWS = 128


def _cparams(semantics, vmem_mb):
    return pltpu.CompilerParams(
        dimension_semantics=semantics,
        vmem_limit_bytes=min(vmem_mb * 1024 * 1024, VMEM_LIMIT_CAP))


def _rmsnorm(x, g):
    ms = jnp.mean(x * x, axis=-1, keepdims=True)
    return x * lax.rsqrt(ms + NORM_EPS) * g


def _cast_kernel(x_ref, o_ref):
    o_ref[...] = x_ref[...].astype(o_ref.dtype)


def _cast_bf16(w, rows_per_step=256):
    r, c = w.shape
    return pl.pallas_call(
        _cast_kernel,
        out_shape=jax.ShapeDtypeStruct((r, c), BF16),
        grid=(r // rows_per_step,),
        in_specs=[pl.BlockSpec((rows_per_step, c), lambda i: (i, 0))],
        out_specs=pl.BlockSpec((rows_per_step, c), lambda i: (i, 0)),
        compiler_params=_cparams(("parallel",), 32),
        name="cast_bf16",
    )(w)


def _rope(yc, cos, sin_lo, sin_hi):
    return yc * cos + pltpu.roll(yc, LANES - 32, 1) * sin_lo + pltpu.roll(yc, 32, 1) * sin_hi


def _qkv_kernel(tiles_per_seq, first_win_tile, win_b,
                x_ref, g_ref, w_ref, wkv_ref, b_ref, bkv_ref, cos_ref, slo_ref, shi_ref,
                qkv_ref, kvb_ref, kt_ref, vt_ref, kvbt_ref, h_scr):
    i = pl.program_id(0)
    j = pl.program_id(1)
    tile_in_seq = i % tiles_per_seq
    in_window = tile_in_seq >= first_win_tile
    n_chunks = WIDTH // LANES
    chunk = lambda v, c: v[:, c * LANES:(c + 1) * LANES]

    @pl.when(j == 0)
    def _():
        h_scr[...] = _rmsnorm(x_ref[...], g_ref[...]).astype(BF16)

    h = h_scr[...]
    cos, slo, shi = cos_ref[...], slo_ref[...], shi_ref[...]
    is_v = j == 2
    for c2 in range(n_chunks // 2):
        cols = slice(2 * c2 * LANES, 2 * (c2 + 1) * LANES)
        y2 = jnp.dot(h, w_ref[:, cols], preferred_element_type=F32) + b_ref[:, cols]
        for k in range(2):
            yc = chunk(y2, k)
            qkv_ref[2 * c2 + k] = jnp.where(is_v, yc, _rope(yc, cos, slo, shi))

    @pl.when((j == 1) & in_window)
    def _():
        for c in range(n_chunks):
            kt_ref[0, c * LANES:(c + 1) * LANES, :] = qkv_ref[c].T

    @pl.when(is_v & in_window)
    def _():
        for c in range(n_chunks):
            vt_ref[0, c * LANES:(c + 1) * LANES, :] = qkv_ref[c].T

    @pl.when(j == 3)
    def _():
        ykv = jnp.dot(h, wkv_ref[...], preferred_element_type=F32) + bkv_ref[...]
        kvb_ref[0] = _rope(chunk(ykv, 0), cos, slo, shi)
        kvb_ref[1] = chunk(ykv, 1)

    @pl.when((j == 3) & (tile_in_seq == tiles_per_seq - 1))
    def _():
        tm = kvb_ref.shape[1]
        kvbt_ref[0, :KV_WIDTH_B, :] = kvb_ref[0, tm - win_b:, :].T
        kvbt_ref[0, KV_WIDTH_B:, :] = kvb_ref[1, tm - win_b:, :].T


def _qkv_proj(x, g_attn, w_in_bf, b_in, rope_tabs, tm, n_seq, win_a, win_b):
    t = x.shape[0]
    seq = t // n_seq
    tiles_per_seq = seq // tm
    assert seq % tm == 0 and win_a % tm == 0 and win_a <= seq and win_b <= tm
    first_win_tile = tiles_per_seq - win_a // tm
    cos, slo, shi = rope_tabs
    tab_blocks = cos.shape[0] // tm
    tab_map = lambda i, j: (i % tab_blocks, 0)
    win_map = lambda i, j: (i // tiles_per_seq, 0, jnp.maximum(i % tiles_per_seq - first_win_tile, 0))
    return pl.pallas_call(
        functools.partial(_qkv_kernel, tiles_per_seq, first_win_tile, win_b),
        out_shape=(jax.ShapeDtypeStruct((QKV4_COLS // LANES, t, LANES), F32),
                   jax.ShapeDtypeStruct((2, t, LANES), F32),
                   jax.ShapeDtypeStruct((n_seq, WIDTH, win_a), F32),
                   jax.ShapeDtypeStruct((n_seq, WIDTH, win_a), F32),
                   jax.ShapeDtypeStruct((n_seq, 2 * KV_WIDTH_B, win_b), F32)),
        grid=(t // tm, 4),
        in_specs=[
            pl.BlockSpec((tm, D_MODEL), lambda i, j: (i, 0)),
            pl.BlockSpec((1, D_MODEL), lambda i, j: (0, 0)),
            pl.BlockSpec((D_MODEL, WIDTH), lambda i, j: (0, j)),
            pl.BlockSpec((D_MODEL, 2 * KV_WIDTH_B), lambda i, j: (0, QKV4_COLS // (2 * KV_WIDTH_B))),
            pl.BlockSpec((1, WIDTH), lambda i, j: (0, j)),
            pl.BlockSpec((1, 2 * KV_WIDTH_B), lambda i, j: (0, QKV4_COLS // (2 * KV_WIDTH_B))),
            pl.BlockSpec((tm, LANES), tab_map),
            pl.BlockSpec((tm, LANES), tab_map),
            pl.BlockSpec((tm, LANES), tab_map),
        ],
        out_specs=(pl.BlockSpec((WIDTH // LANES, tm, LANES), lambda i, j: (j, i, 0)),
                   pl.BlockSpec((2, tm, LANES), lambda i, j: (0, i, 0)),
                   pl.BlockSpec((1, WIDTH, tm), win_map),
                   pl.BlockSpec((1, WIDTH, tm), win_map),
                   pl.BlockSpec((1, 2 * KV_WIDTH_B, win_b), lambda i, j: (i // tiles_per_seq, 0, 0))),
        scratch_shapes=[pltpu.VMEM((tm, D_MODEL), BF16)],
        compiler_params=_cparams(("arbitrary", "arbitrary"), 48),
        name="qkv_proj",
    )(x, g_attn, w_in_bf, w_in_bf, b_in, b_in, cos, slo, shi)


def _rope_tables(pos):
    half = HEAD_DIM // 2
    inv = ROPE_THETA ** (-jnp.arange(half, dtype=F32) / half)
    ang = pos.astype(F32)[:, None] * inv[None, :]
    cos = jnp.tile(jnp.cos(ang), (1, LANES // half))
    sin = jnp.tile(jnp.sin(ang), (1, LANES // half))
    first = (jnp.arange(LANES) % HEAD_DIM) < half
    return cos, jnp.where(first, -sin, 0.0), jnp.where(first, 0.0, sin)


def _two_head_softmax_pv(q0, q1, k2, v2, valid2):
    qm = jnp.concatenate([q0, q1], axis=0).astype(BF16)
    s = lax.dot_general(qm, k2, (((1,), (1,)), ((), ())), preferred_element_type=F32)
    s = jnp.where(valid2, s, MASK_VALUE)
    m = jnp.max(s, axis=1, keepdims=True)
    p = jnp.exp(s - m)
    l = jnp.sum(p, axis=1, keepdims=True)
    o = jnp.dot(p.astype(BF16), v2, preferred_element_type=F32) / l
    lse = m + jnp.log(l)
    return o[:BLOCK], lse[:BLOCK], o[BLOCK:], lse[BLOCK:]


ATT_SB = BLOCK * max(d for _, d in DILATIONS)
ATT_UNITS = ATT_SB // BLOCK
ATT_UNROLL = 8


ATT_SUBSTEPS = (len(DILATIONS) + 1) * (ATT_UNITS // ATT_UNROLL)
N_SAMPLE_REFS = 12


def _attn_kernel(units_per_step, *refs):
    sample_in, rest = refs[:N_SAMPLE_REFS], refs[N_SAMPLE_REFS:]
    prompt_in, (oas_ref, obs_ref, oa_ref, ob_ref), scratch = rest[:11], rest[11:15], rest[15:]
    for j in range(units_per_step):
        _sample_unit(j, *sample_in, oas_ref, obs_ref)
    _attn_prompt_substep(*prompt_in, oa_ref, ob_ref, *scratch)


def _attn_prompt_substep(qa_ref, ka_ref, kap_ref, va_ref, vap_ref, qb_ref, kb_ref, kbp_ref, vb_ref, vbp_ref,
                         sink_ref, oa_ref, ob_ref, kcat, vcat, kbcat, vbcat, o_scr, lse_scr):
    has_prev = pl.program_id(1) > 0
    pair = pl.program_id(2)
    sub = pl.program_id(3)
    sb = ATT_SB

    @pl.when(sub == 0)
    def _():
        kcat[0:sb, :] = kap_ref[0]
        kcat[sb:2 * sb, :] = ka_ref[0]
        vcat[0:sb, :] = vap_ref[0]
        vcat[sb:2 * sb, :] = va_ref[0]
        kv_first = pair < (N_HEADS // N_KV_B) // 2
        lane_row = lax.broadcasted_iota(I32, (1, LANES), 1)
        kv_lanes = jnp.where(lane_row >= HEAD_DIM, 1, 0) == jnp.where(kv_first, 0, 1)
        both_halves = lambda x: jnp.where(kv_lanes, x, pltpu.roll(x, HEAD_DIM, 1))
        kbcat[0:BLOCK, :] = both_halves(kbp_ref[0])
        kbcat[BLOCK:BLOCK + sb, :] = both_halves(kb_ref[0])
        vbcat[0:BLOCK, :] = both_halves(vbp_ref[0])
        vbcat[BLOCK:BLOCK + sb, :] = both_halves(vb_ref[0])

    lane = lax.broadcasted_iota(I32, (BLOCK, LANES), 1)
    first = lane < HEAD_DIM
    qi = lax.broadcasted_iota(I32, (2 * BLOCK, 2 * BLOCK), 0) & (BLOCK - 1)
    si = lax.broadcasted_iota(I32, (2 * BLOCK, 2 * BLOCK), 1)
    dist = qi - si + BLOCK
    band = (dist >= 0) & (dist <= BLOCK)
    own_block = si >= BLOCK

    def rows(start, size, d):
        return pl.ds(start, size) if d == 1 else pl.ds(start, size, stride=d)

    def key_mask(blk):
        return band if blk > 0 else band & (own_block | has_prev)

    halves = ATT_UNITS // ATT_UNROLL
    for p, (_, d) in enumerate(DILATIONS):
        for half in range(halves):
            @pl.when(sub == p * halves + half)
            def _(p=p, d=d, half=half):
                for u in range(half * ATT_UNROLL, (half + 1) * ATT_UNROLL):
                    blk = u // d
                    qs = blk * (BLOCK * d) + u % d
                    q = qa_ref[0, rows(qs, BLOCK, d), :] * SCALE
                    k2 = kcat[rows(sb + qs - BLOCK * d, 2 * BLOCK, d), :].astype(BF16)
                    v2 = vcat[rows(sb + qs - BLOCK * d, 2 * BLOCK, d), :].astype(BF16)
                    o0, l0, o1, l1 = _two_head_softmax_pv(jnp.where(first, q, 0.0), jnp.where(first, 0.0, q),
                                                          k2, v2, key_mask(blk))
                    o_scr[p, rows(qs, BLOCK, d), :] = jnp.where(first, o0, o1)
                    lse_scr[p, rows(qs, BLOCK, d), :] = jnp.where(first, l0, l1)

    @pl.when(sub == len(DILATIONS) * halves - 1)
    def _():
        def mix(bk, carry):
            r = pl.ds(pl.multiple_of(bk * BLOCK, BLOCK), BLOCK)
            lses = [lse_scr[p, r, :] for p in range(len(DILATIONS))]
            m = functools.reduce(jnp.maximum, lses)
            ws = [jnp.exp(l - m) for l in lses]
            oa_ref[0, r, :] = sum(w * o_scr[p, r, :] for p, w in enumerate(ws)) / sum(ws)
            return carry
        lax.fori_loop(0, ATT_UNITS, mix, 0)

    for half in range(halves):
        @pl.when(sub == len(DILATIONS) * halves + half)
        def _(half=half):
            sink = sink_ref[...]
            for blk in range(half * ATT_UNROLL, (half + 1) * ATT_UNROLL):
                r = pl.ds(blk * BLOCK, BLOCK)
                r2 = pl.ds(blk * BLOCK, 2 * BLOCK)
                q = qb_ref[0, r, :] * SCALE
                k2 = kbcat[r2, :].astype(BF16)
                v2 = vbcat[r2, :].astype(BF16)
                o0, l0, o1, l1 = _two_head_softmax_pv(jnp.where(first, q, 0.0), jnp.where(first, 0.0, q),
                                                      k2, v2, key_mask(blk))
                lse = jnp.where(first, l0, l1)
                ob_ref[0, r, :] = jnp.where(first, o0, o1) * jax.nn.sigmoid(lse - sink)


def _attention(qkvc, kvbc, sinks_lanes, batch, seq, qkvc_s, kvbc_s, cache_a_k, cache_a_v, cache_b_k, cache_b_v, sinks):
    t = batch * seq
    assert seq % ATT_SB == 0
    n_sb = seq // ATT_SB
    n_pairs = WIDTH // LANES
    cur = lambda comp: (lambda b, s, p, q: (comp * n_pairs + p, b * n_sb + s, 0))
    prev = lambda comp: (lambda b, s, p, q: (comp * n_pairs + p, b * n_sb + jnp.maximum(s - 1, 0), 0))
    blk = lambda f: pl.BlockSpec((1, ATT_SB, LANES), f)
    kvb_cur = lambda c: pl.BlockSpec((1, ATT_SB, LANES), lambda b, s, p, q: (c, b * n_sb + s, 0))
    kvb_prev = lambda c: pl.BlockSpec(
        (1, BLOCK, LANES),
        lambda b, s, p, q: (c, jnp.maximum((b * n_sb + s) * ATT_UNITS - 1, b * n_sb * ATT_UNITS), 0))
    out_spec = pl.BlockSpec((1, ATT_SB, LANES), lambda b, s, p, q: (p, b * n_sb + s, 0))
    prompt_specs = [blk(cur(0)), blk(cur(1)), blk(prev(1)), blk(cur(2)), blk(prev(2)), blk(cur(3)),
                    kvb_cur(0), kvb_prev(0), kvb_cur(1), kvb_prev(1),
                    pl.BlockSpec((1, LANES), lambda b, s, p, q: (0, p))]

    nb = qkvc_s.shape[1]
    wa = cache_a_k.shape[1]
    hps = N_HEADS // N_KV_B
    n_units = nb * N_KV_B
    n_steps = batch * n_sb * n_pairs * ATT_SUBSTEPS
    assert n_units % n_steps == 0
    ups = n_units // n_steps
    assert cache_b_k.shape[1] == WINDOW_B and all(wa >= w and wa % d == 0 for w, d in DILATIONS)
    dist = wa - np.arange(wa)
    cnt = sum(((dist % d == 0) & (dist <= w)).astype(np.float32) for w, d in DILATIONS).reshape(1, wa)
    q4u = (qkvc_s.reshape(4, N_KV_B, n_pairs // N_KV_B, nb, LANES // HEAD_DIM, HEAD_DIM)
           .transpose(3, 1, 0, 2, 4, 5).reshape(n_units, 4, hps, HEAD_DIM))
    kvnu = (kvbc_s.reshape(2, nb, N_KV_B, HEAD_DIM).transpose(1, 2, 0, 3).reshape(n_units, 2, 1, HEAD_DIM))
    sink_u = jnp.tile(sinks.reshape(N_KV_B, hps, 1), (nb, 1, 1))
    to_units = lambda c: jnp.transpose(c, (0, 2, 3, 1)).reshape(n_units, -1, HEAD_DIM, c.shape[1])
    step = lambda b, s, p, q: ((b * n_sb + s) * n_pairs + p) * ATT_SUBSTEPS + q
    unit_blk = lambda shape: pl.BlockSpec((ups,) + shape, lambda b, s, p, q: (step(b, s, p, q),) + (0,) * len(shape))
    comp_blk = lambda c, shape: pl.BlockSpec((ups, 1) + shape,
                                             lambda b, s, p, q: (step(b, s, p, q), c) + (0,) * len(shape))
    sample_specs = [pl.BlockSpec((1, wa), lambda b, s, p, q: (0, 0)),
                    comp_blk(0, (hps, HEAD_DIM)), comp_blk(1, (hps, HEAD_DIM)), comp_blk(2, (hps, HEAD_DIM)),
                    comp_blk(3, (hps, HEAD_DIM)), comp_blk(0, (1, HEAD_DIM)), comp_blk(1, (1, HEAD_DIM)),
                    unit_blk((hps, 1)),
                    unit_blk((hps, HEAD_DIM, wa)), unit_blk((hps, HEAD_DIM, wa)),
                    unit_blk((1, HEAD_DIM, WINDOW_B)), unit_blk((1, HEAD_DIM, WINDOW_B))]
    assert len(sample_specs) == N_SAMPLE_REFS
    oas, obs, oa, ob = pl.pallas_call(
        functools.partial(_attn_kernel, ups),
        out_shape=(jax.ShapeDtypeStruct((n_units, hps, HEAD_DIM), F32),
                   jax.ShapeDtypeStruct((n_units, hps, HEAD_DIM), F32),
                   jax.ShapeDtypeStruct((n_pairs, t, LANES), F32),
                   jax.ShapeDtypeStruct((n_pairs, t, LANES), F32)),
        grid=(batch, n_sb, n_pairs, ATT_SUBSTEPS),
        in_specs=sample_specs + prompt_specs,
        out_specs=(unit_blk((hps, HEAD_DIM)), unit_blk((hps, HEAD_DIM)), out_spec, out_spec),
        scratch_shapes=[
            pltpu.VMEM((2 * ATT_SB, LANES), F32), pltpu.VMEM((2 * ATT_SB, LANES), F32),
            pltpu.VMEM((BLOCK + ATT_SB, LANES), F32), pltpu.VMEM((BLOCK + ATT_SB, LANES), F32),
            pltpu.VMEM((len(DILATIONS), ATT_SB, LANES), F32), pltpu.VMEM((len(DILATIONS), ATT_SB, LANES), F32)],
        compiler_params=_cparams(("parallel", "parallel", "arbitrary", "arbitrary"), 56),
        name="attention",
    )(jnp.asarray(cnt), q4u, q4u, q4u, q4u, kvnu, kvnu, sink_u,
      to_units(cache_a_k), to_units(cache_a_v), to_units(cache_b_k), to_units(cache_b_v),
      qkvc, qkvc, qkvc, qkvc, qkvc, qkvc, kvbc, kvbc, kvbc, kvbc, sinks_lanes)
    chunked = lambda o: o.reshape(nb, n_pairs, LANES).transpose(1, 0, 2)
    return oa, ob, chunked(oas), chunked(obs)


def _hi_dot(a, b, dims=(((1,), (0,)), ((), ()))):
    return lax.dot_general(a, b, dims, preferred_element_type=F32, precision=lax.Precision.HIGHEST)


def _sample_unit(j, cnt_ref, qa_ref, kn_ref, vn_ref, qb_ref, kbn_ref, vbn_ref, sink_ref,
                 kt_ref, vt_ref, kbt_ref, vbt_ref, oa_ref, ob_ref):
    hps = N_HEADS // N_KV_B
    nt = (((1,), (1,)), ((), ()))
    head = lax.broadcasted_iota(I32, (hps, 1), 0)
    cnt = cnt_ref[...]
    n_pat = float(len(DILATIONS))

    qa = qa_ref[j, 0] * SCALE
    qa_bf = qa.astype(BF16)
    s = jnp.zeros((hps, cnt.shape[1]), F32)
    for hl in range(hps):
        s_hl = jnp.dot(qa_bf, kt_ref[j, hl].astype(BF16), preferred_element_type=F32)
        s = jnp.where(head == hl, s_hl, s)
    s = jnp.where(cnt > 0.0, s, MASK_VALUE)
    s_new = jnp.sum(qa * kn_ref[j, 0], axis=1, keepdims=True)
    m = jnp.maximum(jnp.max(s, axis=1, keepdims=True), s_new)
    e = cnt * jnp.exp(s - m)
    e_new = n_pat * jnp.exp(s_new - m)
    den = jnp.sum(e, axis=1, keepdims=True) + e_new
    e_bf = e.astype(BF16)
    pv = jnp.zeros((hps, HEAD_DIM), F32)
    for hl in range(hps):
        o_hl = lax.dot_general(e_bf, vt_ref[j, hl].astype(BF16), nt, preferred_element_type=F32)
        pv = jnp.where(head == hl, o_hl, pv)
    oa_ref[j] = (pv + e_new * vn_ref[j, 0]) / den

    qb = qb_ref[j, 0] * SCALE
    sb = _hi_dot(qb, kbt_ref[j, 0])
    sn = jnp.sum(qb * kbn_ref[j, 0], axis=1, keepdims=True)
    mm = jnp.maximum(jnp.max(sb, axis=1, keepdims=True), sn)
    eb = jnp.exp(sb - mm)
    en = jnp.exp(sn - mm)
    l = jnp.sum(eb, axis=1, keepdims=True) + en
    ob = (_hi_dot(eb, vbt_ref[j, 0], nt) + en * vbn_ref[j, 0]) / l
    lse = mm + jnp.log(l)
    ob_ref[j] = ob * jax.nn.sigmoid(lse - sink_ref[j])


def _split_bf16(x):
    hi = x.astype(BF16)
    lo = (x - hi.astype(F32)).astype(BF16)
    return hi, lo


def _post_first_kernel(n_valid, *refs):
    i = pl.program_id(0)

    @pl.when(i < n_valid)
    def _():
        _post_body(*refs)

    @pl.when(i >= n_valid)
    def _():
        for o_ref in refs[-4:]:
            o_ref[...] = jnp.zeros(o_ref.shape, o_ref.dtype)


def _post_append_kernel(x1_all, h2_all, ridx_all, rgate_all, *refs):
    del x1_all, h2_all, ridx_all, rgate_all
    _post_body(*refs)


def _post_body(oa_ref, ob_ref, x_ref, ga_ref, gb_ref, w_ref, b_ref, gf_ref, wr_ref, br_ref,
               x1_ref, h2_ref, ridx_ref, rgate_ref):
    unchunk = lambda ref: jnp.concatenate([ref[c] for c in range(ref.shape[0])], axis=1)
    na = _rmsnorm(unchunk(oa_ref), ga_ref[...]).astype(BF16)
    nb = _rmsnorm(unchunk(ob_ref), gb_ref[...]).astype(BF16)
    c = jnp.concatenate([na, nb], axis=1)
    x1 = x_ref[...] + jnp.dot(c, w_ref[...], preferred_element_type=F32) + b_ref[...]
    x1_ref[...] = x1
    h2 = _rmsnorm(x1, gf_ref[...])
    h2_ref[...] = h2

    h_hi, h_lo = _split_bf16(h2)
    w_hi, w_lo = _split_bf16(wr_ref[...])
    dot = lambda a, b: jnp.dot(a, b, preferred_element_type=F32)
    logits = dot(h_hi, w_hi) + (dot(h_hi, w_lo) + dot(h_lo, w_hi)) + br_ref[...]

    tm = logits.shape[0]
    eidx = lax.broadcasted_iota(I32, (tm, N_EXPERTS), 1)
    lane = lax.broadcasted_iota(I32, (tm, LANES), 1)
    work = logits
    vals, idxs = [], []
    for _ in range(TOP_K):
        v = jnp.max(work, axis=1, keepdims=True)
        i = jnp.min(jnp.where(work == v, eidx, N_EXPERTS), axis=1, keepdims=True)
        vals.append(v)
        idxs.append(i)
        work = jnp.where(eidx == i, -jnp.inf, work)
    es = [jnp.exp(v - vals[0]) for v in vals]
    den = sum(es)
    ridx = jnp.zeros((tm, LANES), I32)
    rgate = jnp.zeros((tm, LANES), F32)
    for k in range(TOP_K):
        ridx = jnp.where(lane == k, idxs[k], ridx)
        rgate = jnp.where(lane == k, es[k] / den, rgate)
    ridx_ref[...] = ridx
    rgate_ref[...] = rgate


def _post_attention(oa, ob, x, g_out_a, g_out_b, w_out_bf, b_out, g_ffn, w_router, b_router, tm,
                    out_rows=None, append_to=None, row0=0):
    t = x.shape[0]
    n_valid = t // tm
    n_pairs = WIDTH // LANES
    clamp = lambda i: jnp.minimum(i, n_valid - 1)
    chunked = pl.BlockSpec((n_pairs, tm, LANES), lambda i: (0, clamp(i), 0))
    full = lambda r, c: pl.BlockSpec((r, c), lambda i: (0, 0))
    in_specs = [chunked, chunked, pl.BlockSpec((tm, D_MODEL), lambda i: (clamp(i), 0)),
                full(1, WIDTH), full(1, WIDTH), full(D_MODEL, D_MODEL), full(1, D_MODEL), full(1, D_MODEL),
                full(D_MODEL, N_EXPERTS), full(1, N_EXPERTS)]
    args = (oa, ob, x, g_out_a, g_out_b, w_out_bf, b_out, g_ffn, w_router, b_router)
    if append_to is None:
        rows = t if out_rows is None else out_rows
        assert rows % tm == 0 and t % tm == 0
        kernel_fn, aliases, blk0 = functools.partial(_post_first_kernel, n_valid), {}, 0
    else:
        rows = append_to[0].shape[0]
        assert row0 % tm == 0 and t % tm == 0
        kernel_fn, aliases, blk0 = _post_append_kernel, {k: k for k in range(4)}, row0 // tm
        in_specs = [pl.BlockSpec(memory_space=pl.ANY)] * 4 + in_specs
        args = tuple(append_to) + args
    n_steps = rows // tm if append_to is None else n_valid
    outblk = lambda w: pl.BlockSpec((tm, w), lambda i: (blk0 + i, 0))
    return pl.pallas_call(
        kernel_fn,
        out_shape=(jax.ShapeDtypeStruct((rows, D_MODEL), F32), jax.ShapeDtypeStruct((rows, D_MODEL), F32),
                   jax.ShapeDtypeStruct((rows, LANES), I32), jax.ShapeDtypeStruct((rows, LANES), F32)),
        grid=(n_steps,),
        in_specs=in_specs,
        out_specs=(outblk(D_MODEL), outblk(D_MODEL), outblk(LANES), outblk(LANES)),
        input_output_aliases=aliases,
        compiler_params=_cparams(("arbitrary",), 48),
        name="post_attention",
    )(*args)


def _dispatch_kernel(dest_ref, h_hbm, o_ref, tok_ref, buf, sem):
    i = pl.program_id(0)

    @pl.when(i == 0)
    def _():
        def clear(p, carry):
            tok_ref[p] = 0
            return carry
        lax.fori_loop(0, tok_ref.shape[0], clear, 0, unroll=8)

        def place(t, carry):
            for k in range(TOP_K):
                tok_ref[dest_ref[t * TOP_K + k]] = t
            return carry
        lax.fori_loop(0, dest_ref.shape[0] // TOP_K, place, 0, unroll=4)

    def issue(step, slot):
        base = step * DISPATCH_ROWS

        def body(r8, carry):
            for k in range(8):
                r = r8 * 8 + k
                pltpu.make_async_copy(h_hbm.at[pl.ds(tok_ref[base + r], 1)],
                                      buf.at[slot, pl.ds(r, 1)], sem.at[slot]).start(priority=k % 2)
            return carry
        lax.fori_loop(0, DISPATCH_ROWS // 8, body, 0)

    @pl.when(i == 0)
    def _():
        issue(0, 0)

    @pl.when(i + 1 < pl.num_programs(0))
    def _():
        issue(i + 1, (i + 1) % 2)

    slot = i % 2
    pltpu.make_async_copy(h_hbm.at[pl.ds(0, DISPATCH_ROWS)], buf.at[slot], sem.at[slot]).wait()
    o_ref[...] = buf[slot].astype(BF16)


def _dispatch(dest, h2, p):
    return pl.pallas_call(
        _dispatch_kernel,
        out_shape=jax.ShapeDtypeStruct((p, D_MODEL), BF16),
        grid_spec=pltpu.PrefetchScalarGridSpec(
            num_scalar_prefetch=1,
            grid=(p // DISPATCH_ROWS,),
            in_specs=[pl.BlockSpec(memory_space=pl.ANY)],
            out_specs=pl.BlockSpec((DISPATCH_ROWS, D_MODEL), lambda i, dst: (i, 0)),
            scratch_shapes=[pltpu.SMEM((p,), I32),
                            pltpu.VMEM((2, DISPATCH_ROWS, D_MODEL), F32), pltpu.SemaphoreType.DMA((2,))],
        ),
        compiler_params=_cparams(("arbitrary",), 32),
        name="moe_dispatch",
    )(dest, h2)


def _moe_kernel(ie_ref, iblk_ref, insub_ref, iact_ref, used_ref,
                x_hbm, wg_ref, wu_ref, wd_ref, bg_ref, bu_ref, bd_ref, y_hbm,
                xbuf, act, ystage, xsem, ysem):
    it = pl.program_id(0)
    s = pl.program_id(1)
    nsub = insub_ref[it]
    row0 = iblk_ref[it] * MOE_SUB

    xslot = it % 2

    def x_copies(item, op):
        first = iblk_ref[item] * MOE_SUB

        def body(j, c):
            cp = pltpu.make_async_copy(x_hbm.at[pl.ds(first + j * MOE_SUB, MOE_SUB)],
                                       xbuf.at[item % 2, pl.ds(j * MOE_SUB, MOE_SUB)], xsem.at[item % 2])
            cp.start() if op == "start" else cp.wait()
            return c
        lax.fori_loop(0, insub_ref[item], body, 0)

    def for_row_chunks(fn):
        big = nsub // 4

        def body(j, c):
            fn(pl.multiple_of(j * (4 * MOE_SUB), 4 * MOE_SUB), 4 * MOE_SUB)
            return c
        lax.fori_loop(0, big, body, 0)
        rem = nsub - 4 * big
        has2 = rem >= 2

        @pl.when(has2)
        def _():
            fn(pl.multiple_of(big * (4 * MOE_SUB), MOE_SUB), 2 * MOE_SUB)

        @pl.when((rem & 1) == 1)
        def _():
            fn(pl.multiple_of((4 * big + jnp.where(has2, 2, 0)) * MOE_SUB, MOE_SUB), MOE_SUB)

    @pl.when((s == 0) & (it == 0))
    def _():
        x_copies(0, "start")

    @pl.when(s == 0)
    def _():
        x_copies(it, "wait")

    @pl.when((s == 1) & (it + 1 < pl.num_programs(0)))
    def _():
        x_copies(it + 1, "start")

    @pl.when((s < MOE_F) & (nsub > 0))
    def _():
        bg, bu = bg_ref[...], bu_ref[...]

        def gate_up(r0, rows):
            x = xbuf[xslot, pl.ds(r0, rows), :]
            g = jnp.dot(x, wg_ref[...].astype(BF16), preferred_element_type=F32) + bg
            u = jnp.dot(x, wu_ref[...].astype(BF16), preferred_element_type=F32) + bu
            g = jnp.minimum(g, SWIGLU_LIMIT)
            u = jnp.clip(u, -SWIGLU_LIMIT, SWIGLU_LIMIT)
            a = g * jax.nn.sigmoid(SWIGLU_ALPHA * g) * (u + 1.0)
            act[s, pl.ds(r0, rows), :] = a.astype(BF16)
        for_row_chunks(gate_up)

    @pl.when((s >= MOE_F) & (nsub > 0))
    def _():
        n = s - MOE_F
        slot = n % 2
        bd = bd_ref[...]

        def y_copy(step, r0, rows):
            c0 = pl.multiple_of(step * MOE_TF, MOE_TF)
            return pltpu.make_async_copy(
                ystage.at[step % 2, pl.ds(r0, rows), :],
                y_hbm.at[pl.ds(row0 + r0, rows), pl.ds(c0, MOE_TF)], ysem.at[step % 2])

        def down(r0, rows):
            a = jnp.concatenate([act[f, pl.ds(r0, rows), :] for f in range(MOE_F)], axis=1)
            ystage[slot, pl.ds(r0, rows), :] = (
                jnp.dot(a, wd_ref[...].astype(BF16), preferred_element_type=F32) + bd)
            y_copy(n, r0, rows).start()
        for_row_chunks(down)

        @pl.when(n > 0)
        def _():
            for_row_chunks(lambda r0, rows: y_copy(n - 1, r0, rows).wait())

        @pl.when(n == MOE_F - 1)
        def _():
            for_row_chunks(lambda r0, rows: y_copy(n, r0, rows).wait())

    @pl.when((it == pl.num_programs(0) - 1) & (s == pl.num_programs(1) - 1))
    def _():
        n_blocks = y_hbm.shape[0] // MOE_SUB
        ystage[0, pl.ds(0, MOE_SUB), :] = jnp.zeros((MOE_SUB, MOE_TF), F32)

        def z_copy(b, n):
            return pltpu.make_async_copy(
                ystage.at[0, pl.ds(0, MOE_SUB), :],
                y_hbm.at[pl.ds(b * MOE_SUB, MOE_SUB), pl.ds(n * MOE_TF, MOE_TF)], ysem.at[0])

        def fill(b, c):
            for n in range(MOE_F):
                z_copy(b, n).start()
            for n in range(MOE_F):
                z_copy(b, n).wait()
            return c
        lax.fori_loop(used_ref[0], n_blocks, fill, 0)


def _moe_experts(items, x_sorted, w_gate_up, b_gate_up, w_down, b_down):
    item_e, item_blk, item_nsub, item_act, used_blocks = items
    n_items = item_e.shape[0]
    p = x_sorted.shape[0]
    rmax = MOE_MAX_SUB * MOE_SUB
    last = MOE_F - 1
    f_of = lambda s, act: jnp.where(act == 1, jnp.minimum(s, last), last)
    n_of = lambda s, act: jnp.where(act == 1, jnp.maximum(s - MOE_F, 0), last)
    wg_map = lambda i, s, ie, ib, ins, ia, iu: (ie[i], 0, f_of(s, ia[i]))
    wu_map = lambda i, s, ie, ib, ins, ia, iu: (ie[i], 0, MOE_F + f_of(s, ia[i]))
    wd_map = lambda i, s, ie, ib, ins, ia, iu: (ie[i], 0, n_of(s, ia[i]))
    bgu = b_gate_up.reshape(N_EXPERTS, 1, 2 * D_FF)
    bd = b_down.reshape(N_EXPERTS, 1, D_MODEL)
    return pl.pallas_call(
        _moe_kernel,
        out_shape=jax.ShapeDtypeStruct((p, D_MODEL), F32),
        grid_spec=pltpu.PrefetchScalarGridSpec(
            num_scalar_prefetch=5,
            grid=(n_items, 2 * MOE_F),
            in_specs=[
                pl.BlockSpec(memory_space=pl.ANY),
                pl.BlockSpec((None, D_MODEL, MOE_TF), wg_map),
                pl.BlockSpec((None, D_MODEL, MOE_TF), wu_map),
                pl.BlockSpec((None, D_FF, MOE_TF), wd_map),
                pl.BlockSpec((None, 1, MOE_TF), wg_map),
                pl.BlockSpec((None, 1, MOE_TF), wu_map),
                pl.BlockSpec((None, 1, MOE_TF), wd_map),
            ],
            out_specs=pl.BlockSpec(memory_space=pl.ANY),
            scratch_shapes=[
                pltpu.VMEM((2, rmax, D_MODEL), BF16),
                pltpu.VMEM((MOE_F, rmax, MOE_TF), BF16),
                pltpu.VMEM((2, rmax, MOE_TF), F32),
                pltpu.SemaphoreType.DMA((2,)),
                pltpu.SemaphoreType.DMA((2,)),
            ],
        ),
        compiler_params=_cparams(("arbitrary", "arbitrary"), 56),
        name="moe_experts",
    )(item_e, item_blk, item_nsub, item_act, used_blocks, x_sorted, w_gate_up, w_gate_up, w_down, bgu, bgu, bd)


def _tail_kernel(n_first, dest_ref, y_hbm, gate_ref, x1_ref, p_ref, wg_ref, wp_ref, gp_ref, gf_ref,
                 o1_ref, o2_ref, ybuf, sem):
    i = pl.program_id(0)

    def issue(step, slot):
        base = step * (TAIL_ROWS * TOP_K)

        def body(r, carry):
            for k in range(TOP_K):
                pltpu.make_async_copy(y_hbm.at[pl.ds(dest_ref[base + r * TOP_K + k], 1)],
                                      ybuf.at[slot, k, pl.ds(r, 1)], sem.at[slot]).start(priority=k % 2)
            return carry
        lax.fori_loop(0, TAIL_ROWS, body, 0, unroll=2)

    @pl.when(i == 0)
    def _():
        issue(0, 0)

    @pl.when(i + 1 < pl.num_programs(0))
    def _():
        issue(i + 1, (i + 1) % 2)

    slot = i % 2
    for k in range(TOP_K):
        pltpu.make_async_copy(y_hbm.at[pl.ds(0, TAIL_ROWS)], ybuf.at[slot, k], sem.at[slot]).wait()

    gate = gate_ref[...]
    moe = ybuf[slot, 0] * gate[:, 0:1]
    for k in range(1, TOP_K):
        moe = moe + ybuf[slot, k] * gate[:, k:k + 1]
    x2 = x1_ref[...] + moe
    h3 = _rmsnorm(x2, gp_ref[...]).astype(BF16)
    ple_gate = jax.nn.sigmoid(jnp.dot(h3, wg_ref[...], preferred_element_type=F32))
    ple = jnp.dot(p_ref[...].astype(BF16), wp_ref[...], preferred_element_type=F32)
    x3 = x2 + ple_gate * ple
    y = _rmsnorm(x3, gf_ref[...])

    @pl.when(i < n_first)
    def _():
        o1_ref[...] = y

    @pl.when(i >= n_first)
    def _():
        o2_ref[...] = y


def _tail(dest, y_sorted, rgate, x1, p_all, w_ple_gate_bf, w_ple_proj_bf, g_ple, g_final, t_first):
    t = dest.shape[0] // TOP_K
    assert t_first % TAIL_ROWS == 0 and t % TAIL_ROWS == 0
    n_first = t_first // TAIL_ROWS
    rowblk = lambda w: pl.BlockSpec((TAIL_ROWS, w), lambda i, d: (i, 0))
    full = lambda r, c: pl.BlockSpec((r, c), lambda i, d: (0, 0))
    return pl.pallas_call(
        functools.partial(_tail_kernel, n_first),
        out_shape=(jax.ShapeDtypeStruct((t_first, D_MODEL), F32),
                   jax.ShapeDtypeStruct((t - t_first, D_MODEL), F32)),
        grid_spec=pltpu.PrefetchScalarGridSpec(
            num_scalar_prefetch=1,
            grid=(t // TAIL_ROWS,),
            in_specs=[
                pl.BlockSpec(memory_space=pl.ANY),
                rowblk(LANES), rowblk(D_MODEL), rowblk(PLE_DIM),
                full(D_MODEL, D_MODEL), full(PLE_DIM, D_MODEL), full(1, D_MODEL), full(1, D_MODEL)],
            out_specs=(
                pl.BlockSpec((TAIL_ROWS, D_MODEL), lambda i, d: (jnp.minimum(i, n_first - 1), 0)),
                pl.BlockSpec((TAIL_ROWS, D_MODEL), lambda i, d: (jnp.maximum(i - n_first, 0), 0))),
            scratch_shapes=[pltpu.VMEM((2, TOP_K, TAIL_ROWS, D_MODEL), F32), pltpu.SemaphoreType.DMA((2,))],
        ),
        compiler_params=_cparams(("arbitrary",), 48),
        name="moe_tail",
    )(dest, y_sorted, rgate, x1, p_all, w_ple_gate_bf, w_ple_proj_bf, g_ple, g_final)


def _routing(ridx, rgate_unused, t):
    del rgate_unused
    m = t * TOP_K
    flat_e = ridx[:, :TOP_K].reshape(m)
    onehot = (flat_e[:, None] == jnp.arange(N_EXPERTS, dtype=I32)[None, :]).astype(I32)
    csum = jnp.cumsum(onehot, axis=0)
    rank = jnp.sum((csum - onehot) * onehot, axis=1)
    counts = csum[-1]
    nblk = (counts + MOE_SUB - 1) // MOE_SUB
    blk_end = jnp.cumsum(nblk)
    blk_start = blk_end - nblk
    dest = blk_start[flat_e] * MOE_SUB + rank
    n_blocks = -(-m // MOE_SUB) + N_EXPERTS
    n_blocks = -(-n_blocks // (DISPATCH_ROWS // MOE_SUB)) * (DISPATCH_ROWS // MOE_SUB)
    p = n_blocks * MOE_SUB

    n_items = (n_blocks + (MOE_MAX_SUB - 1) * N_EXPERTS) // MOE_MAX_SUB
    items_per_e = (nblk + MOE_MAX_SUB - 1) // MOE_MAX_SUB
    item_end = jnp.cumsum(items_per_e)
    total = item_end[-1]
    ids = jnp.arange(n_items, dtype=I32)
    active = ids < total
    ids_c = jnp.minimum(ids, total - 1)
    item_e = jnp.searchsorted(item_end, ids_c, side="right").astype(I32)
    local = ids_c - (item_end - items_per_e)[item_e]
    item_blk = blk_start[item_e] + local * MOE_MAX_SUB
    item_nsub = jnp.where(active, jnp.minimum(MOE_MAX_SUB, nblk[item_e] - local * MOE_MAX_SUB), 0)
    items = (item_e, item_blk.astype(I32), item_nsub.astype(I32), active.astype(I32),
             blk_end[-1:].astype(I32))
    return p, dest.astype(I32), items


def kernel(x_prompt, x_sample, cache_a_k, cache_a_v, cache_b_k, cache_b_v, p_prompt, p_sample, g_attn, w_in, b_in, sinks, g_out_a, g_out_b, w_out, b_out, g_ffn, w_router, b_router, w_gate_up, b_gate_up, w_down, b_down, g_ple, w_ple_gate, w_ple_proj, g_final):
    batch, seq, _ = x_prompt.shape
    nb, nseq, _ = x_sample.shape
    assert nseq == 1 and g_attn.shape[0] == 1
    tp, ts = batch * seq, nb * nseq
    row = lambda v: v.reshape(1, -1)

    w_in_bf = _cast_bf16(w_in[0])
    w_out_bf = _cast_bf16(w_out[0])
    w_pg_bf = _cast_bf16(w_ple_gate[0])
    w_pp_bf = _cast_bf16(w_ple_proj[0])

    tabs_p = _rope_tables(jnp.arange(seq, dtype=I32))
    tabs_s = _rope_tables(jnp.full((ts,), PAST_LEN, I32))
    sinks_lanes = jnp.repeat(sinks[0], HEAD_DIM).reshape(1, WIDTH)

    tm_p = 512
    xp = x_prompt.reshape(tp, D_MODEL)
    xs = x_sample.reshape(ts, D_MODEL)
    win_a, win_b = min(DILATIONS[-1][0], seq), min(WINDOW_B, seq)
    qkvc_p, kvbc_p, kt_p, vt_p, kvbt_p = _qkv_proj(xp, row(g_attn), w_in_bf, row(b_in), tabs_p, tm_p,
                                                    batch, win_a, win_b)
    qkvc_s, kvbc_s, kt_s, vt_s, kvbt_s = _qkv_proj(xs, row(g_attn), w_in_bf, row(b_in), tabs_s, ts, 1, ts, ts)

    oa_p, ob_p, oa_s, ob_s = _attention(qkvc_p, kvbc_p, sinks_lanes, batch, seq, qkvc_s, kvbc_s,
                                        cache_a_k[0], cache_a_v[0], cache_b_k[0], cache_b_v[0], sinks[0])

    post = functools.partial(_post_attention, g_out_a=row(g_out_a), g_out_b=row(g_out_b), w_out_bf=w_out_bf,
                             b_out=row(b_out), g_ffn=row(g_ffn), w_router=w_router[0], b_router=row(b_router))
    t = tp + ts
    tm_post = 256
    merged = post(oa_p, ob_p, xp, tm=tm_post, out_rows=-(-t // tm_post) * tm_post)
    x1, h2, ridx, rgate = post(oa_s, ob_s, xs, tm=ts, append_to=merged, row0=tp)

    cat = lambda a, b: jnp.concatenate([a, b], axis=0)
    n_slots, dest, items = _routing(ridx[:t], rgate, t)
    x_sorted = _dispatch(dest, h2, n_slots)
    y_sorted = _moe_experts(items, x_sorted, w_gate_up[0], b_gate_up[0], w_down[0], b_down[0])
    p_all = cat(p_prompt[0].reshape(tp, PLE_DIM), p_sample[0].reshape(ts, PLE_DIM))
    y_p, y_s = _tail(dest, y_sorted, rgate, x1, p_all, w_pg_bf, w_pp_bf, row(g_ple), row(g_final), tp)

    y_prompt = y_p.reshape(batch, seq, D_MODEL)
    y_sample = y_s.reshape(nb, nseq, D_MODEL)

    def rows_out(t3, heads):
        n, _, rows = t3.shape
        return t3.reshape(n, heads, HEAD_DIM, rows).transpose(0, 3, 1, 2)[None]

    def sample_out(t3, heads):
        return t3.reshape(heads, HEAD_DIM, nb).transpose(2, 0, 1)[None, :, None]

    return (y_prompt, y_sample,
            rows_out(kt_p, N_HEADS), rows_out(vt_p, N_HEADS),
            rows_out(kvbt_p[:, :KV_WIDTH_B], N_KV_B), rows_out(kvbt_p[:, KV_WIDTH_B:], N_KV_B),
            sample_out(kt_s, N_HEADS), sample_out(vt_s, N_HEADS),
            sample_out(kvbt_s[:, :KV_WIDTH_B], N_KV_B), sample_out(kvbt_s[:, KV_WIDTH_B:], N_KV_B))
```
